```python
import functools
import jax, jax.numpy as jnp
from jax import lax
import numpy as np

D_MODEL = 1024
BATCH = 2
SEQ = 8192
DEPTH = 4
DEC_BATCH = 32
DEC_SEQ = 4
PAST_LEN = 8192
PAGE_SIZE = 128

C_CONV = 512
CONV_W = 31
N_HEADS = 8
N_KV_HEADS = 4
HEAD_DIM = 64
Q_PER_KV = N_HEADS // N_KV_HEADS
ATTN_W = N_HEADS * HEAD_DIM
N_IDX_HEADS = 8
IDX_DIM = 64
TOPK_MAX = 256
Q_BLOCK = 128
N_GROUPS = 4
EXP_PER_GROUP = 8
N_EXPERTS = N_GROUPS * EXP_PER_GROUP
EXPERT_TOP_K = 2
D_EXPERT = 256
MOE_BLOCK = 128
DN_ALPHA = (2 * DEPTH) ** 0.25
DN_BETA = (8 * DEPTH) ** -0.25
LN_EPS = 1e-5
SPLIT_SIZES = (C_CONV, C_CONV, N_HEADS * HEAD_DIM, N_KV_HEADS * HEAD_DIM, N_KV_HEADS * HEAD_DIM,
               N_IDX_HEADS * IDX_DIM, IDX_DIM, N_IDX_HEADS, 2 * D_MODEL)
N_IN = sum(SPLIT_SIZES)
SPLIT_POINTS = tuple(np.cumsum(SPLIT_SIZES)[:-1].tolist())

kernel_name = 'hybrid_conformer_dsa_hmoe_step'


def layer_norm(x, g, b):
    xf = x.astype(jnp.float32)
    mu = jnp.mean(xf, axis=-1, keepdims=True)
    var = jnp.mean(jnp.square(xf - mu), axis=-1, keepdims=True)
    y = (xf - mu) * lax.rsqrt(var + LN_EPS) * g.astype(jnp.float32) + b.astype(jnp.float32)
    return y.astype(x.dtype)


def gather_rows(rows, idx):
    return jax.vmap(lambda r, i: r[i])(rows, idx)


def project_inputs(x, w_in, b_in):
    B, S = x.shape[:2]
    h = jnp.einsum('bsd,dn->bsn', x, w_in) + b_in
    ca, cb, q, k, v, qi, ki, wi, gates = jnp.split(h, list(SPLIT_POINTS), axis=-1)
    u = ca * jax.nn.sigmoid(cb)
    q = q.reshape(B, S, N_HEADS, HEAD_DIM)
    k = k.reshape(B, S, N_KV_HEADS, HEAD_DIM)
    v = v.reshape(B, S, N_KV_HEADS, HEAD_DIM)
    qi = qi.reshape(B, S, N_IDX_HEADS, IDX_DIM)
    wi = wi * (N_IDX_HEADS ** -0.5)
    g_conv, g_attn = jnp.split(jax.nn.sigmoid(gates), 2, axis=-1)
    return u, q, k, v, qi, ki, wi, g_conv, g_attn


def conv_branch(u_hist, conv_w, conv_b, ln_g, ln_b, w_conv_out):
    y = lax.conv_general_dilated(u_hist, conv_w[:, None, :], window_strides=(1,), padding='VALID',
                                 dimension_numbers=('NWC', 'WIO', 'NWC'),
                                 feature_group_count=C_CONV) + conv_b
    y = jax.nn.silu(layer_norm(y, ln_g, ln_b))
    return jnp.einsum('bsc,cd->bsd', y, w_conv_out)


def indexer_select(qi, wi, kidx, qpos, k_sel):
    dots = jnp.einsum('bqhd,bsd->bqhs', qi, kidx) * (IDX_DIM ** -0.5)
    score = jnp.einsum('bqhs,bqh->bqs', jax.nn.relu(dots), wi).astype(jnp.float32)
    kpos = jnp.arange(kidx.shape[1])
    score = jnp.where(kpos[None, None, :] <= qpos[None, :, None], score, -jnp.inf)
    return lax.top_k(score, k_sel)[1]


def sparse_attend(q, k_sel, v_sel, valid):
    B, Q = q.shape[:2]
    qg = q.reshape(B, Q, N_KV_HEADS, Q_PER_KV, HEAD_DIM)
    s = jnp.einsum('bqkgd,bqnkd->bqkgn', qg, k_sel).astype(jnp.float32) * (HEAD_DIM ** -0.5)
    s = jnp.where(valid[:, :, None, None, :], s, -jnp.inf)
    p = jax.nn.softmax(s, axis=-1).astype(v_sel.dtype)
    o = jnp.einsum('bqkgn,bqnkd->bqkgd', p, v_sel)
    return o.reshape(B, Q, ATTN_W)


def dsa_prompt(q, k, v, qi, ki, wi):
    B, T = q.shape[:2]
    k_sel = min(TOPK_MAX, T // 4)

    def block(i):
        s0 = i * Q_BLOCK
        sl = lambda a: lax.dynamic_slice_in_dim(a, s0, Q_BLOCK, axis=1)
        qpos = s0 + jnp.arange(Q_BLOCK)
        idx = indexer_select(sl(qi), sl(wi), ki, qpos, k_sel)
        valid = idx <= qpos[None, :, None]
        return sparse_attend(sl(q), gather_rows(k, idx), gather_rows(v, idx), valid)

    o = lax.map(block, jnp.arange(T // Q_BLOCK))
    return jnp.swapaxes(o, 0, 1).reshape(B, T, ATTN_W)


def dsa_sample(q, k, v, qi, ki, wi, k_pool, v_pool, ki_pool, page_table):
    DB, S = q.shape[:2]
    past = page_table.shape[1] * PAGE_SIZE
    k_sel = min(TOPK_MAX, (past + S) // 4)
    ki_past = ki_pool[page_table].reshape(DB, past, IDX_DIM)
    ki_all = jnp.concatenate([ki_past, ki], axis=1)
    qpos = past + jnp.arange(S)
    idx = indexer_select(qi, wi, ki_all, qpos, k_sel)
    is_past = idx < past
    pidx = jnp.minimum(idx, past - 1)
    phys = gather_rows(page_table, pidx // PAGE_SIZE)
    off = pidx % PAGE_SIZE
    nidx = jnp.clip(idx - past, 0, S - 1)
    sel = lambda pool, new: jnp.where(is_past[..., None, None], pool[phys, off], gather_rows(new, nidx))
    valid = idx <= qpos[None, :, None]
    return sparse_attend(q, sel(k_pool, k), sel(v_pool, v), valid)


def grouped_experts(xf, experts, gates, w_gate, w_up, w_down):
    n_tok = xf.shape[0]
    n_asg = n_tok * EXPERT_TOP_K
    flat_e = experts.reshape(-1)
    flat_t = jnp.repeat(jnp.arange(n_tok, dtype=jnp.int32), EXPERT_TOP_K)
    flat_g = gates.reshape(-1)
    order = jnp.argsort(flat_e)
    se, st, sg = flat_e[order], flat_t[order], flat_g[order]
    counts = jnp.bincount(flat_e, length=N_EXPERTS)
    padded = (counts + MOE_BLOCK - 1) // MOE_BLOCK * MOE_BLOCK
    pad_end = jnp.cumsum(padded)
    pad_start = pad_end - padded
    start = jnp.cumsum(counts) - counts
    dest = pad_start[se] + jnp.arange(n_asg) - start[se]
    n_blocks = -(-n_asg // MOE_BLOCK) + N_EXPERTS
    slot_tok = jnp.full((n_blocks * MOE_BLOCK,), n_tok, jnp.int32).at[dest].set(st)
    blk_exp = jnp.minimum(jnp.searchsorted(pad_end, jnp.arange(n_blocks) * MOE_BLOCK, side='right'), N_EXPERTS - 1)
    x_pad = jnp.concatenate([xf, jnp.zeros((1, xf.shape[1]), xf.dtype)], axis=0)
    xb = x_pad[slot_tok].reshape(n_blocks, MOE_BLOCK, xf.shape[1])

    def expert_block(args):
        xblk, e = args
        h = jax.nn.silu(xblk @ w_gate[e]) * (xblk @ w_up[e])
        return h @ w_down[e]

    yb = lax.map(expert_block, (xb, blk_exp)).reshape(n_blocks * MOE_BLOCK, -1)
    y = yb[dest] * sg[:, None].astype(yb.dtype)
    return jax.ops.segment_sum(y, st, num_segments=n_tok)


def hier_moe(x, rgw, rgb, rew, reb, w_gate, w_up, w_down):
    shape = x.shape
    xf = x.reshape(-1, D_MODEL)
    n_tok = xf.shape[0]
    g_logits = (xf @ rgw + rgb).astype(jnp.float32)
    grp = jnp.argmax(g_logits, axis=-1).astype(jnp.int32)
    p_grp = jnp.take_along_axis(jax.nn.softmax(g_logits, axis=-1), grp[:, None], axis=1)
    e_logits = (xf @ rew + reb).astype(jnp.float32).reshape(n_tok, N_GROUPS, EXP_PER_GROUP)
    e_logits = jnp.take_along_axis(e_logits, grp[:, None, None], axis=1)[:, 0]
    top_p, top_i = lax.top_k(jax.nn.softmax(e_logits, axis=-1), EXPERT_TOP_K)
    gates = p_grp * top_p / jnp.sum(top_p, axis=-1, keepdims=True)
    experts = grp[:, None] * EXP_PER_GROUP + top_i.astype(jnp.int32)
    return grouped_experts(xf, experts, gates.astype(x.dtype), w_gate, w_up, w_down).reshape(shape)


def decoder_layer(x, conv_prefix, attend, w_in, b_in, conv_w, conv_b, conv_ln_g, conv_ln_b, w_conv_out,
                  w_attn_out, w_out, ln1_g, ln1_b, router_group_w, router_group_b, router_expert_w,
                  router_expert_b, w_gate, w_up, w_down, ln2_g, ln2_b):
    u, q, k, v, qi, ki, wi, g_conv, g_attn = project_inputs(x, w_in, b_in)
    u_hist = jnp.concatenate([conv_prefix.astype(u.dtype), u], axis=1)
    y_conv = conv_branch(u_hist, conv_w, conv_b, conv_ln_g, conv_ln_b, w_conv_out)
    y_attn = jnp.einsum('bsa,ad->bsd', attend(q, k, v, qi, ki, wi), w_attn_out)
    mix = jnp.einsum('bsd,de->bse', g_conv * y_conv + g_attn * y_attn, w_out)
    x = layer_norm(DN_ALPHA * x + mix, ln1_g, ln1_b)
    ffn = hier_moe(x, router_group_w, router_group_b, router_expert_w, router_expert_b, w_gate, w_up, w_down)
    x = layer_norm(DN_ALPHA * x + ffn, ln2_g, ln2_b)
    return x, k, v, ki, u_hist[:, -(CONV_W - 1):]


def setup_inputs(seed: int = 0) -> dict:
    key = jax.random.key(seed)
    ks = jax.random.split(key, 32)
    f32 = jnp.float32
    n_pages = PAST_LEN // PAGE_SIZE
    n_used = DEC_BATCH * n_pages
    n_phys = n_used + max(1, n_used // 4)

    def nrm(k, shape, scale):
        return jax.random.normal(k, shape, f32) * scale

    page_table = jax.random.permutation(ks[6], n_phys)[:n_used].reshape(DEC_BATCH, n_pages).astype(jnp.int32)
    return {
        'x_prompt': nrm(ks[0], (BATCH, SEQ, D_MODEL), 1.0),
        'x_sample': nrm(ks[1], (DEC_BATCH, DEC_SEQ, D_MODEL), 1.0),
        'cache_k': nrm(ks[2], (DEPTH, n_phys, PAGE_SIZE, N_KV_HEADS, HEAD_DIM), 1.0),
        'cache_v': nrm(ks[3], (DEPTH, n_phys, PAGE_SIZE, N_KV_HEADS, HEAD_DIM), 1.0),
        'cache_kidx': nrm(ks[4], (DEPTH, n_phys, PAGE_SIZE, IDX_DIM), 1.0),
        'state_conv': nrm(ks[5], (DEPTH, DEC_BATCH, CONV_W - 1, C_CONV), 0.5),
        'page_table': page_table,
        'w_in': nrm(ks[7], (DEPTH, D_MODEL, N_IN), D_MODEL ** -0.5),
        'b_in': nrm(ks[8], (DEPTH, N_IN), 0.02),
        'conv_w': nrm(ks[9], (DEPTH, CONV_W, C_CONV), CONV_W ** -0.5),
        'conv_b': nrm(ks[10], (DEPTH, C_CONV), 0.02),
        'conv_ln_g': 1.0 + nrm(ks[11], (DEPTH, C_CONV), 0.02),
        'conv_ln_b': nrm(ks[12], (DEPTH, C_CONV), 0.02),
        'w_conv_out': nrm(ks[13], (DEPTH, C_CONV, D_MODEL), C_CONV ** -0.5),
        'w_attn_out': nrm(ks[14], (DEPTH, ATTN_W, D_MODEL), ATTN_W ** -0.5),
        'w_out': nrm(ks[15], (DEPTH, D_MODEL, D_MODEL), DN_BETA * D_MODEL ** -0.5),
        'ln1_g': 1.0 + nrm(ks[16], (DEPTH, D_MODEL), 0.02),
        'ln1_b': nrm(ks[17], (DEPTH, D_MODEL), 0.02),
        'router_group_w': nrm(ks[18], (DEPTH, D_MODEL, N_GROUPS), D_MODEL ** -0.5),
        'router_group_b': nrm(ks[19], (DEPTH, N_GROUPS), 0.01),
        'router_expert_w': nrm(ks[20], (DEPTH, D_MODEL, N_EXPERTS), D_MODEL ** -0.5),
        'router_expert_b': nrm(ks[21], (DEPTH, N_EXPERTS), 0.01),
        'w_gate': nrm(ks[22], (DEPTH, N_EXPERTS, D_MODEL, D_EXPERT), D_MODEL ** -0.5),
        'w_up': nrm(ks[23], (DEPTH, N_EXPERTS, D_MODEL, D_EXPERT), D_MODEL ** -0.5),
        'w_down': nrm(ks[24], (DEPTH, N_EXPERTS, D_EXPERT, D_MODEL), DN_BETA * D_EXPERT ** -0.5),
        'ln2_g': 1.0 + nrm(ks[25], (DEPTH, D_MODEL), 0.02),
        'ln2_b': nrm(ks[26], (DEPTH, D_MODEL), 0.02),
    }


def reference(x_prompt, x_sample, cache_k, cache_v, cache_kidx, state_conv, page_table, w_in, b_in,
              conv_w, conv_b, conv_ln_g, conv_ln_b, w_conv_out, w_attn_out, w_out, ln1_g, ln1_b,
              router_group_w, router_group_b, router_expert_w, router_expert_b, w_gate, w_up, w_down,
              ln2_g, ln2_b):
    xp, xs = x_prompt, x_sample
    kp_l, vp_l, ip_l, cp_l = [], [], [], []
    ks_l, vs_l, is_l, cs_l = [], [], [], []
    conv_zero = jnp.zeros((x_prompt.shape[0], CONV_W - 1, C_CONV), x_prompt.dtype)
    for l in range(DEPTH):
        params = (w_in[l], b_in[l], conv_w[l], conv_b[l], conv_ln_g[l], conv_ln_b[l], w_conv_out[l],
                  w_attn_out[l], w_out[l], ln1_g[l], ln1_b[l], router_group_w[l], router_group_b[l],
                  router_expert_w[l], router_expert_b[l], w_gate[l], w_up[l], w_down[l], ln2_g[l], ln2_b[l])
        xp, kp, vp, ip, cp = decoder_layer(xp, conv_zero, dsa_prompt, *params)
        sample_attend = functools.partial(dsa_sample, k_pool=cache_k[l], v_pool=cache_v[l],
                                          ki_pool=cache_kidx[l], page_table=page_table)
        xs, k_s, v_s, i_s, c_s = decoder_layer(xs, state_conv[l], sample_attend, *params)
        kp_l.append(kp); vp_l.append(vp); ip_l.append(ip); cp_l.append(cp)
        ks_l.append(k_s); vs_l.append(v_s); is_l.append(i_s); cs_l.append(c_s)
    return (xp, xs, jnp.stack(kp_l), jnp.stack(vp_l), jnp.stack(ip_l), jnp.stack(cp_l),
            jnp.stack(ks_l), jnp.stack(vs_l), jnp.stack(is_l), jnp.stack(cs_l))
```

```python
import functools

import jax
import jax.numpy as jnp
from jax import lax
from jax.experimental import pallas as pl
from jax.experimental.pallas import tpu as pltpu

F32 = jnp.float32
BF16 = jnp.bfloat16
I32 = jnp.int32

D_MODEL = 1024
PAGE_SIZE = 128
C_CONV = 512
CONV_W = 31
N_HEADS = 8
N_KV_HEADS = 4
HEAD_DIM = 64
Q_PER_KV = N_HEADS // N_KV_HEADS
ATTN_W = N_HEADS * HEAD_DIM
KV_W = N_KV_HEADS * HEAD_DIM
N_IDX_HEADS = 8
IDX_DIM = 64
TOPK_MAX = 256
N_GROUPS = 4
EXP_PER_GROUP = 8
N_EXPERTS = N_GROUPS * EXP_PER_GROUP
D_EXPERT = 256
LN_EPS = 1e-5

LANES = 128
SUBLANES_BF16 = 16
VMEM_LIMIT = 56 * 1024 * 1024

_N_SMALL = IDX_DIM + N_IDX_HEADS
_COL_CA = 0
_COL_CB = _COL_CA + C_CONV
_COL_Q = _COL_CB + C_CONV
_COL_K = _COL_Q + ATTN_W
_COL_V = _COL_K + KV_W
_COL_QI = _COL_V + KV_W
_COL_KI = _COL_QI + N_IDX_HEADS * IDX_DIM
_COL_G = _COL_KI + LANES
_N_IN_PAD = _COL_G + 2 * D_MODEL
_N_IN_HEAD = _COL_KI + _N_SMALL

_NT = (((1,), (1,)), ((), ()))
_NEG = -1e30


def _params(*sem):
    return pltpu.CompilerParams(dimension_semantics=sem, vmem_limit_bytes=VMEM_LIMIT)


def _lspec(*shape):
    nd = len(shape)
    return pl.BlockSpec((None,) + shape, lambda *a: (a[-1][0],) + (0,) * nd)


def _layer_norm(x, g, b):
    mu = jnp.mean(x, axis=-1, keepdims=True)
    xc = x - mu
    var = jnp.mean(xc * xc, axis=-1, keepdims=True)
    return xc * lax.rsqrt(var + LN_EPS) * g + b


def _inproj_kernel(l_ref, x_ref, w_ref, b_ref, u_ref, q_ref, k_ref, v_ref, qi_ref, ki_ref, wi_ref, g_ref,
                   kb_ref, vb_ref, kib_ref):
    xb = x_ref[...].astype(BF16)

    def proj(c0, n):
        return jnp.dot(xb, w_ref[:, c0:c0 + n], preferred_element_type=F32) + b_ref[:, c0:c0 + n]

    u_ref[...] = proj(_COL_CA, C_CONV) * jax.nn.sigmoid(proj(_COL_CB, C_CONV))
    q_ref[...] = (proj(_COL_Q, ATTN_W) * (HEAD_DIM ** -0.5)).astype(BF16)
    k = proj(_COL_K, KV_W)
    k_ref[...] = k
    kb_ref[...] = k.astype(BF16)
    v = proj(_COL_V, KV_W)
    v_ref[...] = v
    vb_ref[...] = v.astype(BF16)
    qi_ref[...] = (proj(_COL_QI, N_IDX_HEADS * IDX_DIM) * (IDX_DIM ** -0.5)).astype(BF16)
    small = proj(_COL_KI, LANES)
    ki = small[:, :IDX_DIM]
    ki_ref[...] = ki
    kib_ref[...] = ki.astype(BF16)
    wi_ref[...] = small[:, IDX_DIM:_N_SMALL] * (N_IDX_HEADS ** -0.5)
    g_ref[...] = jax.nn.sigmoid(proj(_COL_G, 2 * D_MODEL))


def _inproj(lidx, x2d, w_pad, b_pad, tm):
    t = x2d.shape[0]
    widths = [(C_CONV, F32), (ATTN_W, BF16), (KV_W, F32), (KV_W, F32), (N_IDX_HEADS * IDX_DIM, BF16),
              (IDX_DIM, F32), (N_IDX_HEADS, F32), (2 * D_MODEL, F32), (KV_W, BF16), (KV_W, BF16),
              (IDX_DIM, BF16)]
    return pl.pallas_call(
        _inproj_kernel,
        out_shape=[jax.ShapeDtypeStruct((t, n), dt) for n, dt in widths],
        grid_spec=pltpu.PrefetchScalarGridSpec(
            num_scalar_prefetch=1, grid=(t // tm,),
            in_specs=[pl.BlockSpec((tm, D_MODEL), lambda i, l: (i, 0)),
                      _lspec(D_MODEL, _N_IN_PAD), _lspec(1, _N_IN_PAD)],
            out_specs=[pl.BlockSpec((tm, n), lambda i, l: (i, 0)) for n, _ in widths]),
        compiler_params=_params("parallel"),
        name="inproj",
    )(lidx, x2d, w_pad, b_pad)


_HALO = 32
_CONV_ROWS = 32


def _conv_kernel(l_ref, prev_ref, cur_ref, cw_ref, cb_ref, lg_ref, lb_ref, wo_ref, y_ref, hist_ref, acc_ref):
    tm = cur_ref.shape[1]
    hist_ref[0:_HALO, :] = prev_ref[0, tm - _HALO:tm, :]
    hist_ref[_HALO:_HALO + tm, :] = cur_ref[0]
    first = _HALO - (CONV_W - 1)
    for r0 in range(0, tm, _CONV_ROWS):
        acc = jnp.zeros((_CONV_ROWS, C_CONV), F32)
        for j in range(CONV_W):
            acc = acc + cw_ref[j:j + 1, :] * hist_ref[first + r0 + j:first + r0 + j + _CONV_ROWS, :]
        acc_ref[r0:r0 + _CONV_ROWS, :] = acc + cb_ref[...]
    y = _layer_norm(acc_ref[...], lg_ref[...], lb_ref[...])
    y = y * jax.nn.sigmoid(y)
    y_ref[0] = jnp.dot(y.astype(BF16), wo_ref[...], preferred_element_type=F32)


def _conv_branch(lidx, uh, conv_w, conv_b, ln_g, ln_b, w_out_bf, tm):
    nb, lp, _ = uh.shape
    nt = lp // tm - 1
    return pl.pallas_call(
        _conv_kernel,
        out_shape=jax.ShapeDtypeStruct((nb, nt * tm, D_MODEL), F32),
        grid_spec=pltpu.PrefetchScalarGridSpec(
            num_scalar_prefetch=1, grid=(nb, nt),
            in_specs=[pl.BlockSpec((1, tm, C_CONV), lambda b, i, l: (b, i, 0)),
                      pl.BlockSpec((1, tm, C_CONV), lambda b, i, l: (b, i + 1, 0)),
                      _lspec(CONV_W, C_CONV), _lspec(1, C_CONV), _lspec(1, C_CONV), _lspec(1, C_CONV),
                      _lspec(C_CONV, D_MODEL)],
            out_specs=pl.BlockSpec((1, tm, D_MODEL), lambda b, i, l: (b, i, 0)),
            scratch_shapes=[pltpu.VMEM((_HALO + tm, C_CONV), F32), pltpu.VMEM((tm, C_CONV), F32)]),
        compiler_params=_params("parallel", "parallel"),
        name="conv_branch",
    )(lidx, uh, uh, conv_w, conv_b, ln_g, ln_b, w_out_bf)


_BISECT_CAP = 40
_IDX_STEPS = 14


def _reduce_cols(sc_ref, n_chunks, tk, init, fn):
    rows = sc_ref.shape[0]

    def body(c, acc):
        for j in range(tk // LANES):
            col = pl.multiple_of(c * tk + j * LANES, LANES)
            acc = fn(acc, sc_ref[:, pl.ds(col, LANES)], col)
        return acc

    return lax.fori_loop(0, n_chunks, body, jnp.full((rows, LANES), init, F32))


def _count_ge(sc_ref, n_chunks, tk, thr):
    acc = _reduce_cols(sc_ref, n_chunks, tk, 0.0, lambda a, blk, col: a + jnp.where(blk >= thr, 1.0, 0.0))
    return jnp.sum(acc, axis=1, keepdims=True)


def _any(flag):
    return jnp.max(jnp.where(flag, 1, 0))


def _select_threshold(sc_ref, st_ref, n_chunks, tk, kt, row_min, row_max, n_valid):
    lo_ref, hi_ref, cl_ref, done_ref, jlo_ref, jhi_ref = (st_ref.at[n] for n in range(6))
    c_max = _count_ge(sc_ref, n_chunks, tk, row_max)
    top = c_max >= kt
    lo_ref[...] = jnp.where(top, row_max, row_min)
    hi_ref[...] = row_max
    cl0 = jnp.where(top, c_max, n_valid)
    cl_ref[...] = cl0
    done0 = jnp.where(top, 1.0, jnp.where(cl0 == kt, 1.0, 0.0))
    done_ref[...] = done0

    def bisect(carry):
        it, _ = carry
        lo, hi, cl, done = lo_ref[...], hi_ref[...], cl_ref[...], done_ref[...]
        mid = 0.5 * lo + 0.5 * hi
        c = _count_ge(sc_ref, n_chunks, tk, mid)
        act = done == 0.0
        up = c >= kt
        lo_ref[...] = jnp.where(act, jnp.where(up, mid, lo), lo)
        cl_ref[...] = jnp.where(act, jnp.where(up, c, cl), cl)
        hi_ref[...] = jnp.where(act, jnp.where(up, hi, mid), hi)
        done_new = jnp.where(act, jnp.where(c == kt, 1.0, 0.0), done)
        done_ref[...] = done_new
        return it + 1, _any(done_new == 0.0)

    lax.while_loop(lambda c: (c[0] < _BISECT_CAP) & (c[1] > 0), bisect, (jnp.int32(0), _any(done0 == 0.0)))

    def snap(_):
        lo, hi, cl, done = lo_ref[...], hi_ref[...], cl_ref[...], done_ref[...]
        acc = _reduce_cols(sc_ref, n_chunks, tk, -jnp.inf,
                           lambda a, blk, col: jnp.maximum(a, jnp.where(blk < hi, blk, -jnp.inf)))
        t1 = jnp.max(acc, axis=1, keepdims=True)
        c1 = _count_ge(sc_ref, n_chunks, tk, t1)
        act = done == 0.0
        found = c1 >= kt
        lo_ref[...] = jnp.where(act, jnp.where(found, t1, lo), lo)
        cl_ref[...] = jnp.where(act, jnp.where(found, c1, cl), cl)
        hi_ref[...] = jnp.where(act, jnp.where(found, hi, t1), hi)
        done_new = jnp.where(act, jnp.where(found, 1.0, 0.0), done)
        done_ref[...] = done_new
        return _any(done_new == 0.0)

    lax.while_loop(lambda f: f > 0, snap, _any(done_ref[...] == 0.0))

    thr = lo_ref[...]

    @pl.when(_any(cl_ref[...] != kt) > 0)
    def _cut_ties():
        gt = _reduce_cols(sc_ref, n_chunks, tk, 0.0, lambda a, blk, col: a + jnp.where(blk > thr, 1.0, 0.0))
        need = kt - jnp.sum(gt, axis=1, keepdims=True)
        jlo_ref[...] = jnp.full_like(thr, -1.0)
        jhi_ref[...] = jnp.full_like(thr, float(sc_ref.shape[1] - 1))
        lane = lax.broadcasted_iota(I32, (1, LANES), 1)

        def step(_, carry):
            jlo, jhi = jlo_ref[...], jhi_ref[...]
            mid = jnp.floor(0.5 * (jlo + jhi))
            acc = _reduce_cols(
                sc_ref, n_chunks, tk, 0.0,
                lambda a, blk, col: a + jnp.where(blk == thr,
                                                  jnp.where((col + lane).astype(F32) <= mid, 1.0, 0.0), 0.0))
            ok = jnp.sum(acc, axis=1, keepdims=True) >= need
            jhi_ref[...] = jnp.where(ok, mid, jhi)
            jlo_ref[...] = jnp.where(ok, jlo, mid)
            return carry

        lax.fori_loop(0, _IDX_STEPS, step, 0)
        cut = jhi_ref[...]

        def demote(c, carry):
            for j in range(tk // LANES):
                col = pl.multiple_of(c * tk + j * LANES, LANES)
                blk = sc_ref[:, pl.ds(col, LANES)]
                beyond = (col + lane).astype(F32) > cut
                sc_ref[:, pl.ds(col, LANES)] = jnp.where(blk == thr, jnp.where(beyond, -jnp.inf, blk), blk)
            return carry

        lax.fori_loop(0, n_chunks, demote, 0)

    return thr


_TQ = 128
_TK = 512


def _indexer_scores(qi, wi, kc):
    rows = wi.shape[0]
    d = lax.dot_general(qi, kc, _NT, preferred_element_type=F32)
    score = None
    for h in range(N_IDX_HEADS):
        t = wi[:, h:h + 1] * jnp.maximum(d[h * rows:(h + 1) * rows, :], 0.0)
        score = t if score is None else score + t
    return jnp.where(score == 0.0, 0.0, score)


def _flash_step(s, vc, m_ref, l_ref, acc_ref, g):
    m_old = m_ref[g]
    m_new = jnp.maximum(m_old, jnp.max(s, axis=1, keepdims=True))
    alpha = jnp.exp(m_old - m_new)
    p = jnp.exp(s - m_new)
    l_ref[g] = alpha * l_ref[g] + jnp.sum(p, axis=1, keepdims=True)
    acc_ref[g] = alpha * acc_ref[g] + jnp.dot(p.astype(BF16), vc, preferred_element_type=F32)
    m_ref[g] = m_new


def _dsa_prompt_kernel(qi_ref, wi_ref, ki_ref, q_ref, k_ref, v_ref, o_ref,
                       sc_ref, st_ref, m_ref, l_ref, acc_ref):
    tq, tk = _TQ, _TK
    i = pl.program_id(1)
    n_chunks = ((i + 1) * tq + tk - 1) // tk
    qpos = i * tq + lax.broadcasted_iota(I32, (tq, 1), 0)

    def scores(c, carry):
        mn, mx = carry
        off = pl.multiple_of(c * tk, tk)
        score = _indexer_scores(qi_ref[0, 0], wi_ref[0], ki_ref[0, pl.ds(off, tk), :])
        valid = off + lax.broadcasted_iota(I32, (1, tk), 1) <= qpos
        sc_ref[:, pl.ds(off, tk)] = jnp.where(valid, score, -jnp.inf)
        for j in range(tk // LANES):
            sj = score[:, j * LANES:(j + 1) * LANES]
            vj = valid[:, j * LANES:(j + 1) * LANES]
            mn = jnp.minimum(mn, jnp.where(vj, sj, jnp.inf))
            mx = jnp.maximum(mx, jnp.where(vj, sj, -jnp.inf))
        return mn, mx

    mn, mx = lax.fori_loop(0, n_chunks, scores, (jnp.full((tq, LANES), jnp.inf, F32),
                                                 jnp.full((tq, LANES), -jnp.inf, F32)))
    n_valid = (qpos + 1).astype(F32)
    kt = jnp.minimum(n_valid, float(TOPK_MAX))
    thr = _select_threshold(sc_ref, st_ref, n_chunks, tk, kt, jnp.min(mn, axis=1, keepdims=True),
                            jnp.max(mx, axis=1, keepdims=True), n_valid)

    m_ref[...] = jnp.full(m_ref.shape, _NEG, F32)
    l_ref[...] = jnp.zeros(l_ref.shape, F32)
    acc_ref[...] = jnp.zeros(acc_ref.shape, F32)

    def attend(c, carry):
        off = pl.multiple_of(c * tk, tk)
        bias = jnp.where(sc_ref[:, pl.ds(off, tk)] >= thr, 0.0, -jnp.inf)
        for g in range(N_KV_HEADS):
            s = lax.dot_general(q_ref[0, 0, g], k_ref[0, g, pl.ds(off, tk), :], _NT,
                                preferred_element_type=F32)
            s = jnp.concatenate([s[h * tq:(h + 1) * tq] + bias for h in range(Q_PER_KV)], axis=0)
            _flash_step(s, v_ref[0, g, pl.ds(off, tk), :], m_ref, l_ref, acc_ref, g)
        return carry

    lax.fori_loop(0, n_chunks, attend, 0)
    for g in range(N_KV_HEADS):
        o_ref[0, 0, g] = acc_ref[g] / l_ref[g]


def _dsa_prompt(q_b, k_b, v_b, qi_b, ki_b, wi, nb, s):
    tq = _TQ
    nq = s // tq
    qg = q_b.reshape(nb, nq, tq, N_KV_HEADS, Q_PER_KV, HEAD_DIM).transpose(0, 1, 3, 4, 2, 5)
    qg = qg.reshape(nb, nq, N_KV_HEADS, Q_PER_KV * tq, HEAD_DIM)
    qi = qi_b.reshape(nb, nq, tq, N_IDX_HEADS, IDX_DIM).transpose(0, 1, 3, 2, 4)
    qi = qi.reshape(nb, nq, N_IDX_HEADS * tq, IDX_DIM)
    kh = k_b.reshape(nb, s, N_KV_HEADS, HEAD_DIM).transpose(0, 2, 1, 3)
    vh = v_b.reshape(nb, s, N_KV_HEADS, HEAD_DIM).transpose(0, 2, 1, 3)
    rows = Q_PER_KV * tq
    o = pl.pallas_call(
        _dsa_prompt_kernel,
        out_shape=jax.ShapeDtypeStruct((nb, nq, N_KV_HEADS, rows, HEAD_DIM), F32),
        grid=(nb, nq),
        in_specs=[pl.BlockSpec((1, 1, N_IDX_HEADS * tq, IDX_DIM), lambda b, i: (b, i, 0, 0)),
                  pl.BlockSpec((1, tq, N_IDX_HEADS), lambda b, i: (b, i, 0)),
                  pl.BlockSpec((1, s, IDX_DIM), lambda b, i: (b, 0, 0), pipeline_mode=pl.Buffered(1)),
                  pl.BlockSpec((1, 1, N_KV_HEADS, rows, HEAD_DIM), lambda b, i: (b, i, 0, 0, 0)),
                  pl.BlockSpec((1, N_KV_HEADS, s, HEAD_DIM), lambda b, i: (b, 0, 0, 0),
                               pipeline_mode=pl.Buffered(1)),
                  pl.BlockSpec((1, N_KV_HEADS, s, HEAD_DIM), lambda b, i: (b, 0, 0, 0),
                               pipeline_mode=pl.Buffered(1))],
        out_specs=pl.BlockSpec((1, 1, N_KV_HEADS, rows, HEAD_DIM), lambda b, i: (b, i, 0, 0, 0)),
        scratch_shapes=[pltpu.VMEM((tq, s), F32), pltpu.VMEM((8, tq, 1), F32),
                        pltpu.VMEM((N_KV_HEADS, rows, 1), F32), pltpu.VMEM((N_KV_HEADS, rows, 1), F32),
                        pltpu.VMEM((N_KV_HEADS, rows, HEAD_DIM), F32)],
        compiler_params=_params("parallel", "arbitrary"),
        name="dsa_prompt",
    )(qi, wi.reshape(nb, s, N_IDX_HEADS), ki_b.reshape(nb, s, IDX_DIM), qg, kh, vh)
    o = o.reshape(nb, nq, N_KV_HEADS, Q_PER_KV, tq, HEAD_DIM).transpose(0, 1, 4, 2, 3, 5)
    return o.reshape(nb * s, ATTN_W)


_QPAD = 8
_PAGES_SC = 16
_PAGES_KV = 8
_SEL_ROWS = 128


def _sample_scores_kernel(pt_ref, qi_ref, wi_ref, kn_ref, *rest):
    pages, sc_ref = rest[:_PAGES_SC], rest[_PAGES_SC]
    j = pl.program_id(1)
    last = pl.num_programs(1) - 1

    @pl.when(j < last)
    def _past():
        for r in range(_PAGES_SC):
            kc = pages[r][0].astype(BF16)
            sc_ref[0, :, r * PAGE_SIZE:(r + 1) * PAGE_SIZE] = _indexer_scores(qi_ref[0], wi_ref[0], kc)

    @pl.when(j == last)
    def _new():
        sc_ref[0] = jnp.zeros(sc_ref.shape[1:], F32)
        sc_ref[0, :, 0:PAGE_SIZE] = _indexer_scores(qi_ref[0], wi_ref[0], kn_ref[0])


def _sample_select_kernel(sc_in_ref, bias_ref, sc_ref, st_ref, *, past, dec_seq):
    rows, width = sc_ref.shape
    tk = _TK
    n_chunks = width // tk
    q = lax.broadcasted_iota(I32, (rows, 1), 0) % dec_seq
    mn = jnp.full((rows, LANES), jnp.inf, F32)
    mx = jnp.full((rows, LANES), -jnp.inf, F32)
    for c0 in range(0, width, LANES):
        s = sc_in_ref[:, c0:c0 + LANES]
        if c0 + LANES > past:
            valid = c0 + lax.broadcasted_iota(I32, (1, LANES), 1) <= past + q
            sc_ref[:, c0:c0 + LANES] = jnp.where(valid, s, -jnp.inf)
            mn = jnp.minimum(mn, jnp.where(valid, s, jnp.inf))
            mx = jnp.maximum(mx, jnp.where(valid, s, -jnp.inf))
        else:
            sc_ref[:, c0:c0 + LANES] = s
            mn = jnp.minimum(mn, s)
            mx = jnp.maximum(mx, s)
    n_valid = (past + 1 + q).astype(F32)
    kt = jnp.minimum(n_valid, float(min(TOPK_MAX, (past + dec_seq) // 4)))
    thr = _select_threshold(sc_ref, st_ref, n_chunks, tk, kt, jnp.min(mn, axis=1, keepdims=True),
                            jnp.max(mx, axis=1, keepdims=True), n_valid)
    for c0 in range(0, width, tk):
        bias_ref[:, c0:c0 + tk] = jnp.where(sc_ref[:, c0:c0 + tk] >= thr, 0.0, -jnp.inf)


def _sample_attend_kernel(pt_ref, q_ref, bias_ref, kn_ref, vn_ref, *rest):
    kpages, vpages = rest[:_PAGES_KV], rest[_PAGES_KV:2 * _PAGES_KV]
    o_ref, m_ref, l_ref, acc_ref = rest[2 * _PAGES_KV:]
    j = pl.program_id(1)
    last = pl.num_programs(1) - 1

    @pl.when(j == 0)
    def _init():
        m_ref[...] = jnp.full(m_ref.shape, _NEG, F32)
        l_ref[...] = jnp.zeros(l_ref.shape, F32)
        acc_ref[...] = jnp.zeros(acc_ref.shape, F32)

    def page(kc, vc, bias):
        s = lax.dot_general(q_ref[0], kc, _NT, preferred_element_type=F32)
        s = jnp.concatenate([s[h * _QPAD:(h + 1) * _QPAD] + bias for h in range(N_HEADS)], axis=0)
        _flash_step(s, vc, m_ref, l_ref, acc_ref, 0)

    @pl.when(j < last)
    def _past():
        for r in range(_PAGES_KV):
            page(kpages[r][0].astype(BF16), vpages[r][0].astype(BF16),
                 bias_ref[0, :, r * PAGE_SIZE:(r + 1) * PAGE_SIZE])

    @pl.when(j == last)
    def _new():
        page(kn_ref[0], vn_ref[0], bias_ref[0, :, 0:PAGE_SIZE])
        o_ref[0] = acc_ref[0] / l_ref[0]


def _dsa_sample(q_b, k_b, v_b, qi_b, ki_b, wi, k_pool, v_pool, ki_pool, page_table, nb, s):
    n_pages = page_table.shape[1]
    past = n_pages * PAGE_SIZE
    dup = jnp.arange(_QPAD) % s
    qi = qi_b.reshape(nb, s, N_IDX_HEADS, IDX_DIM)[:, dup].transpose(0, 2, 1, 3)
    qi = qi.reshape(nb, N_IDX_HEADS * _QPAD, IDX_DIM)
    wi8 = wi.reshape(nb, s, N_IDX_HEADS)[:, dup]
    pad_rows = lambda a: jnp.pad(a.reshape(nb, s, -1), ((0, 0), (0, PAGE_SIZE - s), (0, 0)))
    kin, kn, vn = pad_rows(ki_b), pad_rows(k_b), pad_rows(v_b)

    n_sc = n_pages // _PAGES_SC
    sc_w = _PAGES_SC * PAGE_SIZE

    def page_spec(width, per_step, r):
        return pl.BlockSpec((1, PAGE_SIZE, width),
                            lambda b, j, pt: (pt[b, jnp.minimum(j * per_step + r, n_pages - 1)], 0, 0))

    scores = pl.pallas_call(
        _sample_scores_kernel,
        out_shape=jax.ShapeDtypeStruct((nb, _QPAD, (n_sc + 1) * sc_w), F32),
        grid_spec=pltpu.PrefetchScalarGridSpec(
            num_scalar_prefetch=1, grid=(nb, n_sc + 1),
            in_specs=[pl.BlockSpec((1, N_IDX_HEADS * _QPAD, IDX_DIM), lambda b, j, pt: (b, 0, 0)),
                      pl.BlockSpec((1, _QPAD, N_IDX_HEADS), lambda b, j, pt: (b, 0, 0)),
                      pl.BlockSpec((1, PAGE_SIZE, IDX_DIM), lambda b, j, pt: (b, 0, 0))]
                     + [page_spec(IDX_DIM, _PAGES_SC, r) for r in range(_PAGES_SC)],
            out_specs=pl.BlockSpec((1, _QPAD, sc_w), lambda b, j, pt: (b, 0, j))),
        compiler_params=_params("parallel", "arbitrary"),
        name="sample_scores",
    )(page_table, qi, wi8, kin, *([ki_pool] * _PAGES_SC))

    width = scores.shape[2]
    sel_rows = _token_tile(nb * _QPAD, _SEL_ROWS)
    bias = pl.pallas_call(
        functools.partial(_sample_select_kernel, past=past, dec_seq=s),
        out_shape=jax.ShapeDtypeStruct((nb * _QPAD, width), F32),
        grid=(nb * _QPAD // sel_rows,),
        in_specs=[pl.BlockSpec((sel_rows, width), lambda i: (i, 0))],
        out_specs=pl.BlockSpec((sel_rows, width), lambda i: (i, 0)),
        scratch_shapes=[pltpu.VMEM((sel_rows, width), F32), pltpu.VMEM((8, sel_rows, 1), F32)],
        compiler_params=_params("parallel"),
        name="sample_select",
    )(scores.reshape(nb * _QPAD, width)).reshape(nb, _QPAD, width)

    q4 = q_b.reshape(nb, s, N_KV_HEADS, Q_PER_KV, HEAD_DIM)[:, dup]
    eye = jnp.eye(N_KV_HEADS, dtype=q_b.dtype)
    qx = jnp.einsum("bqgjd,gk->bgjqkd", q4, eye).reshape(nb, N_HEADS * _QPAD, KV_W)
    n_kv = n_pages // _PAGES_KV
    kv_w = _PAGES_KV * PAGE_SIZE
    new_blk = past // kv_w
    hq = N_HEADS * _QPAD
    o = pl.pallas_call(
        _sample_attend_kernel,
        out_shape=jax.ShapeDtypeStruct((nb, hq, KV_W), F32),
        grid_spec=pltpu.PrefetchScalarGridSpec(
            num_scalar_prefetch=1, grid=(nb, n_kv + 1),
            in_specs=[pl.BlockSpec((1, hq, KV_W), lambda b, j, pt: (b, 0, 0)),
                      pl.BlockSpec((1, _QPAD, kv_w), lambda b, j, pt: (b, 0, jnp.minimum(j, new_blk))),
                      pl.BlockSpec((1, PAGE_SIZE, KV_W), lambda b, j, pt: (b, 0, 0)),
                      pl.BlockSpec((1, PAGE_SIZE, KV_W), lambda b, j, pt: (b, 0, 0))]
                     + [page_spec(KV_W, _PAGES_KV, r) for r in range(_PAGES_KV)] * 2,
            out_specs=pl.BlockSpec((1, hq, KV_W), lambda b, j, pt: (b, 0, 0)),
            scratch_shapes=[pltpu.VMEM((1, hq, 1), F32), pltpu.VMEM((1, hq, 1), F32),
                            pltpu.VMEM((1, hq, KV_W), F32)]),
        compiler_params=_params("parallel", "arbitrary"),
        name="sample_attend",
    )(page_table, qx, bias, kn, vn, *([k_pool] * _PAGES_KV), *([v_pool] * _PAGES_KV))
    o = o.reshape(nb, N_KV_HEADS, Q_PER_KV, _QPAD, N_KV_HEADS, HEAD_DIM)[:, :, :, :s]
    o = jnp.einsum("bgjqkd,gk->bqgjd", o, jnp.eye(N_KV_HEADS, dtype=o.dtype))
    return o.reshape(nb * s, ATTN_W)


_ROUTE_W = 8


def _merge_kernel(l_ref, x_ref, yc_ref, o_ref, g_ref, wao_ref, wout_ref, lg_ref, lb_ref, rwh_ref, rwl_ref,
                  rb_ref, x1_ref, x1b_ref, route_ref, *, alpha):
    y_attn = jnp.dot(o_ref[...].astype(BF16), wao_ref[...], preferred_element_type=F32)
    mix_in = g_ref[:, :D_MODEL] * yc_ref[...] + g_ref[:, D_MODEL:] * y_attn
    mix = jnp.dot(mix_in.astype(BF16), wout_ref[...], preferred_element_type=F32)
    x1 = _layer_norm(alpha * x_ref[...] + mix, lg_ref[...], lb_ref[...])
    x1_ref[...] = x1
    hi = x1.astype(BF16)
    x1b_ref[...] = hi

    lo = (x1 - hi.astype(F32)).astype(BF16)
    logits = (jnp.dot(hi, rwh_ref[...], preferred_element_type=F32)
              + jnp.dot(hi, rwl_ref[...], preferred_element_type=F32)
              + jnp.dot(lo, rwh_ref[...], preferred_element_type=F32)) + rb_ref[...]
    lane = lax.broadcasted_iota(I32, (1, LANES), 1)

    def first_max(v):
        m = jnp.max(v, axis=1, keepdims=True)
        return m, jnp.min(jnp.where(v == m, lane, LANES), axis=1, keepdims=True)

    is_group = (lane >= N_EXPERTS) & (lane < N_EXPERTS + N_GROUPS)
    gl = jnp.where(is_group, logits, -jnp.inf)
    gm, gidx = first_max(gl)
    p_grp = 1.0 / jnp.sum(jnp.exp(gl - gm), axis=1, keepdims=True)
    grp = gidx - N_EXPERTS
    group_of_lane = lane >> (EXP_PER_GROUP.bit_length() - 1)
    el = jnp.where(group_of_lane == grp, logits, -jnp.inf)
    e1, i1 = first_max(el)
    e2, i2 = first_max(jnp.where(lane == i1, -jnp.inf, el))
    t = jnp.exp(e2 - e1)
    g1 = p_grp / (1.0 + t)
    g2 = g1 * t
    rl = lax.broadcasted_iota(I32, (1, _ROUTE_W), 1)
    route_ref[...] = jnp.where(rl == 0, i1.astype(F32), jnp.where(rl == 1, i2.astype(F32),
                               jnp.where(rl == 2, g1, jnp.where(rl == 3, g2, 0.0))))


def _merge(lidx, x2d, yc, o, g, p, tm, alpha):
    t = x2d.shape[0]
    row = lambda n: pl.BlockSpec((tm, n), lambda i, l: (i, 0))
    return pl.pallas_call(
        functools.partial(_merge_kernel, alpha=alpha),
        out_shape=[jax.ShapeDtypeStruct((t, D_MODEL), F32), jax.ShapeDtypeStruct((t, D_MODEL), BF16),
                   jax.ShapeDtypeStruct((t, _ROUTE_W), F32)],
        grid_spec=pltpu.PrefetchScalarGridSpec(
            num_scalar_prefetch=1, grid=(t // tm,),
            in_specs=[row(D_MODEL), row(D_MODEL), row(ATTN_W), row(2 * D_MODEL),
                      _lspec(ATTN_W, D_MODEL), _lspec(D_MODEL, D_MODEL), _lspec(1, D_MODEL),
                      _lspec(1, D_MODEL), _lspec(D_MODEL, LANES), _lspec(D_MODEL, LANES), _lspec(1, LANES)],
            out_specs=[row(D_MODEL), row(D_MODEL), row(_ROUTE_W)]),
        compiler_params=_params("parallel"),
        name="merge_ln1_router",
    )(lidx, x2d, yc, o, g, p["w_attn_out"], p["w_out"], p["ln1_g"], p["ln1_b"], p["router_hi"],
      p["router_lo"], p["router_b"])


_SLOT_ALIGN = SUBLANES_BF16
_ROW_BLK = 64
_MOE_CHUNK = 256
_META_W = LANES
_MOE_TILE_MAX = 256 * _SLOT_ALIGN


def _moe_plan_kernel(route_ref, dcol_ref, drow_ref, meta_ref):
    tm = route_ref.shape[0]
    blk_rows = min(_MOE_CHUNK, tm)
    lane = lax.broadcasted_iota(I32, (1, LANES), 1).astype(F32)
    hit0 = lane == route_ref[:, 0:1]
    hit1 = lane == route_ref[:, 1:2]
    onehot = jnp.where(hit0, 1.0, jnp.where(hit1, 1.0, 0.0))
    r = lax.broadcasted_iota(I32, (blk_rows, blk_rows), 0)
    c = lax.broadcasted_iota(I32, (blk_rows, blk_rows), 1)
    tri = jnp.where(c < r, 1.0, 0.0).astype(BF16)
    carry = jnp.zeros((1, LANES), F32)
    prefix = []
    for b0 in range(0, tm, blk_rows):
        blk = onehot[b0:b0 + blk_rows]
        prefix.append(jnp.dot(tri, blk.astype(BF16), preferred_element_type=F32) + carry)
        carry = carry + jnp.sum(blk, axis=0, keepdims=True)
    prefix = jnp.concatenate(prefix, axis=0)
    units = jnp.ceil(carry * (1.0 / _SLOT_ALIGN))
    rr = lax.broadcasted_iota(I32, (LANES, LANES), 0)
    cc = lax.broadcasted_iota(I32, (LANES, LANES), 1)
    upper = jnp.where(rr < cc, 1.0, 0.0).astype(BF16)
    units8 = jnp.broadcast_to(units, (8, LANES)).astype(BF16)
    offs = jnp.dot(units8, upper, preferred_element_type=F32)[0:1] * _SLOT_ALIGN
    slot = offs + prefix
    d0 = jnp.sum(jnp.where(hit0, slot, 0.0), axis=1, keepdims=True)
    d1 = jnp.sum(jnp.where(hit1, slot, 0.0), axis=1, keepdims=True)
    rl = lax.broadcasted_iota(I32, (1, LANES), 1)
    rec = jnp.where(rl == 0, d0, jnp.where(rl == 1, d1, jnp.where(rl == 2, route_ref[:, 2:3],
                    jnp.where(rl == 3, route_ref[:, 3:4], 0.0))))
    dcol_ref[...] = rec[:, :_ROUTE_W]
    drow_ref[...] = rec.T[:_ROUTE_W, :]
    nblk = jnp.ceil(units * (_SLOT_ALIGN / _ROW_BLK))
    shift = jnp.where(cc == rr + N_EXPERTS, 1.0, 0.0).astype(BF16)
    nblk_sh = jnp.dot(jnp.broadcast_to(nblk, (8, LANES)).astype(BF16), shift,
                      preferred_element_type=F32)[0:1]
    meta_ref[0] = jnp.where(rl < N_EXPERTS, offs, nblk_sh).astype(I32)


def _moe_plan(route, tm):
    t = route.shape[0]
    nt = t // tm
    assert tm <= _MOE_TILE_MAX
    return pl.pallas_call(
        _moe_plan_kernel,
        out_shape=[jax.ShapeDtypeStruct((t, _ROUTE_W), F32), jax.ShapeDtypeStruct((_ROUTE_W, t), F32),
                   jax.ShapeDtypeStruct((nt, 1, _META_W), I32)],
        grid=(nt,),
        in_specs=[pl.BlockSpec((tm, _ROUTE_W), lambda i: (i, 0))],
        out_specs=[pl.BlockSpec((tm, _ROUTE_W), lambda i: (i, 0)),
                   pl.BlockSpec((_ROUTE_W, tm), lambda i: (0, i)),
                   pl.BlockSpec((1, 1, _META_W), lambda i: (i, 0, 0))],
        compiler_params=_params("parallel"),
        name="moe_plan",
    )(route)


def _n_slots(tm):
    n = 2 * tm + N_EXPERTS * (_SLOT_ALIGN - 1) + _ROW_BLK
    return -(-n // _MOE_CHUNK) * _MOE_CHUNK


def _moe_kernel(meta_ref, l_ref, x1b_ref, x1_ref, drow_ref, dcol_ref, wg_ref, wu_ref, wd_ref, lg_ref, lb_ref,
                x2_ref, xb_ref, yb_ref, gs_ref, *, alpha):
    i, e = pl.program_id(0), pl.program_id(1)
    tm = x1_ref.shape[0]
    n_slots = xb_ref.shape[0]

    @pl.when(e == 0)
    def _dispatch():
        d0, d1 = drow_ref[0:1, :], drow_ref[1:2, :]
        g0, g1 = drow_ref[2:3, :], drow_ref[3:4, :]
        for s0 in range(0, n_slots, _MOE_CHUNK):
            sl = (s0 + lax.broadcasted_iota(I32, (_MOE_CHUNK, 1), 0)).astype(F32)
            a, b = sl == d0, sl == d1
            p = jnp.where(a, 1.0, jnp.where(b, 1.0, 0.0)).astype(BF16)
            xb_ref[s0:s0 + _MOE_CHUNK, :] = jnp.dot(p, x1b_ref[...], preferred_element_type=F32).astype(BF16)
            gs_ref[s0:s0 + _MOE_CHUNK, :] = jnp.sum(jnp.where(a, g0, jnp.where(b, g1, 0.0)), axis=1,
                                                    keepdims=True)
        yb_ref[...] = jnp.zeros(yb_ref.shape, BF16)

    off = meta_ref[i * _META_W + e]
    nblk = meta_ref[i * _META_W + N_EXPERTS + e]

    def block(r, carry):
        start = pl.multiple_of(off + r * _ROW_BLK, _SLOT_ALIGN)
        xs = xb_ref[pl.ds(start, _ROW_BLK), :]
        hg = jnp.dot(xs, wg_ref[...], preferred_element_type=F32)
        hu = jnp.dot(xs, wu_ref[...], preferred_element_type=F32)
        h = (hg * jax.nn.sigmoid(hg)) * hu
        y = jnp.dot(h.astype(BF16), wd_ref[...], preferred_element_type=F32)
        yb_ref[pl.ds(start, _ROW_BLK), :] = (y * gs_ref[pl.ds(start, _ROW_BLK), :]).astype(BF16)
        return carry

    lax.fori_loop(0, nblk, block, 0)

    @pl.when(e == pl.num_programs(1) - 1)
    def _combine():
        lane = lax.broadcasted_iota(I32, (1, n_slots), 1).astype(F32)
        rows = min(_MOE_CHUNK, tm)
        for t0 in range(0, tm, rows):
            d0, d1 = dcol_ref[t0:t0 + rows, 0:1], dcol_ref[t0:t0 + rows, 1:2]
            qm = jnp.where(lane == d0, 1.0, jnp.where(lane == d1, 1.0, 0.0)).astype(BF16)
            y = jnp.dot(qm, yb_ref[...], preferred_element_type=F32)
            z = alpha * x1_ref[t0:t0 + rows, :] + y
            x2_ref[t0:t0 + rows, :] = _layer_norm(z, lg_ref[...], lb_ref[...])


def _moe(lidx, x1, x1b, drow, dcol, meta, p, tm, alpha):
    t = x1.shape[0]
    n_slots = _n_slots(tm)
    tile = lambda shape, imap: pl.BlockSpec(shape, imap)
    return pl.pallas_call(
        functools.partial(_moe_kernel, alpha=alpha),
        out_shape=jax.ShapeDtypeStruct((t, D_MODEL), F32),
        grid_spec=pltpu.PrefetchScalarGridSpec(
            num_scalar_prefetch=2, grid=(t // tm, N_EXPERTS),
            in_specs=[tile((tm, D_MODEL), lambda i, e, m, l: (i, 0)),
                      tile((tm, D_MODEL), lambda i, e, m, l: (i, 0)),
                      tile((_ROUTE_W, tm), lambda i, e, m, l: (0, i)),
                      tile((tm, _ROUTE_W), lambda i, e, m, l: (i, 0)),
                      tile((None, None, D_MODEL, D_EXPERT), lambda i, e, m, l: (l[0], e, 0, 0)),
                      tile((None, None, D_MODEL, D_EXPERT), lambda i, e, m, l: (l[0], e, 0, 0)),
                      tile((None, None, D_EXPERT, D_MODEL), lambda i, e, m, l: (l[0], e, 0, 0)),
                      _lspec(1, D_MODEL), _lspec(1, D_MODEL)],
            out_specs=tile((tm, D_MODEL), lambda i, e, m, l: (i, 0)),
            scratch_shapes=[pltpu.VMEM((n_slots, D_MODEL), BF16), pltpu.VMEM((n_slots, D_MODEL), BF16),
                            pltpu.VMEM((n_slots, 1), F32)]),
        compiler_params=_params("parallel", "arbitrary"),
        name="moe_experts_ln2",
    )(meta.reshape(-1), lidx, x1b, x1, drow, dcol, p["w_gate"], p["w_up"], p["w_down"], p["ln2_g"], p["ln2_b"])


def _token_tile(t, want):
    return want if t % want == 0 else t


def _decoder_layer(lidx, x, prefix, attend, p, alpha):
    nb, s, _ = x.shape
    t = nb * s
    x2d = x.reshape(t, D_MODEL)
    u, q_b, k, v, qi_b, ki, wi, g, k_b, v_b, ki_b = _inproj(lidx, x2d, p["w_in"], p["b_in"], _token_tile(t, 512))

    ctm = 512 if s % 512 == 0 else _HALO
    u3 = u.reshape(nb, s, C_CONV)
    uh = jnp.concatenate([jnp.zeros((nb, ctm - (CONV_W - 1), C_CONV), F32), prefix, u3,
                          jnp.zeros((nb, (-s) % ctm, C_CONV), F32)], axis=1)
    yc = _conv_branch(lidx, uh, p["conv_w"], p["conv_b"], p["conv_ln_g"], p["conv_ln_b"], p["w_conv_out"], ctm)
    yc = yc[:, :s].reshape(t, D_MODEL)

    o = attend(q_b, k_b, v_b, qi_b, ki_b, wi)
    x1, x1b, route = _merge(lidx, x2d, yc, o, g, p, _token_tile(t, 256), alpha)
    mtm = _token_tile(t, 1024)
    dcol, drow, meta = _moe_plan(route, mtm)
    x2 = _moe(lidx, x1, x1b, drow, dcol, meta, p, mtm, alpha)

    new_conv = jnp.concatenate([prefix, u3], axis=1)[:, -(CONV_W - 1):]
    return (x2.reshape(nb, s, D_MODEL), k.reshape(nb, s, N_KV_HEADS, HEAD_DIM),
            v.reshape(nb, s, N_KV_HEADS, HEAD_DIM), ki.reshape(nb, s, IDX_DIM), new_conv)


def _prepare_params(w_in, b_in, conv_w, conv_b, conv_ln_g, conv_ln_b, w_conv_out, w_attn_out, w_out,
                    ln1_g, ln1_b, router_group_w, router_group_b, router_expert_w, router_expert_b,
                    w_gate, w_up, w_down, ln2_g, ln2_b):
    depth = w_in.shape[0]
    pad = _COL_G - _N_IN_HEAD
    w_pad = jnp.concatenate([w_in[..., :_N_IN_HEAD], jnp.zeros((depth, D_MODEL, pad), F32),
                             w_in[..., _N_IN_HEAD:]], axis=-1).astype(BF16)
    b_pad = jnp.concatenate([b_in[..., :_N_IN_HEAD], jnp.zeros((depth, pad), F32),
                             b_in[..., _N_IN_HEAD:]], axis=-1)[:, None, :]
    rpad = LANES - N_EXPERTS - N_GROUPS
    rw = jnp.concatenate([router_expert_w, router_group_w, jnp.zeros((depth, D_MODEL, rpad), F32)], axis=-1)
    rb = jnp.concatenate([router_expert_b, router_group_b, jnp.zeros((depth, rpad), F32)], axis=-1)
    rw_hi = rw.astype(BF16)
    rw_lo = (rw - rw_hi.astype(F32)).astype(BF16)
    vec = lambda a: a[:, None, :]
    return dict(w_in=w_pad, b_in=b_pad, conv_w=conv_w, conv_b=vec(conv_b), conv_ln_g=vec(conv_ln_g),
                conv_ln_b=vec(conv_ln_b), w_conv_out=w_conv_out.astype(BF16),
                w_attn_out=w_attn_out.astype(BF16), w_out=w_out.astype(BF16), ln1_g=vec(ln1_g),
                ln1_b=vec(ln1_b), router_hi=rw_hi, router_lo=rw_lo, router_b=vec(rb),
                w_gate=w_gate.astype(BF16), w_up=w_up.astype(BF16), w_down=w_down.astype(BF16),
                ln2_g=vec(ln2_g), ln2_b=vec(ln2_b))


def kernel(x_prompt, x_sample, cache_k, cache_v, cache_kidx, state_conv, page_table, w_in, b_in, conv_w,
           conv_b, conv_ln_g, conv_ln_b, w_conv_out, w_attn_out, w_out, ln1_g, ln1_b, router_group_w,
           router_group_b, router_expert_w, router_expert_b, w_gate, w_up, w_down, ln2_g, ln2_b):
    params = _prepare_params(w_in, b_in, conv_w, conv_b, conv_ln_g, conv_ln_b, w_conv_out, w_attn_out,
                             w_out, ln1_g, ln1_b, router_group_w, router_group_b, router_expert_w,
                             router_expert_b, w_gate, w_up, w_down, ln2_g, ln2_b)
    nb, s, _ = x_prompt.shape
    db, ds, _ = x_sample.shape
    depth, n_phys = cache_k.shape[:2]
    alpha = (2 * depth) ** 0.25
    k_pool = cache_k.reshape(depth * n_phys, PAGE_SIZE, KV_W)
    v_pool = cache_v.reshape(depth * n_phys, PAGE_SIZE, KV_W)
    ki_pool = cache_kidx.reshape(depth * n_phys, PAGE_SIZE, IDX_DIM)
    conv_zero = jnp.zeros((nb, CONV_W - 1, C_CONV), F32)

    def layer(carry, xs):
        xp, xsm = carry
        l, st = xs
        lidx = l.reshape(1)
        prompt_attend = functools.partial(_dsa_prompt, nb=nb, s=s)
        xp, *new_p = _decoder_layer(lidx, xp, conv_zero, prompt_attend, params, alpha)
        sample_attend = functools.partial(_dsa_sample, k_pool=k_pool, v_pool=v_pool, ki_pool=ki_pool,
                                          page_table=page_table + l * n_phys, nb=db, s=ds)
        xsm, *new_s = _decoder_layer(lidx, xsm, st, sample_attend, params, alpha)
        return (xp, xsm), (tuple(new_p), tuple(new_s))

    (xp, xsm), (new_p, new_s) = lax.scan(layer, (x_prompt, x_sample),
                                         (jnp.arange(depth, dtype=I32), state_conv))
    return (xp, xsm, *new_p, *new_s)
```

```python
import functools

import jax
import jax.numpy as jnp
from jax import lax
from jax.experimental import pallas as pl
from jax.experimental.pallas import tpu as pltpu

F32 = jnp.float32
BF16 = jnp.bfloat16
I32 = jnp.int32

D_MODEL = 1024
PAGE_SIZE = 128
C_CONV = 512
CONV_W = 31
N_HEADS = 8
N_KV_HEADS = 4
HEAD_DIM = 64
Q_PER_KV = N_HEADS // N_KV_HEADS
ATTN_W = N_HEADS * HEAD_DIM
KV_W = N_KV_HEADS * HEAD_DIM
N_IDX_HEADS = 8
IDX_DIM = 64
TOPK_MAX = 256
N_GROUPS = 4
EXP_PER_GROUP = 8
N_EXPERTS = N_GROUPS * EXP_PER_GROUP
D_EXPERT = 256
LN_EPS = 1e-5

LANES = 128
SUBLANES_BF16 = 16
VMEM_LIMIT = 56 * 1024 * 1024

_N_SMALL = IDX_DIM + N_IDX_HEADS
_COL_CA = 0
_COL_CB = _COL_CA + C_CONV
_COL_Q = _COL_CB + C_CONV
_COL_K = _COL_Q + ATTN_W
_COL_V = _COL_K + KV_W
_COL_QI = _COL_V + KV_W
_COL_KI = _COL_QI + N_IDX_HEADS * IDX_DIM
_COL_G = _COL_KI + LANES
_N_IN_PAD = _COL_G + 2 * D_MODEL
_N_IN_HEAD = _COL_KI + _N_SMALL

_NT = (((1,), (1,)), ((), ()))
_NEG = -1e30


def _params(*sem):
    return pltpu.CompilerParams(dimension_semantics=sem, vmem_limit_bytes=VMEM_LIMIT)


def _lspec(*shape):
    nd = len(shape)
    return pl.BlockSpec((None,) + shape, lambda *a: (a[-1][0],) + (0,) * nd)


def _layer_norm(x, g, b):
    mu = jnp.mean(x, axis=-1, keepdims=True)
    xc = x - mu
    var = jnp.mean(xc * xc, axis=-1, keepdims=True)
    return xc * lax.rsqrt(var + LN_EPS) * g + b


def _inproj_kernel(l_ref, x_ref, w_ref, b_ref, u_ref, q_ref, k_ref, v_ref, qi_ref, ki_ref, wi_ref, g_ref,
                   kb_ref, vb_ref, kib_ref):
    xb = x_ref[...].astype(BF16)

    def proj(c0, n):
        return jnp.dot(xb, w_ref[:, c0:c0 + n], preferred_element_type=F32) + b_ref[:, c0:c0 + n]

    u_ref[...] = proj(_COL_CA, C_CONV) * jax.nn.sigmoid(proj(_COL_CB, C_CONV))
    q_ref[...] = (proj(_COL_Q, ATTN_W) * (HEAD_DIM ** -0.5)).astype(BF16)
    k = proj(_COL_K, KV_W)
    k_ref[...] = k
    kb_ref[...] = k.astype(BF16)
    v = proj(_COL_V, KV_W)
    v_ref[...] = v
    vb_ref[...] = v.astype(BF16)
    qi_ref[...] = (proj(_COL_QI, N_IDX_HEADS * IDX_DIM) * (IDX_DIM ** -0.5)).astype(BF16)
    small = proj(_COL_KI, LANES)
    ki = small[:, :IDX_DIM]
    ki_ref[...] = ki
    kib_ref[...] = ki.astype(BF16)
    wi_ref[...] = small[:, IDX_DIM:_N_SMALL] * (N_IDX_HEADS ** -0.5)
    g_ref[...] = jax.nn.sigmoid(proj(_COL_G, 2 * D_MODEL))


def _inproj(lidx, x2d, w_pad, b_pad, tm):
    t = x2d.shape[0]
    widths = [(C_CONV, F32), (ATTN_W, BF16), (KV_W, F32), (KV_W, F32), (N_IDX_HEADS * IDX_DIM, BF16),
              (IDX_DIM, F32), (N_IDX_HEADS, F32), (2 * D_MODEL, F32), (KV_W, BF16), (KV_W, BF16),
              (IDX_DIM, BF16)]
    return pl.pallas_call(
        _inproj_kernel,
        out_shape=[jax.ShapeDtypeStruct((t, n), dt) for n, dt in widths],
        grid_spec=pltpu.PrefetchScalarGridSpec(
            num_scalar_prefetch=1, grid=(t // tm,),
            in_specs=[pl.BlockSpec((tm, D_MODEL), lambda i, l: (i, 0)),
                      _lspec(D_MODEL, _N_IN_PAD), _lspec(1, _N_IN_PAD)],
            out_specs=[pl.BlockSpec((tm, n), lambda i, l: (i, 0)) for n, _ in widths]),
        compiler_params=_params("parallel"),
        name="inproj",
    )(lidx, x2d, w_pad, b_pad)


_HALO = 32
_CONV_ROWS = 32


def _conv_kernel(l_ref, prev_ref, cur_ref, cw_ref, cb_ref, lg_ref, lb_ref, wo_ref, y_ref, hist_ref, acc_ref):
    tm = cur_ref.shape[1]
    hist_ref[0:_HALO, :] = prev_ref[0, tm - _HALO:tm, :]
    hist_ref[_HALO:_HALO + tm, :] = cur_ref[0]
    first = _HALO - (CONV_W - 1)
    for r0 in range(0, tm, _CONV_ROWS):
        acc = jnp.zeros((_CONV_ROWS, C_CONV), F32)
        for j in range(CONV_W):
            acc = acc + cw_ref[j:j + 1, :] * hist_ref[first + r0 + j:first + r0 + j + _CONV_ROWS, :]
        acc_ref[r0:r0 + _CONV_ROWS, :] = acc + cb_ref[...]
    y = _layer_norm(acc_ref[...], lg_ref[...], lb_ref[...])
    y = y * jax.nn.sigmoid(y)
    y_ref[0] = jnp.dot(y.astype(BF16), wo_ref[...], preferred_element_type=F32)


def _conv_branch(lidx, uh, conv_w, conv_b, ln_g, ln_b, w_out_bf, tm):
    nb, lp, _ = uh.shape
    nt = lp // tm - 1
    return pl.pallas_call(
        _conv_kernel,
        out_shape=jax.ShapeDtypeStruct((nb, nt * tm, D_MODEL), F32),
        grid_spec=pltpu.PrefetchScalarGridSpec(
            num_scalar_prefetch=1, grid=(nb, nt),
            in_specs=[pl.BlockSpec((1, tm, C_CONV), lambda b, i, l: (b, i, 0)),
                      pl.BlockSpec((1, tm, C_CONV), lambda b, i, l: (b, i + 1, 0)),
                      _lspec(CONV_W, C_CONV), _lspec(1, C_CONV), _lspec(1, C_CONV), _lspec(1, C_CONV),
                      _lspec(C_CONV, D_MODEL)],
            out_specs=pl.BlockSpec((1, tm, D_MODEL), lambda b, i, l: (b, i, 0)),
            scratch_shapes=[pltpu.VMEM((_HALO + tm, C_CONV), F32), pltpu.VMEM((tm, C_CONV), F32)]),
        compiler_params=_params("parallel", "parallel"),
        name="conv_branch",
    )(lidx, uh, uh, conv_w, conv_b, ln_g, ln_b, w_out_bf)


_BISECT_CAP = 40
_STEPS_PER_CHECK = 4
_IDX_STEPS = 14


_SUBLANES = 8
_FOLD_ROWS = 64


def _reduce_rows(x, op):
    rows, cols = x.shape
    if rows > _FOLD_ROWS:
        x = op(x.reshape(rows // _FOLD_ROWS, _FOLD_ROWS, cols), axis=0)
    x = op(x.reshape(x.shape[0] // _SUBLANES, _SUBLANES, cols), axis=0)
    return op(x, axis=0, keepdims=True)


def _fold_keys(sc_ref, n_chunks, tk, init, fn, op, merge):
    def body(c, acc):
        off = pl.multiple_of(c * tk, tk)
        v = fn(sc_ref[pl.ds(off, tk), :], off).reshape(tk // _FOLD_ROWS, _FOLD_ROWS, LANES)
        return merge(acc, op(v, axis=0))

    acc = lax.fori_loop(0, n_chunks, body, jnp.full((_FOLD_ROWS, LANES), init, F32))
    return _reduce_rows(acc, op)


def _count_keys(sc_ref, n_chunks, tk, fn):
    return _fold_keys(sc_ref, n_chunks, tk, 0.0, fn, jnp.sum, jnp.add)


def _count_ge(sc_ref, n_chunks, tk, thr):
    return _count_keys(sc_ref, n_chunks, tk, lambda blk, off: jnp.where(blk >= thr, 1.0, 0.0))


def _any(flag):
    return jnp.max(jnp.where(flag, 1, 0))


def _select_threshold(sc_ref, st_ref, n_chunks, tk, kt, s_min, s_max, n_valid):
    lo_ref, hi_ref, cl_ref, done_ref, jlo_ref, jhi_ref = (st_ref.at[n:n + 1] for n in range(6))
    c_max = _count_ge(sc_ref, n_chunks, tk, s_max)
    top = c_max >= kt
    lo_ref[...] = jnp.where(top, s_max, s_min)
    hi_ref[...] = s_max
    cl0 = jnp.where(top, c_max, n_valid)
    cl_ref[...] = cl0
    done0 = jnp.where(top, 1.0, jnp.where(cl0 == kt, 1.0, 0.0))
    done_ref[...] = done0

    def bisect_step():
        lo, hi, cl, done = lo_ref[...], hi_ref[...], cl_ref[...], done_ref[...]
        mid = 0.5 * lo + 0.5 * hi
        c = _count_ge(sc_ref, n_chunks, tk, mid)
        act = done == 0.0
        up = c >= kt
        lo_ref[...] = jnp.where(act, jnp.where(up, mid, lo), lo)
        cl_ref[...] = jnp.where(act, jnp.where(up, c, cl), cl)
        hi_ref[...] = jnp.where(act, jnp.where(up, hi, mid), hi)
        done_ref[...] = jnp.where(act, jnp.where(c == kt, 1.0, 0.0), done)

    def bisect(carry):
        for _ in range(_STEPS_PER_CHECK):
            bisect_step()
        return carry[0] + 1, _any(done_ref[...] == 0.0)

    lax.while_loop(lambda c: (c[0] < _BISECT_CAP // _STEPS_PER_CHECK) & (c[1] > 0), bisect,
                   (jnp.int32(0), _any(done0 == 0.0)))

    def snap(_):
        lo, hi, cl, done = lo_ref[...], hi_ref[...], cl_ref[...], done_ref[...]
        t1 = _fold_keys(sc_ref, n_chunks, tk, -jnp.inf, lambda blk, off: jnp.where(blk < hi, blk, -jnp.inf),
                        jnp.max, jnp.maximum)
        c1 = _count_ge(sc_ref, n_chunks, tk, t1)
        act = done == 0.0
        found = c1 >= kt
        lo_ref[...] = jnp.where(act, jnp.where(found, t1, lo), lo)
        cl_ref[...] = jnp.where(act, jnp.where(found, c1, cl), cl)
        hi_ref[...] = jnp.where(act, jnp.where(found, hi, t1), hi)
        done_new = jnp.where(act, jnp.where(found, 1.0, 0.0), done)
        done_ref[...] = done_new
        return _any(done_new == 0.0)

    lax.while_loop(lambda f: f > 0, snap, _any(done_ref[...] == 0.0))

    thr = lo_ref[...]

    @pl.when(_any(cl_ref[...] != kt) > 0)
    def _cut_ties():
        need = kt - _count_keys(sc_ref, n_chunks, tk, lambda blk, off: jnp.where(blk > thr, 1.0, 0.0))
        jlo_ref[...] = jnp.full_like(thr, -1.0)
        jhi_ref[...] = jnp.full_like(thr, float(sc_ref.shape[0] - 1))
        row = lax.broadcasted_iota(I32, (tk, 1), 0)

        def step(_, carry):
            jlo, jhi = jlo_ref[...], jhi_ref[...]
            mid = jnp.floor(0.5 * (jlo + jhi))
            c = _count_keys(
                sc_ref, n_chunks, tk,
                lambda blk, off: jnp.where(blk == thr, jnp.where((off + row).astype(F32) <= mid, 1.0, 0.0), 0.0))
            ok = c >= need
            jhi_ref[...] = jnp.where(ok, mid, jhi)
            jlo_ref[...] = jnp.where(ok, jlo, mid)
            return carry

        lax.fori_loop(0, _IDX_STEPS, step, 0)
        cut = jhi_ref[...]

        def demote(c, carry):
            off = pl.multiple_of(c * tk, tk)
            blk = sc_ref[pl.ds(off, tk), :]
            beyond = (off + row).astype(F32) > cut
            sc_ref[pl.ds(off, tk), :] = jnp.where(blk == thr, jnp.where(beyond, -jnp.inf, blk), blk)
            return carry

        lax.fori_loop(0, n_chunks, demote, 0)

    return thr


def _min_max_init():
    return (jnp.full((_FOLD_ROWS, LANES), jnp.inf, F32), jnp.full((_FOLD_ROWS, LANES), -jnp.inf, F32))


def _min_max_update(mn, mx, score, valid):
    shape = (score.shape[0] // _FOLD_ROWS, _FOLD_ROWS, LANES)
    mn = jnp.minimum(mn, jnp.min(jnp.where(valid, score, jnp.inf).reshape(shape), axis=0))
    mx = jnp.maximum(mx, jnp.max(jnp.where(valid, score, -jnp.inf).reshape(shape), axis=0))
    return mn, mx


_TQ = 128
_TK = 512


_IDX_PAIRS = N_IDX_HEADS // 2


def _dsa_prompt_kernel(qi_ref, wi_ref, ki_ref, q_ref, k_ref, vt_ref, o_ref,
                       sc_ref, st_ref, m_ref, l_ref, acc_ref, s_ref, cm_ref):
    tq, tk = _TQ, _TK
    i = pl.program_id(1)
    n_chunks = ((i + 1) * tq + tk - 1) // tk
    qpos = i * tq + lax.broadcasted_iota(I32, (1, tq), 1)
    krow = lax.broadcasted_iota(I32, (tk, 1), 0)

    def scores(c, carry):
        off = pl.multiple_of(c * tk, tk)
        kc = ki_ref[0, pl.ds(off, tk), :]
        score = None
        for p in range(_IDX_PAIRS):
            d = lax.dot_general(kc, qi_ref[0, 0, p], _NT, preferred_element_type=F32)
            for j in range(2):
                h = 2 * p + j
                t = wi_ref[0, h:h + 1, :] * jnp.maximum(d[:, j * tq:(j + 1) * tq], 0.0)
                score = t if score is None else score + t
        score = jnp.where(score == 0.0, 0.0, score)
        valid = off + krow <= qpos
        sc_ref[pl.ds(off, tk), :] = jnp.where(valid, score, -jnp.inf)
        return _min_max_update(*carry, score, valid)

    mn, mx = lax.fori_loop(0, n_chunks, scores, _min_max_init())
    n_valid = (qpos + 1).astype(F32)
    kt = jnp.minimum(n_valid, float(TOPK_MAX))
    thr = _select_threshold(sc_ref, st_ref, n_chunks, tk, kt, _reduce_rows(mn, jnp.min),
                            _reduce_rows(mx, jnp.max), n_valid)

    m_ref[...] = jnp.full(m_ref.shape, _NEG, F32)
    l_ref[...] = jnp.zeros(l_ref.shape, F32)
    acc_ref[...] = jnp.zeros(acc_ref.shape, F32)

    def qk_stage(c, slot):
        off = pl.multiple_of(c * tk, tk)
        bias = jnp.where(sc_ref[pl.ds(off, tk), :] >= thr, 0.0, -jnp.inf)
        bias = jnp.concatenate([bias] * Q_PER_KV, axis=1)
        for g in range(N_KV_HEADS):
            s = lax.dot_general(k_ref[0, g, pl.ds(off, tk), :], q_ref[0, 0, g], _NT,
                                preferred_element_type=F32) + bias
            s_ref[slot, g] = s
            cm_ref[slot, g] = _reduce_rows(s, jnp.max)

    def pv_stage(c, slot):
        off = pl.multiple_of(c * tk, tk)
        for g in range(N_KV_HEADS):
            m_old = m_ref[g]
            m_new = jnp.maximum(m_old, cm_ref[slot, g])
            alpha = jnp.exp(m_old - m_new)
            p = jnp.exp(s_ref[slot, g] - m_new)
            l_ref[g] = alpha * l_ref[g] + _reduce_rows(p, jnp.sum)
            acc_ref[g] = alpha * acc_ref[g] + jnp.dot(vt_ref[0, g, :, pl.ds(off, tk)], p.astype(BF16),
                                                      preferred_element_type=F32)
            m_ref[g] = m_new

    def attend(c, carry):
        slot = c % 2
        qk_stage(c + 1, 1 - slot)
        pv_stage(c, slot)
        return carry

    qk_stage(0, 0)
    lax.fori_loop(0, n_chunks - 1, attend, 0)
    pv_stage(n_chunks - 1, (n_chunks - 1) % 2)
    for g in range(N_KV_HEADS):
        o_ref[0, 0, g] = acc_ref[g] / l_ref[g]


def _dsa_prompt(q_b, k_b, v_b, qi_b, ki_b, wi, nb, s):
    tq = _TQ
    nq = s // tq
    cols = Q_PER_KV * tq

    def pair_major(a, n_pairs):
        a = a.reshape(nb, nq, tq, n_pairs, 2, a.shape[-1] // (2 * n_pairs)).transpose(0, 1, 3, 4, 2, 5)
        return a.reshape(nb, nq, n_pairs, 2 * tq, -1)

    kh = k_b.reshape(nb, s, N_KV_HEADS, HEAD_DIM).transpose(0, 2, 1, 3)
    vt = v_b.reshape(nb, s, N_KV_HEADS, HEAD_DIM).transpose(0, 2, 3, 1)
    wit = wi.reshape(nb, s, N_IDX_HEADS).transpose(0, 2, 1)
    once = dict(pipeline_mode=pl.Buffered(1))
    o = pl.pallas_call(
        _dsa_prompt_kernel,
        out_shape=jax.ShapeDtypeStruct((nb, nq, N_KV_HEADS, HEAD_DIM, cols), F32),
        grid=(nb, nq),
        in_specs=[pl.BlockSpec((1, 1, _IDX_PAIRS, 2 * tq, IDX_DIM), lambda b, i: (b, i, 0, 0, 0)),
                  pl.BlockSpec((1, N_IDX_HEADS, tq), lambda b, i: (b, 0, i)),
                  pl.BlockSpec((1, s, IDX_DIM), lambda b, i: (b, 0, 0), **once),
                  pl.BlockSpec((1, 1, N_KV_HEADS, cols, HEAD_DIM), lambda b, i: (b, i, 0, 0, 0)),
                  pl.BlockSpec((1, N_KV_HEADS, s, HEAD_DIM), lambda b, i: (b, 0, 0, 0), **once),
                  pl.BlockSpec((1, N_KV_HEADS, HEAD_DIM, s), lambda b, i: (b, 0, 0, 0), **once)],
        out_specs=pl.BlockSpec((1, 1, N_KV_HEADS, HEAD_DIM, cols), lambda b, i: (b, i, 0, 0, 0)),
        scratch_shapes=[pltpu.VMEM((s, tq), F32), pltpu.VMEM((_SUBLANES, tq), F32),
                        pltpu.VMEM((N_KV_HEADS, 1, cols), F32), pltpu.VMEM((N_KV_HEADS, 1, cols), F32),
                        pltpu.VMEM((N_KV_HEADS, HEAD_DIM, cols), F32),
                        pltpu.VMEM((2, N_KV_HEADS, _TK, cols), F32), pltpu.VMEM((2, N_KV_HEADS, 1, cols), F32)],
        compiler_params=_params("parallel", "arbitrary"),
        name="dsa_prompt",
    )(pair_major(qi_b, _IDX_PAIRS), wit, ki_b.reshape(nb, s, IDX_DIM), pair_major(q_b, N_KV_HEADS), kh, vt)
    o = o.reshape(nb, nq, N_KV_HEADS, HEAD_DIM, Q_PER_KV, tq).transpose(0, 1, 5, 2, 4, 3)
    return o.reshape(nb * s, ATTN_W)


_QPAD = 8
_PAGES_SC = 16
_PAGES_KV = 8


def _sample_scores_kernel(pt_ref, qi_ref, wi_ref, knt_ref, *rest):
    pages, sc_ref = rest[:_PAGES_SC], rest[_PAGES_SC]
    j = pl.program_id(1)
    last = pl.num_programs(1) - 1

    def scores(kt):
        d = jnp.dot(qi_ref[0], kt, preferred_element_type=F32)
        score = None
        for h in range(N_IDX_HEADS):
            t = wi_ref[0, :, h:h + 1] * jnp.maximum(d[h * _QPAD:(h + 1) * _QPAD, :], 0.0)
            score = t if score is None else score + t
        return jnp.where(score == 0.0, 0.0, score)

    @pl.when(j < last)
    def _past():
        sc_ref[0] = scores(jnp.concatenate([pg[0] for pg in pages], axis=1).astype(BF16))

    @pl.when(j == last)
    def _new():
        sc_ref[0] = jnp.zeros(sc_ref.shape[1:], F32)
        sc_ref[0, :, 0:PAGE_SIZE] = scores(knt_ref[0])


def _sample_select_kernel(sc_in_ref, bias_ref, sc_ref, st_ref, *, past, dec_seq):
    tk = _TK
    n_chunks = sc_ref.shape[0] // tk
    q = lax.broadcasted_iota(I32, (1, LANES), 1) % dec_seq
    krow = lax.broadcasted_iota(I32, (tk, 1), 0)

    def load(c, carry):
        off = pl.multiple_of(c * tk, tk)
        s = sc_in_ref[pl.ds(off, tk), :]
        valid = off + krow <= past + q
        sc_ref[pl.ds(off, tk), :] = jnp.where(valid, s, -jnp.inf)
        return _min_max_update(*carry, s, valid)

    mn, mx = lax.fori_loop(0, n_chunks, load, _min_max_init())
    n_valid = (past + 1 + q).astype(F32)
    kt = jnp.minimum(n_valid, float(min(TOPK_MAX, (past + dec_seq) // 4)))
    thr = _select_threshold(sc_ref, st_ref, n_chunks, tk, kt, _reduce_rows(mn, jnp.min),
                            _reduce_rows(mx, jnp.max), n_valid)

    def emit(c, carry):
        off = pl.multiple_of(c * tk, tk)
        bias_ref[pl.ds(off, tk), :] = jnp.where(sc_ref[pl.ds(off, tk), :] >= thr, 0.0, -jnp.inf)
        return carry

    lax.fori_loop(0, n_chunks, emit, 0)


def _sample_attend_kernel(pt_ref, q_ref, bias_ref, knt_ref, vnt_ref, *rest):
    kpages, vpages = rest[:_PAGES_KV], rest[_PAGES_KV:2 * _PAGES_KV]
    o_ref, m_ref, l_ref, acc_ref = rest[2 * _PAGES_KV:]
    j = pl.program_id(1)
    last = pl.num_programs(1) - 1

    @pl.when(j == 0)
    def _init():
        m_ref[...] = jnp.full(m_ref.shape, _NEG, F32)
        l_ref[...] = jnp.zeros(l_ref.shape, F32)
        acc_ref[...] = jnp.zeros(acc_ref.shape, F32)

    def attend(kt, vt, bias):
        s = jnp.dot(q_ref[0], kt, preferred_element_type=F32)
        s = jnp.concatenate([s[h * _QPAD:(h + 1) * _QPAD] + bias for h in range(N_HEADS)], axis=0)
        m_old = m_ref[...]
        m_new = jnp.maximum(m_old, jnp.max(s, axis=1, keepdims=True))
        alpha = jnp.exp(m_old - m_new)
        p = jnp.exp(s - m_new)
        l_ref[...] = alpha * l_ref[...] + jnp.sum(p, axis=1, keepdims=True)
        acc_ref[...] = alpha * acc_ref[...] + lax.dot_general(p.astype(BF16), vt, _NT,
                                                              preferred_element_type=F32)
        m_ref[...] = m_new

    @pl.when(j < last)
    def _past():
        attend(jnp.concatenate([pg[0] for pg in kpages], axis=1).astype(BF16),
               jnp.concatenate([pg[0] for pg in vpages], axis=1).astype(BF16), bias_ref[0])

    @pl.when(j == last)
    def _new():
        attend(knt_ref[0], vnt_ref[0], bias_ref[0, :, 0:PAGE_SIZE])
        o_ref[0] = acc_ref[...] / l_ref[...]


def _dsa_sample(q_b, k_b, v_b, qi_b, ki_b, wi, k_pool, v_pool, ki_pool, page_table, nb, s):
    n_pages = page_table.shape[1]
    past = n_pages * PAGE_SIZE
    assert _QPAD % s == 0 and (nb * _QPAD) % LANES == 0
    dup = jnp.arange(_QPAD) % s
    qi = qi_b.reshape(nb, s, N_IDX_HEADS, IDX_DIM)[:, dup].transpose(0, 2, 1, 3)
    qi = qi.reshape(nb, N_IDX_HEADS * _QPAD, IDX_DIM)
    wi8 = wi.reshape(nb, s, N_IDX_HEADS)[:, dup]
    new_page = lambda a: jnp.pad(a.reshape(nb, s, -1).transpose(0, 2, 1), ((0, 0), (0, 0), (0, PAGE_SIZE - s)))
    kint, knt, vnt = new_page(ki_b), new_page(k_b), new_page(v_b)

    n_sc = n_pages // _PAGES_SC
    sc_w = _PAGES_SC * PAGE_SIZE

    def page_spec(width, per_step, r):
        return pl.BlockSpec((1, width, PAGE_SIZE),
                            lambda b, j, pt: (pt[b, jnp.minimum(j * per_step + r, n_pages - 1)], 0, 0))

    scores = pl.pallas_call(
        _sample_scores_kernel,
        out_shape=jax.ShapeDtypeStruct((nb, _QPAD, (n_sc + 1) * sc_w), F32),
        grid_spec=pltpu.PrefetchScalarGridSpec(
            num_scalar_prefetch=1, grid=(nb, n_sc + 1),
            in_specs=[pl.BlockSpec((1, N_IDX_HEADS * _QPAD, IDX_DIM), lambda b, j, pt: (b, 0, 0)),
                      pl.BlockSpec((1, _QPAD, N_IDX_HEADS), lambda b, j, pt: (b, 0, 0)),
                      pl.BlockSpec((1, IDX_DIM, PAGE_SIZE), lambda b, j, pt: (b, 0, 0))]
                     + [page_spec(IDX_DIM, _PAGES_SC, r) for r in range(_PAGES_SC)],
            out_specs=pl.BlockSpec((1, _QPAD, sc_w), lambda b, j, pt: (b, 0, j))),
        compiler_params=_params("parallel", "arbitrary"),
        name="sample_scores",
    )(page_table, qi, wi8, kint, *([ki_pool] * _PAGES_SC))

    width = scores.shape[2]
    cols = nb * _QPAD
    bias = pl.pallas_call(
        functools.partial(_sample_select_kernel, past=past, dec_seq=s),
        out_shape=jax.ShapeDtypeStruct((width, cols), F32),
        grid=(cols // LANES,),
        in_specs=[pl.BlockSpec((width, LANES), lambda i: (0, i))],
        out_specs=pl.BlockSpec((width, LANES), lambda i: (0, i)),
        scratch_shapes=[pltpu.VMEM((width, LANES), F32), pltpu.VMEM((_SUBLANES, LANES), F32)],
        compiler_params=_params("parallel"),
        name="sample_select",
    )(scores.reshape(cols, width).T).T.reshape(nb, _QPAD, width)

    q4 = q_b.reshape(nb, s, N_KV_HEADS, Q_PER_KV, HEAD_DIM)[:, dup]
    eye = jnp.eye(N_KV_HEADS, dtype=q_b.dtype)
    qx = jnp.einsum("bqgjd,gk->bgjqkd", q4, eye).reshape(nb, N_HEADS * _QPAD, KV_W)
    n_kv = n_pages // _PAGES_KV
    kv_w = _PAGES_KV * PAGE_SIZE
    new_blk = past // kv_w
    hq = N_HEADS * _QPAD
    o = pl.pallas_call(
        _sample_attend_kernel,
        out_shape=jax.ShapeDtypeStruct((nb, hq, KV_W), F32),
        grid_spec=pltpu.PrefetchScalarGridSpec(
            num_scalar_prefetch=1, grid=(nb, n_kv + 1),
            in_specs=[pl.BlockSpec((1, hq, KV_W), lambda b, j, pt: (b, 0, 0)),
                      pl.BlockSpec((1, _QPAD, kv_w), lambda b, j, pt: (b, 0, jnp.minimum(j, new_blk))),
                      pl.BlockSpec((1, KV_W, PAGE_SIZE), lambda b, j, pt: (b, 0, 0)),
                      pl.BlockSpec((1, KV_W, PAGE_SIZE), lambda b, j, pt: (b, 0, 0))]
                     + [page_spec(KV_W, _PAGES_KV, r) for r in range(_PAGES_KV)] * 2,
            out_specs=pl.BlockSpec((1, hq, KV_W), lambda b, j, pt: (b, 0, 0)),
            scratch_shapes=[pltpu.VMEM((hq, 1), F32), pltpu.VMEM((hq, 1), F32), pltpu.VMEM((hq, KV_W), F32)]),
        compiler_params=_params("parallel", "arbitrary"),
        name="sample_attend",
    )(page_table, qx, bias, knt, vnt, *([k_pool] * _PAGES_KV), *([v_pool] * _PAGES_KV))
    o = o.reshape(nb, N_KV_HEADS, Q_PER_KV, _QPAD, N_KV_HEADS, HEAD_DIM)[:, :, :, :s]
    o = jnp.einsum("bgjqkd,gk->bqgjd", o, jnp.eye(N_KV_HEADS, dtype=o.dtype))
    return o.reshape(nb * s, ATTN_W)


_ROUTE_W = 8


def _merge_kernel(l_ref, x_ref, yc_ref, o_ref, g_ref, wao_ref, wout_ref, lg_ref, lb_ref, rwh_ref, rwl_ref,
                  rb_ref, x1_ref, x1b_ref, route_ref, *, alpha):
    y_attn = jnp.dot(o_ref[...].astype(BF16), wao_ref[...], preferred_element_type=F32)
    mix_in = g_ref[:, :D_MODEL] * yc_ref[...] + g_ref[:, D_MODEL:] * y_attn
    mix = jnp.dot(mix_in.astype(BF16), wout_ref[...], preferred_element_type=F32)
    x1 = _layer_norm(alpha * x_ref[...] + mix, lg_ref[...], lb_ref[...])
    x1_ref[...] = x1
    hi = x1.astype(BF16)
    x1b_ref[...] = hi

    lo = (x1 - hi.astype(F32)).astype(BF16)
    logits = (jnp.dot(hi, rwh_ref[...], preferred_element_type=F32)
              + jnp.dot(hi, rwl_ref[...], preferred_element_type=F32)
              + jnp.dot(lo, rwh_ref[...], preferred_element_type=F32)) + rb_ref[...]
    lane = lax.broadcasted_iota(I32, (1, LANES), 1)

    def first_max(v):
        m = jnp.max(v, axis=1, keepdims=True)
        return m, jnp.min(jnp.where(v == m, lane, LANES), axis=1, keepdims=True)

    is_group = (lane >= N_EXPERTS) & (lane < N_EXPERTS + N_GROUPS)
    gl = jnp.where(is_group, logits, -jnp.inf)
    gm, gidx = first_max(gl)
    p_grp = 1.0 / jnp.sum(jnp.exp(gl - gm), axis=1, keepdims=True)
    grp = gidx - N_EXPERTS
    group_of_lane = lane >> (EXP_PER_GROUP.bit_length() - 1)
    el = jnp.where(group_of_lane == grp, logits, -jnp.inf)
    e1, i1 = first_max(el)
    e2, i2 = first_max(jnp.where(lane == i1, -jnp.inf, el))
    t = jnp.exp(e2 - e1)
    g1 = p_grp / (1.0 + t)
    g2 = g1 * t
    rl = lax.broadcasted_iota(I32, (1, _ROUTE_W), 1)
    route_ref[...] = jnp.where(rl == 0, i1.astype(F32), jnp.where(rl == 1, i2.astype(F32),
                               jnp.where(rl == 2, g1, jnp.where(rl == 3, g2, 0.0))))


def _merge(lidx, x2d, yc, o, g, p, tm, alpha):
    t = x2d.shape[0]
    row = lambda n: pl.BlockSpec((tm, n), lambda i, l: (i, 0))
    return pl.pallas_call(
        functools.partial(_merge_kernel, alpha=alpha),
        out_shape=[jax.ShapeDtypeStruct((t, D_MODEL), F32), jax.ShapeDtypeStruct((t, D_MODEL), BF16),
                   jax.ShapeDtypeStruct((t, _ROUTE_W), F32)],
        grid_spec=pltpu.PrefetchScalarGridSpec(
            num_scalar_prefetch=1, grid=(t // tm,),
            in_specs=[row(D_MODEL), row(D_MODEL), row(ATTN_W), row(2 * D_MODEL),
                      _lspec(ATTN_W, D_MODEL), _lspec(D_MODEL, D_MODEL), _lspec(1, D_MODEL),
                      _lspec(1, D_MODEL), _lspec(D_MODEL, LANES), _lspec(D_MODEL, LANES), _lspec(1, LANES)],
            out_specs=[row(D_MODEL), row(D_MODEL), row(_ROUTE_W)]),
        compiler_params=_params("parallel"),
        name="merge_ln1_router",
    )(lidx, x2d, yc, o, g, p["w_attn_out"], p["w_out"], p["ln1_g"], p["ln1_b"], p["router_hi"],
      p["router_lo"], p["router_b"])


_SLOT_ALIGN = SUBLANES_BF16
_ROW_BLK = 64
_MOE_CHUNK = 256
_META_W = LANES
_MOE_TILE_MAX = 256 * _SLOT_ALIGN


def _moe_plan_kernel(route_ref, dcol_ref, drow_ref, meta_ref):
    tm = route_ref.shape[0]
    blk_rows = min(_MOE_CHUNK, tm)
    lane = lax.broadcasted_iota(I32, (1, LANES), 1).astype(F32)
    hit0 = lane == route_ref[:, 0:1]
    hit1 = lane == route_ref[:, 1:2]
    onehot = jnp.where(hit0, 1.0, jnp.where(hit1, 1.0, 0.0))
    r = lax.broadcasted_iota(I32, (blk_rows, blk_rows), 0)
    c = lax.broadcasted_iota(I32, (blk_rows, blk_rows), 1)
    tri = jnp.where(c < r, 1.0, 0.0).astype(BF16)
    carry = jnp.zeros((1, LANES), F32)
    prefix = []
    for b0 in range(0, tm, blk_rows):
        blk = onehot[b0:b0 + blk_rows]
        prefix.append(jnp.dot(tri, blk.astype(BF16), preferred_element_type=F32) + carry)
        carry = carry + jnp.sum(blk, axis=0, keepdims=True)
    prefix = jnp.concatenate(prefix, axis=0)
    units = jnp.ceil(carry * (1.0 / _SLOT_ALIGN))
    rr = lax.broadcasted_iota(I32, (LANES, LANES), 0)
    cc = lax.broadcasted_iota(I32, (LANES, LANES), 1)
    upper = jnp.where(rr < cc, 1.0, 0.0).astype(BF16)
    units8 = jnp.broadcast_to(units, (8, LANES)).astype(BF16)
    offs = jnp.dot(units8, upper, preferred_element_type=F32)[0:1] * _SLOT_ALIGN
    slot = offs + prefix
    d0 = jnp.sum(jnp.where(hit0, slot, 0.0), axis=1, keepdims=True)
    d1 = jnp.sum(jnp.where(hit1, slot, 0.0), axis=1, keepdims=True)
    rl = lax.broadcasted_iota(I32, (1, LANES), 1)
    rec = jnp.where(rl == 0, d0, jnp.where(rl == 1, d1, jnp.where(rl == 2, route_ref[:, 2:3],
                    jnp.where(rl == 3, route_ref[:, 3:4], 0.0))))
    dcol_ref[...] = rec[:, :_ROUTE_W]
    drow_ref[...] = rec.T[:_ROUTE_W, :]
    nblk = jnp.ceil(units * (_SLOT_ALIGN / _ROW_BLK))
    shift = jnp.where(cc == rr + N_EXPERTS, 1.0, 0.0).astype(BF16)
    nblk_sh = jnp.dot(jnp.broadcast_to(nblk, (8, LANES)).astype(BF16), shift,
                      preferred_element_type=F32)[0:1]
    meta_ref[0] = jnp.where(rl < N_EXPERTS, offs, nblk_sh).astype(I32)


def _moe_plan(route, tm):
    t = route.shape[0]
    nt = t // tm
    assert tm <= _MOE_TILE_MAX
    return pl.pallas_call(
        _moe_plan_kernel,
        out_shape=[jax.ShapeDtypeStruct((t, _ROUTE_W), F32), jax.ShapeDtypeStruct((_ROUTE_W, t), F32),
                   jax.ShapeDtypeStruct((nt, 1, _META_W), I32)],
        grid=(nt,),
        in_specs=[pl.BlockSpec((tm, _ROUTE_W), lambda i: (i, 0))],
        out_specs=[pl.BlockSpec((tm, _ROUTE_W), lambda i: (i, 0)),
                   pl.BlockSpec((_ROUTE_W, tm), lambda i: (0, i)),
                   pl.BlockSpec((1, 1, _META_W), lambda i: (i, 0, 0))],
        compiler_params=_params("parallel"),
        name="moe_plan",
    )(route)


def _n_slots(tm):
    n = 2 * tm + N_EXPERTS * (_SLOT_ALIGN - 1) + _ROW_BLK
    return -(-n // _MOE_CHUNK) * _MOE_CHUNK


def _moe_kernel(meta_ref, l_ref, x1b_ref, x1_ref, drow_ref, dcol_ref, wg_ref, wu_ref, wd_ref, lg_ref, lb_ref,
                x2_ref, xb_ref, yb_ref, gs_ref, *, alpha):
    i, e = pl.program_id(0), pl.program_id(1)
    tm = x1_ref.shape[0]
    n_slots = xb_ref.shape[0]

    @pl.when(e == 0)
    def _dispatch():
        d0, d1 = drow_ref[0:1, :], drow_ref[1:2, :]
        g0, g1 = drow_ref[2:3, :], drow_ref[3:4, :]
        for s0 in range(0, n_slots, _MOE_CHUNK):
            sl = (s0 + lax.broadcasted_iota(I32, (_MOE_CHUNK, 1), 0)).astype(F32)
            a, b = sl == d0, sl == d1
            p = jnp.where(a, 1.0, jnp.where(b, 1.0, 0.0)).astype(BF16)
            xb_ref[s0:s0 + _MOE_CHUNK, :] = jnp.dot(p, x1b_ref[...], preferred_element_type=F32).astype(BF16)
            gs_ref[s0:s0 + _MOE_CHUNK, :] = jnp.sum(jnp.where(a, g0, jnp.where(b, g1, 0.0)), axis=1,
                                                    keepdims=True)
        yb_ref[...] = jnp.zeros(yb_ref.shape, BF16)

    off = meta_ref[i * _META_W + e]
    nblk = meta_ref[i * _META_W + N_EXPERTS + e]

    def block(r, carry):
        start = pl.multiple_of(off + r * _ROW_BLK, _SLOT_ALIGN)
        xs = xb_ref[pl.ds(start, _ROW_BLK), :]
        hg = jnp.dot(xs, wg_ref[...], preferred_element_type=F32)
        hu = jnp.dot(xs, wu_ref[...], preferred_element_type=F32)
        h = (hg * jax.nn.sigmoid(hg)) * hu
        y = jnp.dot(h.astype(BF16), wd_ref[...], preferred_element_type=F32)
        yb_ref[pl.ds(start, _ROW_BLK), :] = (y * gs_ref[pl.ds(start, _ROW_BLK), :]).astype(BF16)
        return carry

    lax.fori_loop(0, nblk, block, 0)

    @pl.when(e == pl.num_programs(1) - 1)
    def _combine():
        lane = lax.broadcasted_iota(I32, (1, n_slots), 1).astype(F32)
        rows = min(_MOE_CHUNK, tm)
        for t0 in range(0, tm, rows):
            d0, d1 = dcol_ref[t0:t0 + rows, 0:1], dcol_ref[t0:t0 + rows, 1:2]
            qm = jnp.where(lane == d0, 1.0, jnp.where(lane == d1, 1.0, 0.0)).astype(BF16)
            y = jnp.dot(qm, yb_ref[...], preferred_element_type=F32)
            z = alpha * x1_ref[t0:t0 + rows, :] + y
            x2_ref[t0:t0 + rows, :] = _layer_norm(z, lg_ref[...], lb_ref[...])


def _moe(lidx, x1, x1b, drow, dcol, meta, p, tm, alpha):
    t = x1.shape[0]
    n_slots = _n_slots(tm)
    tile = lambda shape, imap: pl.BlockSpec(shape, imap)
    return pl.pallas_call(
        functools.partial(_moe_kernel, alpha=alpha),
        out_shape=jax.ShapeDtypeStruct((t, D_MODEL), F32),
        grid_spec=pltpu.PrefetchScalarGridSpec(
            num_scalar_prefetch=2, grid=(t // tm, N_EXPERTS),
            in_specs=[tile((tm, D_MODEL), lambda i, e, m, l: (i, 0)),
                      tile((tm, D_MODEL), lambda i, e, m, l: (i, 0)),
                      tile((_ROUTE_W, tm), lambda i, e, m, l: (0, i)),
                      tile((tm, _ROUTE_W), lambda i, e, m, l: (i, 0)),
                      tile((None, None, D_MODEL, D_EXPERT), lambda i, e, m, l: (l[0], e, 0, 0)),
                      tile((None, None, D_MODEL, D_EXPERT), lambda i, e, m, l: (l[0], e, 0, 0)),
                      tile((None, None, D_EXPERT, D_MODEL), lambda i, e, m, l: (l[0], e, 0, 0)),
                      _lspec(1, D_MODEL), _lspec(1, D_MODEL)],
            out_specs=tile((tm, D_MODEL), lambda i, e, m, l: (i, 0)),
            scratch_shapes=[pltpu.VMEM((n_slots, D_MODEL), BF16), pltpu.VMEM((n_slots, D_MODEL), BF16),
                            pltpu.VMEM((n_slots, 1), F32)]),
        compiler_params=_params("parallel", "arbitrary"),
        name="moe_experts_ln2",
    )(meta.reshape(-1), lidx, x1b, x1, drow, dcol, p["w_gate"], p["w_up"], p["w_down"], p["ln2_g"], p["ln2_b"])


def _token_tile(t, want):
    return want if t % want == 0 else t


def _decoder_layer(lidx, x, prefix, attend, p, alpha):
    nb, s, _ = x.shape
    t = nb * s
    x2d = x.reshape(t, D_MODEL)
    u, q_b, k, v, qi_b, ki, wi, g, k_b, v_b, ki_b = _inproj(lidx, x2d, p["w_in"], p["b_in"], _token_tile(t, 512))

    ctm = 512 if s % 512 == 0 else _HALO
    u3 = u.reshape(nb, s, C_CONV)
    uh = jnp.concatenate([jnp.zeros((nb, ctm - (CONV_W - 1), C_CONV), F32), prefix, u3,
                          jnp.zeros((nb, (-s) % ctm, C_CONV), F32)], axis=1)
    yc = _conv_branch(lidx, uh, p["conv_w"], p["conv_b"], p["conv_ln_g"], p["conv_ln_b"], p["w_conv_out"], ctm)
    yc = yc[:, :s].reshape(t, D_MODEL)

    o = attend(q_b, k_b, v_b, qi_b, ki_b, wi)
    x1, x1b, route = _merge(lidx, x2d, yc, o, g, p, _token_tile(t, 256), alpha)
    mtm = _token_tile(t, 1024)
    dcol, drow, meta = _moe_plan(route, mtm)
    x2 = _moe(lidx, x1, x1b, drow, dcol, meta, p, mtm, alpha)

    new_conv = jnp.concatenate([prefix, u3], axis=1)[:, -(CONV_W - 1):]
    return (x2.reshape(nb, s, D_MODEL), k.reshape(nb, s, N_KV_HEADS, HEAD_DIM),
            v.reshape(nb, s, N_KV_HEADS, HEAD_DIM), ki.reshape(nb, s, IDX_DIM), new_conv)


def _prepare_params(w_in, b_in, conv_w, conv_b, conv_ln_g, conv_ln_b, w_conv_out, w_attn_out, w_out,
                    ln1_g, ln1_b, router_group_w, router_group_b, router_expert_w, router_expert_b,
                    w_gate, w_up, w_down, ln2_g, ln2_b):
    depth = w_in.shape[0]
    pad = _COL_G - _N_IN_HEAD
    w_pad = jnp.concatenate([w_in[..., :_N_IN_HEAD], jnp.zeros((depth, D_MODEL, pad), F32),
                             w_in[..., _N_IN_HEAD:]], axis=-1).astype(BF16)
    b_pad = jnp.concatenate([b_in[..., :_N_IN_HEAD], jnp.zeros((depth, pad), F32),
                             b_in[..., _N_IN_HEAD:]], axis=-1)[:, None, :]
    rpad = LANES - N_EXPERTS - N_GROUPS
    rw = jnp.concatenate([router_expert_w, router_group_w, jnp.zeros((depth, D_MODEL, rpad), F32)], axis=-1)
    rb = jnp.concatenate([router_expert_b, router_group_b, jnp.zeros((depth, rpad), F32)], axis=-1)
    rw_hi = rw.astype(BF16)
    rw_lo = (rw - rw_hi.astype(F32)).astype(BF16)
    vec = lambda a: a[:, None, :]
    return dict(w_in=w_pad, b_in=b_pad, conv_w=conv_w, conv_b=vec(conv_b), conv_ln_g=vec(conv_ln_g),
                conv_ln_b=vec(conv_ln_b), w_conv_out=w_conv_out.astype(BF16),
                w_attn_out=w_attn_out.astype(BF16), w_out=w_out.astype(BF16), ln1_g=vec(ln1_g),
                ln1_b=vec(ln1_b), router_hi=rw_hi, router_lo=rw_lo, router_b=vec(rb),
                w_gate=w_gate.astype(BF16), w_up=w_up.astype(BF16), w_down=w_down.astype(BF16),
                ln2_g=vec(ln2_g), ln2_b=vec(ln2_b))


def kernel(x_prompt, x_sample, cache_k, cache_v, cache_kidx, state_conv, page_table, w_in, b_in, conv_w,
           conv_b, conv_ln_g, conv_ln_b, w_conv_out, w_attn_out, w_out, ln1_g, ln1_b, router_group_w,
           router_group_b, router_expert_w, router_expert_b, w_gate, w_up, w_down, ln2_g, ln2_b):
    params = _prepare_params(w_in, b_in, conv_w, conv_b, conv_ln_g, conv_ln_b, w_conv_out, w_attn_out,
                             w_out, ln1_g, ln1_b, router_group_w, router_group_b, router_expert_w,
                             router_expert_b, w_gate, w_up, w_down, ln2_g, ln2_b)
    nb, s, _ = x_prompt.shape
    db, ds, _ = x_sample.shape
    depth, n_phys = cache_k.shape[:2]
    alpha = (2 * depth) ** 0.25
    k_pool = cache_k.transpose(0, 1, 3, 4, 2).reshape(depth * n_phys, KV_W, PAGE_SIZE)
    v_pool = cache_v.transpose(0, 1, 3, 4, 2).reshape(depth * n_phys, KV_W, PAGE_SIZE)
    ki_pool = cache_kidx.transpose(0, 1, 3, 2).reshape(depth * n_phys, IDX_DIM, PAGE_SIZE)
    conv_zero = jnp.zeros((nb, CONV_W - 1, C_CONV), F32)

    def layer(carry, xs):
        xp, xsm = carry
        l, st = xs
        lidx = l.reshape(1)
        prompt_attend = functools.partial(_dsa_prompt, nb=nb, s=s)
        xp, *new_p = _decoder_layer(lidx, xp, conv_zero, prompt_attend, params, alpha)
        sample_attend = functools.partial(_dsa_sample, k_pool=k_pool, v_pool=v_pool, ki_pool=ki_pool,
                                          page_table=page_table + l * n_phys, nb=db, s=ds)
        xsm, *new_s = _decoder_layer(lidx, xsm, st, sample_attend, params, alpha)
        return (xp, xsm), (tuple(new_p), tuple(new_s))

    (xp, xsm), (new_p, new_s) = lax.scan(layer, (x_prompt, x_sample),
                                         (jnp.arange(depth, dtype=I32), state_conv))
    return (xp, xsm, *new_p, *new_s)
```

```python
import functools

import jax
import jax.numpy as jnp
from jax import lax
from jax.experimental import pallas as pl
from jax.experimental.pallas import tpu as pltpu

F32 = jnp.float32
BF16 = jnp.bfloat16
I32 = jnp.int32

D_MODEL = 1024
PAGE_SIZE = 128
C_CONV = 512
CONV_W = 31
N_HEADS = 8
N_KV_HEADS = 4
HEAD_DIM = 64
Q_PER_KV = N_HEADS // N_KV_HEADS
ATTN_W = N_HEADS * HEAD_DIM
KV_W = N_KV_HEADS * HEAD_DIM
N_IDX_HEADS = 8
IDX_DIM = 64
TOPK_MAX = 256
N_GROUPS = 4
EXP_PER_GROUP = 8
N_EXPERTS = N_GROUPS * EXP_PER_GROUP
D_EXPERT = 256
LN_EPS = 1e-5

LANES = 128
SUBLANES_BF16 = 16
VMEM_LIMIT = 56 * 1024 * 1024

_N_SMALL = IDX_DIM + N_IDX_HEADS
_COL_CA = 0
_COL_CB = _COL_CA + C_CONV
_COL_Q = _COL_CB + C_CONV
_COL_K = _COL_Q + ATTN_W
_COL_V = _COL_K + KV_W
_COL_QI = _COL_V + KV_W
_COL_KI = _COL_QI + N_IDX_HEADS * IDX_DIM
_COL_G = _COL_KI + LANES
_N_IN_PAD = _COL_G + 2 * D_MODEL
_N_IN_HEAD = _COL_KI + _N_SMALL

_NT = (((1,), (1,)), ((), ()))
_NEG = -1e30
_LOG2E = 1.4426950408889634


def _params(*sem):
    return pltpu.CompilerParams(dimension_semantics=sem, vmem_limit_bytes=VMEM_LIMIT)


def _lspec(*shape):
    nd = len(shape)
    return pl.BlockSpec((None,) + shape, lambda *a: (a[-1][0],) + (0,) * nd)


def _layer_norm(x, g, b):
    mu = jnp.mean(x, axis=-1, keepdims=True)
    xc = x - mu
    var = jnp.mean(xc * xc, axis=-1, keepdims=True)
    return xc * lax.rsqrt(var + LN_EPS) * g + b


def _inproj_kernel(l_ref, x_ref, w_ref, b_ref, u_ref, q_ref, k_ref, v_ref, qi_ref, ki_ref, wi_ref, g_ref,
                   kb_ref, vb_ref, kib_ref):
    xb = x_ref[...].astype(BF16)

    def proj(c0, n):
        return jnp.dot(xb, w_ref[:, c0:c0 + n], preferred_element_type=F32) + b_ref[:, c0:c0 + n]

    u_ref[...] = proj(_COL_CA, C_CONV) * jax.nn.sigmoid(proj(_COL_CB, C_CONV))
    q_ref[...] = (proj(_COL_Q, ATTN_W) * (HEAD_DIM ** -0.5 * _LOG2E)).astype(BF16)
    k = proj(_COL_K, KV_W)
    k_ref[...] = k
    kb_ref[...] = k.astype(BF16)
    v = proj(_COL_V, KV_W)
    v_ref[...] = v
    vb_ref[...] = v.astype(BF16)
    qi_ref[...] = (proj(_COL_QI, N_IDX_HEADS * IDX_DIM) * (IDX_DIM ** -0.5)).astype(BF16)
    small = proj(_COL_KI, LANES)
    ki = small[:, :IDX_DIM]
    ki_ref[...] = ki
    kib_ref[...] = ki.astype(BF16)
    wi_ref[...] = small[:, IDX_DIM:_N_SMALL] * (N_IDX_HEADS ** -0.5)
    g_ref[...] = jax.nn.sigmoid(proj(_COL_G, 2 * D_MODEL))


def _inproj(lidx, x2d, w_pad, b_pad, tm):
    t = x2d.shape[0]
    widths = [(C_CONV, F32), (ATTN_W, BF16), (KV_W, F32), (KV_W, F32), (N_IDX_HEADS * IDX_DIM, BF16),
              (IDX_DIM, F32), (N_IDX_HEADS, F32), (2 * D_MODEL, F32), (KV_W, BF16), (KV_W, BF16),
              (IDX_DIM, BF16)]
    return pl.pallas_call(
        _inproj_kernel,
        out_shape=[jax.ShapeDtypeStruct((t, n), dt) for n, dt in widths],
        grid_spec=pltpu.PrefetchScalarGridSpec(
            num_scalar_prefetch=1, grid=(t // tm,),
            in_specs=[pl.BlockSpec((tm, D_MODEL), lambda i, l: (i, 0)),
                      _lspec(D_MODEL, _N_IN_PAD), _lspec(1, _N_IN_PAD)],
            out_specs=[pl.BlockSpec((tm, n), lambda i, l: (i, 0)) for n, _ in widths]),
        compiler_params=_params("parallel"),
        name="inproj",
    )(lidx, x2d, w_pad, b_pad)


_HALO = 32
_CONV_ROWS = 32


def _conv_kernel(l_ref, first_ref, prev_ref, cur_ref, cw_ref, cb_ref, lg_ref, lb_ref, wo_ref, y_ref,
                 hist_ref, acc_ref):
    tm = cur_ref.shape[1]
    at_start = pl.program_id(1) == 0
    hist_ref[0:_HALO, :] = jnp.where(at_start, first_ref[0], prev_ref[0])
    hist_ref[_HALO:_HALO + tm, :] = cur_ref[0]
    first = _HALO - (CONV_W - 1)
    for r0 in range(0, tm, _CONV_ROWS):
        acc = jnp.zeros((_CONV_ROWS, C_CONV), F32)
        for j in range(CONV_W):
            acc = acc + cw_ref[j:j + 1, :] * hist_ref[first + r0 + j:first + r0 + j + _CONV_ROWS, :]
        acc_ref[r0:r0 + _CONV_ROWS, :] = acc + cb_ref[...]
    y = _layer_norm(acc_ref[...], lg_ref[...], lb_ref[...])
    y = y * jax.nn.sigmoid(y)
    y_ref[0] = jnp.dot(y.astype(BF16), wo_ref[...], preferred_element_type=F32)


def _conv_branch(lidx, history, u, conv_w, conv_b, ln_g, ln_b, w_out_bf, tm):
    nb, length, _ = u.shape
    halo_blocks = tm // _HALO
    return pl.pallas_call(
        _conv_kernel,
        out_shape=jax.ShapeDtypeStruct((nb, length, D_MODEL), F32),
        grid_spec=pltpu.PrefetchScalarGridSpec(
            num_scalar_prefetch=1, grid=(nb, length // tm),
            in_specs=[pl.BlockSpec((1, _HALO, C_CONV), lambda b, i, l: (b, 0, 0)),
                      pl.BlockSpec((1, _HALO, C_CONV), lambda b, i, l: (b, jnp.maximum(i * halo_blocks - 1, 0), 0)),
                      pl.BlockSpec((1, tm, C_CONV), lambda b, i, l: (b, i, 0)),
                      _lspec(CONV_W, C_CONV), _lspec(1, C_CONV), _lspec(1, C_CONV), _lspec(1, C_CONV),
                      _lspec(C_CONV, D_MODEL)],
            out_specs=pl.BlockSpec((1, tm, D_MODEL), lambda b, i, l: (b, i, 0)),
            scratch_shapes=[pltpu.VMEM((_HALO + tm, C_CONV), F32), pltpu.VMEM((tm, C_CONV), F32)]),
        compiler_params=_params("parallel", "parallel"),
        name="conv_branch",
    )(lidx, history, u, u, conv_w, conv_b, ln_g, ln_b, w_out_bf)


_BISECT_CAP = 40
_STEPS_PER_CHECK = 4
_IDX_STEPS = 14


_SUBLANES = 8
_FOLD_ROWS = 64


def _reduce_rows(x, op):
    rows, cols = x.shape
    if rows > _FOLD_ROWS:
        x = op(x.reshape(rows // _FOLD_ROWS, _FOLD_ROWS, cols), axis=0)
    x = op(x.reshape(x.shape[0] // _SUBLANES, _SUBLANES, cols), axis=0)
    return op(x, axis=0, keepdims=True)


def _fold_keys(sc_ref, n_chunks, tk, init, fn, op, merge):
    def body(c, acc):
        off = pl.multiple_of(c * tk, tk)
        v = fn(sc_ref[pl.ds(off, tk), :], off).reshape(tk // _FOLD_ROWS, _FOLD_ROWS, LANES)
        return merge(acc, op(v, axis=0))

    acc = lax.fori_loop(0, n_chunks, body, jnp.full((_FOLD_ROWS, LANES), init, F32))
    return _reduce_rows(acc, op)


def _count_keys(sc_ref, n_chunks, tk, fn):
    return _fold_keys(sc_ref, n_chunks, tk, 0.0, fn, jnp.sum, jnp.add)


def _count_ge(sc_ref, n_chunks, tk, thr):
    return _count_keys(sc_ref, n_chunks, tk, lambda blk, off: jnp.where(blk >= thr, 1.0, 0.0))


def _any(flag):
    return jnp.max(jnp.where(flag, 1, 0))


def _select_threshold(sc_ref, st_ref, n_chunks, tk, kt, s_min, s_max, n_valid):
    lo_ref, hi_ref, cl_ref, done_ref, jlo_ref, jhi_ref = (st_ref.at[n:n + 1] for n in range(6))
    c_max = _count_ge(sc_ref, n_chunks, tk, s_max)
    top = c_max >= kt
    lo_ref[...] = jnp.where(top, s_max, s_min)
    hi_ref[...] = s_max
    cl0 = jnp.where(top, c_max, n_valid)
    cl_ref[...] = cl0
    done0 = jnp.where(top, 1.0, jnp.where(cl0 == kt, 1.0, 0.0))
    done_ref[...] = done0

    def bisect_step():
        lo, hi, cl, done = lo_ref[...], hi_ref[...], cl_ref[...], done_ref[...]
        mid = 0.5 * lo + 0.5 * hi
        c = _count_ge(sc_ref, n_chunks, tk, mid)
        act = done == 0.0
        up = c >= kt
        lo_ref[...] = jnp.where(act, jnp.where(up, mid, lo), lo)
        cl_ref[...] = jnp.where(act, jnp.where(up, c, cl), cl)
        hi_ref[...] = jnp.where(act, jnp.where(up, hi, mid), hi)
        done_ref[...] = jnp.where(act, jnp.where(c == kt, 1.0, 0.0), done)

    def bisect(carry):
        for _ in range(_STEPS_PER_CHECK):
            bisect_step()
        return carry[0] + 1, _any(done_ref[...] == 0.0)

    lax.while_loop(lambda c: (c[0] < _BISECT_CAP // _STEPS_PER_CHECK) & (c[1] > 0), bisect,
                   (jnp.int32(0), _any(done0 == 0.0)))

    def snap(_):
        lo, hi, cl, done = lo_ref[...], hi_ref[...], cl_ref[...], done_ref[...]
        t1 = _fold_keys(sc_ref, n_chunks, tk, -jnp.inf, lambda blk, off: jnp.where(blk < hi, blk, -jnp.inf),
                        jnp.max, jnp.maximum)
        c1 = _count_ge(sc_ref, n_chunks, tk, t1)
        act = done == 0.0
        found = c1 >= kt
        lo_ref[...] = jnp.where(act, jnp.where(found, t1, lo), lo)
        cl_ref[...] = jnp.where(act, jnp.where(found, c1, cl), cl)
        hi_ref[...] = jnp.where(act, jnp.where(found, hi, t1), hi)
        done_new = jnp.where(act, jnp.where(found, 1.0, 0.0), done)
        done_ref[...] = done_new
        return _any(done_new == 0.0)

    lax.while_loop(lambda f: f > 0, snap, _any(done_ref[...] == 0.0))

    thr = lo_ref[...]

    @pl.when(_any(cl_ref[...] != kt) > 0)
    def _cut_ties():
        need = kt - _count_keys(sc_ref, n_chunks, tk, lambda blk, off: jnp.where(blk > thr, 1.0, 0.0))
        jlo_ref[...] = jnp.full_like(thr, -1.0)
        jhi_ref[...] = jnp.full_like(thr, float(sc_ref.shape[0] - 1))
        row = lax.broadcasted_iota(I32, (tk, 1), 0)

        def step(_, carry):
            jlo, jhi = jlo_ref[...], jhi_ref[...]
            mid = jnp.floor(0.5 * (jlo + jhi))
            c = _count_keys(
                sc_ref, n_chunks, tk,
                lambda blk, off: jnp.where(blk == thr, jnp.where((off + row).astype(F32) <= mid, 1.0, 0.0), 0.0))
            ok = c >= need
            jhi_ref[...] = jnp.where(ok, mid, jhi)
            jlo_ref[...] = jnp.where(ok, jlo, mid)
            return carry

        lax.fori_loop(0, _IDX_STEPS, step, 0)
        cut = jhi_ref[...]

        def demote(c, carry):
            off = pl.multiple_of(c * tk, tk)
            blk = sc_ref[pl.ds(off, tk), :]
            beyond = (off + row).astype(F32) > cut
            sc_ref[pl.ds(off, tk), :] = jnp.where(blk == thr, jnp.where(beyond, -jnp.inf, blk), blk)
            return carry

        lax.fori_loop(0, n_chunks, demote, 0)

    return thr


def _min_max_init():
    return (jnp.full((_FOLD_ROWS, LANES), jnp.inf, F32), jnp.full((_FOLD_ROWS, LANES), -jnp.inf, F32))


def _min_max_update(mn, mx, score, valid):
    shape = (score.shape[0] // _FOLD_ROWS, _FOLD_ROWS, LANES)
    mn = jnp.minimum(mn, jnp.min(jnp.where(valid, score, jnp.inf).reshape(shape), axis=0))
    mx = jnp.maximum(mx, jnp.max(jnp.where(valid, score, -jnp.inf).reshape(shape), axis=0))
    return mn, mx


_TQ = 128
_TK = 512


_IDX_PAIRS = N_IDX_HEADS // 2
_ONES_ROWS = SUBLANES_BF16


def _dsa_prompt_kernel(qi_ref, wi_ref, ki_ref, q_ref, k_ref, vt_ref, o_ref,
                       sc_ref, st_ref, m_ref, acc_ref, s_ref, cm_ref):
    tq, tk = _TQ, _TK
    i = pl.program_id(1)
    n_chunks = ((i + 1) * tq + tk - 1) // tk
    qpos = i * tq + lax.broadcasted_iota(I32, (1, tq), 1)
    krow = lax.broadcasted_iota(I32, (tk, 1), 0)

    def scores(c, carry):
        off = pl.multiple_of(c * tk, tk)
        kc = ki_ref[0, pl.ds(off, tk), :]
        score = None
        for p in range(_IDX_PAIRS):
            d = lax.dot_general(kc, qi_ref[0, 0, p], _NT, preferred_element_type=F32)
            for j in range(2):
                h = 2 * p + j
                t = wi_ref[0, h:h + 1, :] * jnp.maximum(d[:, j * tq:(j + 1) * tq], 0.0)
                score = t if score is None else score + t
        score = jnp.where(score == 0.0, 0.0, score)
        valid = off + krow <= qpos
        sc_ref[pl.ds(off, tk), :] = jnp.where(valid, score, -jnp.inf)
        return _min_max_update(*carry, score, valid)

    mn, mx = lax.fori_loop(0, n_chunks, scores, _min_max_init())
    n_valid = (qpos + 1).astype(F32)
    kt = jnp.minimum(n_valid, float(TOPK_MAX))
    thr = _select_threshold(sc_ref, st_ref, n_chunks, tk, kt, _reduce_rows(mn, jnp.min),
                            _reduce_rows(mx, jnp.max), n_valid)

    m_ref[...] = jnp.full(m_ref.shape, _NEG, F32)
    acc_ref[...] = jnp.zeros(acc_ref.shape, F32)

    def qk_stage(c, slot):
        off = pl.multiple_of(c * tk, tk)
        bias = jnp.where(sc_ref[pl.ds(off, tk), :] >= thr, 0.0, -jnp.inf)
        bias = jnp.concatenate([bias] * Q_PER_KV, axis=1)
        for g in range(N_KV_HEADS):
            s = lax.dot_general(k_ref[0, g, pl.ds(off, tk), :], q_ref[0, 0, g], _NT,
                                preferred_element_type=F32) + bias
            s_ref[slot, g] = s
            cm_ref[slot, g] = _reduce_rows(s, jnp.max)

    def pv_stage(c, slot):
        off = pl.multiple_of(c * tk, tk)
        for g in range(N_KV_HEADS):
            m_old = m_ref[g]
            m_new = jnp.maximum(m_old, cm_ref[slot, g])
            alpha = jnp.exp2(m_old - m_new)
            p = jnp.exp2(s_ref[slot, g] - m_new)
            acc_ref[g] = alpha * acc_ref[g] + jnp.dot(vt_ref[0, g, :, pl.ds(off, tk)], p.astype(BF16),
                                                      preferred_element_type=F32)
            m_ref[g] = m_new

    def attend(c, carry):
        slot = c % 2
        qk_stage(c + 1, 1 - slot)
        pv_stage(c, slot)
        return carry

    qk_stage(0, 0)
    lax.fori_loop(0, n_chunks - 1, attend, 0)
    pv_stage(n_chunks - 1, (n_chunks - 1) % 2)
    for g in range(N_KV_HEADS):
        o_ref[0, 0, g] = acc_ref[g, 0:HEAD_DIM, :] / acc_ref[g, HEAD_DIM:HEAD_DIM + 1, :]


def _dsa_prompt(q_b, k_b, v_b, qi_b, ki_b, wi, nb, s):
    tq = _TQ
    nq = s // tq
    cols = Q_PER_KV * tq

    def pair_major(a, n_pairs):
        a = a.reshape(nb, nq, tq, n_pairs, 2, a.shape[-1] // (2 * n_pairs)).transpose(0, 1, 3, 4, 2, 5)
        return a.reshape(nb, nq, n_pairs, 2 * tq, -1)

    kh = k_b.reshape(nb, s, N_KV_HEADS, HEAD_DIM).transpose(0, 2, 1, 3)
    vt = v_b.reshape(nb, s, N_KV_HEADS, HEAD_DIM).transpose(0, 2, 3, 1)
    vt = jnp.concatenate([vt, jnp.ones((nb, N_KV_HEADS, _ONES_ROWS, s), vt.dtype)], axis=2)
    wit =wi.reshape(nb, s, N_IDX_HEADS).transpose(0, 2, 1)
    once = dict(pipeline_mode=pl.Buffered(1))
    o = pl.pallas_call(
        _dsa_prompt_kernel,
        out_shape=jax.ShapeDtypeStruct((nb, nq, N_KV_HEADS, HEAD_DIM, cols), F32),
        grid=(nb, nq),
        in_specs=[pl.BlockSpec((1, 1, _IDX_PAIRS, 2 * tq, IDX_DIM), lambda b, i: (b, i, 0, 0, 0)),
                  pl.BlockSpec((1, N_IDX_HEADS, tq), lambda b, i: (b, 0, i)),
                  pl.BlockSpec((1, s, IDX_DIM), lambda b, i: (b, 0, 0), **once),
                  pl.BlockSpec((1, 1, N_KV_HEADS, cols, HEAD_DIM), lambda b, i: (b, i, 0, 0, 0)),
                  pl.BlockSpec((1, N_KV_HEADS, s, HEAD_DIM), lambda b, i: (b, 0, 0, 0), **once),
                  pl.BlockSpec((1, N_KV_HEADS, HEAD_DIM + _ONES_ROWS, s), lambda b, i: (b, 0, 0, 0), **once)],
        out_specs=pl.BlockSpec((1, 1, N_KV_HEADS, HEAD_DIM, cols), lambda b, i: (b, i, 0, 0, 0)),
        scratch_shapes=[pltpu.VMEM((s, tq), F32), pltpu.VMEM((_SUBLANES, tq), F32),
                        pltpu.VMEM((N_KV_HEADS, 1, cols), F32),
                        pltpu.VMEM((N_KV_HEADS, HEAD_DIM + _ONES_ROWS, cols), F32),
                        pltpu.VMEM((2, N_KV_HEADS, _TK, cols), F32), pltpu.VMEM((2, N_KV_HEADS, 1, cols), F32)],
        compiler_params=_params("parallel", "arbitrary"),
        name="dsa_prompt",
    )(pair_major(qi_b, _IDX_PAIRS), wit, ki_b.reshape(nb, s, IDX_DIM), pair_major(q_b, N_KV_HEADS), kh, vt)
    o = o.reshape(nb, nq, N_KV_HEADS, HEAD_DIM, Q_PER_KV, tq).transpose(0, 1, 5, 2, 4, 3)
    return o.reshape(nb * s, ATTN_W)


_QPAD = 8
_PAGES_SC = 16
_PAGES_KV = 8


def _sample_scores_kernel(pt_ref, qi_ref, wi_ref, knt_ref, *rest):
    pages, sc_ref = rest[:_PAGES_SC], rest[_PAGES_SC]
    j = pl.program_id(1)
    last = pl.num_programs(1) - 1

    def scores(kt):
        d = jnp.dot(qi_ref[0], kt, preferred_element_type=F32)
        score = None
        for h in range(N_IDX_HEADS):
            t = wi_ref[0, :, h:h + 1] * jnp.maximum(d[h * _QPAD:(h + 1) * _QPAD, :], 0.0)
            score = t if score is None else score + t
        return jnp.where(score == 0.0, 0.0, score)

    @pl.when(j < last)
    def _past():
        sc_ref[0] = scores(jnp.concatenate([pg[0] for pg in pages], axis=1).astype(BF16))

    @pl.when(j == last)
    def _new():
        sc_ref[0] = jnp.zeros(sc_ref.shape[1:], F32)
        sc_ref[0, :, 0:PAGE_SIZE] = scores(knt_ref[0])


def _sample_select_kernel(sc_in_ref, bias_ref, sc_ref, st_ref, *, past, dec_seq):
    tk = _TK
    n_chunks = sc_ref.shape[0] // tk
    q = lax.broadcasted_iota(I32, (1, LANES), 1) % dec_seq
    krow = lax.broadcasted_iota(I32, (tk, 1), 0)

    def load(c, carry):
        off = pl.multiple_of(c * tk, tk)
        s = sc_in_ref[pl.ds(off, tk), :]
        valid = off + krow <= past + q
        sc_ref[pl.ds(off, tk), :] = jnp.where(valid, s, -jnp.inf)
        return _min_max_update(*carry, s, valid)

    mn, mx = lax.fori_loop(0, n_chunks, load, _min_max_init())
    n_valid = (past + 1 + q).astype(F32)
    kt = jnp.minimum(n_valid, float(min(TOPK_MAX, (past + dec_seq) // 4)))
    thr = _select_threshold(sc_ref, st_ref, n_chunks, tk, kt, _reduce_rows(mn, jnp.min),
                            _reduce_rows(mx, jnp.max), n_valid)

    def emit(c, carry):
        off = pl.multiple_of(c * tk, tk)
        bias_ref[pl.ds(off, tk), :] = jnp.where(sc_ref[pl.ds(off, tk), :] >= thr, 0.0, -jnp.inf)
        return carry

    lax.fori_loop(0, n_chunks, emit, 0)


def _sample_attend_kernel(pt_ref, q_ref, bias_ref, knt_ref, vnt_ref, *rest):
    kpages, vpages = rest[:_PAGES_KV], rest[_PAGES_KV:2 * _PAGES_KV]
    o_ref, m_ref, l_ref, acc_ref = rest[2 * _PAGES_KV:]
    j = pl.program_id(1)
    last = pl.num_programs(1) - 1

    @pl.when(j == 0)
    def _init():
        m_ref[...] = jnp.full(m_ref.shape, _NEG, F32)
        l_ref[...] = jnp.zeros(l_ref.shape, F32)
        acc_ref[...] = jnp.zeros(acc_ref.shape, F32)

    def attend(kt, vt, bias):
        s = jnp.dot(q_ref[0], kt, preferred_element_type=F32)
        s = jnp.concatenate([s[h * _QPAD:(h + 1) * _QPAD] + bias for h in range(N_HEADS)], axis=0)
        m_old = m_ref[...]
        m_new = jnp.maximum(m_old, jnp.max(s, axis=1, keepdims=True))
        alpha = jnp.exp2(m_old - m_new)
        p = jnp.exp2(s - m_new)
        l_ref[...] = alpha * l_ref[...] + jnp.sum(p, axis=1, keepdims=True)
        acc_ref[...] = alpha * acc_ref[...] + lax.dot_general(p.astype(BF16), vt, _NT,
                                                              preferred_element_type=F32)
        m_ref[...] = m_new

    @pl.when(j < last)
    def _past():
        attend(jnp.concatenate([pg[0] for pg in kpages], axis=1).astype(BF16),
               jnp.concatenate([pg[0] for pg in vpages], axis=1).astype(BF16), bias_ref[0])

    @pl.when(j == last)
    def _new():
        attend(knt_ref[0], vnt_ref[0], bias_ref[0, :, 0:PAGE_SIZE])
        o_ref[0] = acc_ref[...] / l_ref[...]


def _dsa_sample(q_b, k_b, v_b, qi_b, ki_b, wi, k_pool, v_pool, ki_pool, page_table, nb, s):
    n_pages = page_table.shape[1]
    past = n_pages * PAGE_SIZE
    assert _QPAD % s == 0 and (nb * _QPAD) % LANES == 0
    dup = jnp.arange(_QPAD) % s
    qi = qi_b.reshape(nb, s, N_IDX_HEADS, IDX_DIM)[:, dup].transpose(0, 2, 1, 3)
    qi = qi.reshape(nb, N_IDX_HEADS * _QPAD, IDX_DIM)
    wi8 = wi.reshape(nb, s, N_IDX_HEADS)[:, dup]
    new_page = lambda a: jnp.pad(a.reshape(nb, s, -1).transpose(0, 2, 1), ((0, 0), (0, 0), (0, PAGE_SIZE - s)))
    kint, knt, vnt = new_page(ki_b), new_page(k_b), new_page(v_b)

    n_sc = n_pages // _PAGES_SC
    sc_w = _PAGES_SC * PAGE_SIZE

    def page_spec(width, per_step, r):
        return pl.BlockSpec((1, width, PAGE_SIZE),
                            lambda b, j, pt: (pt[b, jnp.minimum(j * per_step + r, n_pages - 1)], 0, 0))

    scores = pl.pallas_call(
        _sample_scores_kernel,
        out_shape=jax.ShapeDtypeStruct((nb, _QPAD, (n_sc + 1) * sc_w), F32),
        grid_spec=pltpu.PrefetchScalarGridSpec(
            num_scalar_prefetch=1, grid=(nb, n_sc + 1),
            in_specs=[pl.BlockSpec((1, N_IDX_HEADS * _QPAD, IDX_DIM), lambda b, j, pt: (b, 0, 0)),
                      pl.BlockSpec((1, _QPAD, N_IDX_HEADS), lambda b, j, pt: (b, 0, 0)),
                      pl.BlockSpec((1, IDX_DIM, PAGE_SIZE), lambda b, j, pt: (b, 0, 0))]
                     + [page_spec(IDX_DIM, _PAGES_SC, r) for r in range(_PAGES_SC)],
            out_specs=pl.BlockSpec((1, _QPAD, sc_w), lambda b, j, pt: (b, 0, j))),
        compiler_params=_params("parallel", "arbitrary"),
        name="sample_scores",
    )(page_table, qi, wi8, kint, *([ki_pool] * _PAGES_SC))

    width = scores.shape[2]
    cols = nb * _QPAD
    bias = pl.pallas_call(
        functools.partial(_sample_select_kernel, past=past, dec_seq=s),
        out_shape=jax.ShapeDtypeStruct((width, cols), F32),
        grid=(cols // LANES,),
        in_specs=[pl.BlockSpec((width, LANES), lambda i: (0, i))],
        out_specs=pl.BlockSpec((width, LANES), lambda i: (0, i)),
        scratch_shapes=[pltpu.VMEM((width, LANES), F32), pltpu.VMEM((_SUBLANES, LANES), F32)],
        compiler_params=_params("parallel"),
        name="sample_select",
    )(scores.reshape(cols, width).T).T.reshape(nb, _QPAD, width)

    q4 = q_b.reshape(nb, s, N_KV_HEADS, Q_PER_KV, HEAD_DIM)[:, dup]
    eye = jnp.eye(N_KV_HEADS, dtype=q_b.dtype)
    qx = jnp.einsum("bqgjd,gk->bgjqkd", q4, eye).reshape(nb, N_HEADS * _QPAD, KV_W)
    n_kv = n_pages // _PAGES_KV
    kv_w = _PAGES_KV * PAGE_SIZE
    new_blk = past // kv_w
    hq = N_HEADS * _QPAD
    o = pl.pallas_call(
        _sample_attend_kernel,
        out_shape=jax.ShapeDtypeStruct((nb, hq, KV_W), F32),
        grid_spec=pltpu.PrefetchScalarGridSpec(
            num_scalar_prefetch=1, grid=(nb, n_kv + 1),
            in_specs=[pl.BlockSpec((1, hq, KV_W), lambda b, j, pt: (b, 0, 0)),
                      pl.BlockSpec((1, _QPAD, kv_w), lambda b, j, pt: (b, 0, jnp.minimum(j, new_blk))),
                      pl.BlockSpec((1, KV_W, PAGE_SIZE), lambda b, j, pt: (b, 0, 0)),
                      pl.BlockSpec((1, KV_W, PAGE_SIZE), lambda b, j, pt: (b, 0, 0))]
                     + [page_spec(KV_W, _PAGES_KV, r) for r in range(_PAGES_KV)] * 2,
            out_specs=pl.BlockSpec((1, hq, KV_W), lambda b, j, pt: (b, 0, 0)),
            scratch_shapes=[pltpu.VMEM((hq, 1), F32), pltpu.VMEM((hq, 1), F32), pltpu.VMEM((hq, KV_W), F32)]),
        compiler_params=_params("parallel", "arbitrary"),
        name="sample_attend",
    )(page_table, qx, bias, knt, vnt, *([k_pool] * _PAGES_KV), *([v_pool] * _PAGES_KV))
    o = o.reshape(nb, N_KV_HEADS, Q_PER_KV, _QPAD, N_KV_HEADS, HEAD_DIM)[:, :, :, :s]
    o = jnp.einsum("bgjqkd,gk->bqgjd", o, jnp.eye(N_KV_HEADS, dtype=o.dtype))
    return o.reshape(nb * s, ATTN_W)


_ROUTE_W = 8


def _merge_kernel(l_ref, x_ref, yc_ref, o_ref, g_ref, wao_ref, wout_ref, lg_ref, lb_ref, rwh_ref, rwl_ref,
                  rb_ref, x1_ref, x1b_ref, route_ref, *, alpha):
    y_attn = jnp.dot(o_ref[...].astype(BF16), wao_ref[...], preferred_element_type=F32)
    mix_in = g_ref[:, :D_MODEL] * yc_ref[...] + g_ref[:, D_MODEL:] * y_attn
    mix = jnp.dot(mix_in.astype(BF16), wout_ref[...], preferred_element_type=F32)
    x1 = _layer_norm(alpha * x_ref[...] + mix, lg_ref[...], lb_ref[...])
    x1_ref[...] = x1
    hi = x1.astype(BF16)
    x1b_ref[...] = hi

    lo = (x1 - hi.astype(F32)).astype(BF16)
    logits = (jnp.dot(hi, rwh_ref[...], preferred_element_type=F32)
              + jnp.dot(hi, rwl_ref[...], preferred_element_type=F32)
              + jnp.dot(lo, rwh_ref[...], preferred_element_type=F32)) + rb_ref[...]
    lane = lax.broadcasted_iota(I32, (1, LANES), 1)

    def first_max(v):
        m = jnp.max(v, axis=1, keepdims=True)
        return m, jnp.min(jnp.where(v == m, lane, LANES), axis=1, keepdims=True)

    is_group = (lane >= N_EXPERTS) & (lane < N_EXPERTS + N_GROUPS)
    gl = jnp.where(is_group, logits, -jnp.inf)
    gm, gidx = first_max(gl)
    p_grp = 1.0 / jnp.sum(jnp.exp(gl - gm), axis=1, keepdims=True)
    grp = gidx - N_EXPERTS
    group_of_lane = lane >> (EXP_PER_GROUP.bit_length() - 1)
    el = jnp.where(group_of_lane == grp, logits, -jnp.inf)
    e1, i1 = first_max(el)
    e2, i2 = first_max(jnp.where(lane == i1, -jnp.inf, el))
    t = jnp.exp(e2 - e1)
    g1 = p_grp / (1.0 + t)
    g2 = g1 * t
    rl = lax.broadcasted_iota(I32, (1, _ROUTE_W), 1)
    route_ref[...] = jnp.where(rl == 0, i1.astype(F32), jnp.where(rl == 1, i2.astype(F32),
                               jnp.where(rl == 2, g1, jnp.where(rl == 3, g2, 0.0))))


def _merge(lidx, x2d, yc, o, g, p, tm, alpha):
    t = x2d.shape[0]
    row = lambda n: pl.BlockSpec((tm, n), lambda i, l: (i, 0))
    return pl.pallas_call(
        functools.partial(_merge_kernel, alpha=alpha),
        out_shape=[jax.ShapeDtypeStruct((t, D_MODEL), F32), jax.ShapeDtypeStruct((t, D_MODEL), BF16),
                   jax.ShapeDtypeStruct((t, _ROUTE_W), F32)],
        grid_spec=pltpu.PrefetchScalarGridSpec(
            num_scalar_prefetch=1, grid=(t // tm,),
            in_specs=[row(D_MODEL), row(D_MODEL), row(ATTN_W), row(2 * D_MODEL),
                      _lspec(ATTN_W, D_MODEL), _lspec(D_MODEL, D_MODEL), _lspec(1, D_MODEL),
                      _lspec(1, D_MODEL), _lspec(D_MODEL, LANES), _lspec(D_MODEL, LANES), _lspec(1, LANES)],
            out_specs=[row(D_MODEL), row(D_MODEL), row(_ROUTE_W)]),
        compiler_params=_params("parallel"),
        name="merge_ln1_router",
    )(lidx, x2d, yc, o, g, p["w_attn_out"], p["w_out"], p["ln1_g"], p["ln1_b"], p["router_hi"],
      p["router_lo"], p["router_b"])


_SLOT_ALIGN = SUBLANES_BF16
_ROW_BLK = 64
_EXPERTS_PER_STEP = 4
_MOE_CHUNK = 256
_META_W = LANES
_MOE_TILE_MAX = 256 * _SLOT_ALIGN


def _moe_plan_kernel(route_ref, dcol_ref, drow_ref, meta_ref):
    tm = route_ref.shape[0]
    blk_rows = min(_MOE_CHUNK, tm)
    lane = lax.broadcasted_iota(I32, (1, LANES), 1).astype(F32)
    hit0 = lane == route_ref[:, 0:1]
    hit1 = lane == route_ref[:, 1:2]
    onehot = jnp.where(hit0, 1.0, jnp.where(hit1, 1.0, 0.0))
    r = lax.broadcasted_iota(I32, (blk_rows, blk_rows), 0)
    c = lax.broadcasted_iota(I32, (blk_rows, blk_rows), 1)
    tri = jnp.where(c < r, 1.0, 0.0).astype(BF16)
    carry = jnp.zeros((1, LANES), F32)
    prefix = []
    for b0 in range(0, tm, blk_rows):
        blk = onehot[b0:b0 + blk_rows]
        prefix.append(jnp.dot(tri, blk.astype(BF16), preferred_element_type=F32) + carry)
        carry = carry + jnp.sum(blk, axis=0, keepdims=True)
    prefix = jnp.concatenate(prefix, axis=0)
    units = jnp.ceil(carry * (1.0 / _SLOT_ALIGN))
    rr = lax.broadcasted_iota(I32, (LANES, LANES), 0)
    cc = lax.broadcasted_iota(I32, (LANES, LANES), 1)
    upper = jnp.where(rr < cc, 1.0, 0.0).astype(BF16)
    units8 = jnp.broadcast_to(units, (8, LANES)).astype(BF16)
    offs = jnp.dot(units8, upper, preferred_element_type=F32)[0:1] * _SLOT_ALIGN
    slot = offs + prefix
    d0 = jnp.sum(jnp.where(hit0, slot, 0.0), axis=1, keepdims=True)
    d1 = jnp.sum(jnp.where(hit1, slot, 0.0), axis=1, keepdims=True)
    rl = lax.broadcasted_iota(I32, (1, LANES), 1)
    rec = jnp.where(rl == 0, d0, jnp.where(rl == 1, d1, jnp.where(rl == 2, route_ref[:, 2:3],
                    jnp.where(rl == 3, route_ref[:, 3:4], 0.0))))
    dcol_ref[...] = rec[:, :_ROUTE_W]
    drow_ref[...] = rec.T[:_ROUTE_W, :]
    nblk = jnp.ceil(units * (_SLOT_ALIGN / _ROW_BLK))
    shift = jnp.where(cc == rr + N_EXPERTS, 1.0, 0.0).astype(BF16)
    nblk_sh = jnp.dot(jnp.broadcast_to(nblk, (8, LANES)).astype(BF16), shift,
                      preferred_element_type=F32)[0:1]
    meta_ref[0] = jnp.where(rl < N_EXPERTS, offs, nblk_sh).astype(I32)


def _moe_plan(route, tm):
    t = route.shape[0]
    nt = t // tm
    assert tm <= _MOE_TILE_MAX
    return pl.pallas_call(
        _moe_plan_kernel,
        out_shape=[jax.ShapeDtypeStruct((t, _ROUTE_W), F32), jax.ShapeDtypeStruct((_ROUTE_W, t), F32),
                   jax.ShapeDtypeStruct((nt, 1, _META_W), I32)],
        grid=(nt,),
        in_specs=[pl.BlockSpec((tm, _ROUTE_W), lambda i: (i, 0))],
        out_specs=[pl.BlockSpec((tm, _ROUTE_W), lambda i: (i, 0)),
                   pl.BlockSpec((_ROUTE_W, tm), lambda i: (0, i)),
                   pl.BlockSpec((1, 1, _META_W), lambda i: (i, 0, 0))],
        compiler_params=_params("parallel"),
        name="moe_plan",
    )(route)


def _n_slots(tm):
    n = 2 * tm + N_EXPERTS * (_SLOT_ALIGN - 1) + _ROW_BLK
    return -(-n // _MOE_CHUNK) * _MOE_CHUNK


def _moe_kernel(meta_ref, l_ref, x1b_ref, x1_ref, drow_ref, dcol_ref, wg_ref, wu_ref, wd_ref, lg_ref, lb_ref,
                x2_ref, xb_ref, yb_ref, gs_ref, *, alpha):
    i, e = pl.program_id(0), pl.program_id(1)
    tm = x1_ref.shape[0]
    n_slots = xb_ref.shape[0]

    @pl.when(e == 0)
    def _dispatch():
        d0, d1 = drow_ref[0:1, :], drow_ref[1:2, :]
        g0, g1 = drow_ref[2:3, :], drow_ref[3:4, :]
        for s0 in range(0, n_slots, _MOE_CHUNK):
            sl = (s0 + lax.broadcasted_iota(I32, (_MOE_CHUNK, 1), 0)).astype(F32)
            a, b = sl == d0, sl == d1
            p = jnp.where(a, 1.0, jnp.where(b, 1.0, 0.0)).astype(BF16)
            xb_ref[s0:s0 + _MOE_CHUNK, :] = jnp.dot(p, x1b_ref[...], preferred_element_type=F32).astype(BF16)
            gs_ref[s0:s0 + _MOE_CHUNK, :] = jnp.sum(jnp.where(a, g0, jnp.where(b, g1, 0.0)), axis=1,
                                                    keepdims=True)
        yb_ref[...] = jnp.zeros(yb_ref.shape, BF16)

    for k in range(_EXPERTS_PER_STEP):
        expert = e * _EXPERTS_PER_STEP + k
        off = meta_ref[i * _META_W + expert]
        nblk = meta_ref[i * _META_W + N_EXPERTS + expert]

        def block(r, carry, k=k, off=off):
            start = pl.multiple_of(off + r * _ROW_BLK, _SLOT_ALIGN)
            xs = xb_ref[pl.ds(start, _ROW_BLK), :]
            hg = jnp.dot(xs, wg_ref[k], preferred_element_type=F32)
            hu = jnp.dot(xs, wu_ref[k], preferred_element_type=F32)
            h = (hg * jax.nn.sigmoid(hg)) * hu
            y = jnp.dot(h.astype(BF16), wd_ref[k], preferred_element_type=F32)
            yb_ref[pl.ds(start, _ROW_BLK), :] = (y * gs_ref[pl.ds(start, _ROW_BLK), :]).astype(BF16)
            return carry

        lax.fori_loop(0, nblk, block, 0)

    @pl.when(e == pl.num_programs(1) - 1)
    def _combine():
        lane = lax.broadcasted_iota(I32, (1, n_slots), 1).astype(F32)
        rows = min(_MOE_CHUNK, tm)
        for t0 in range(0, tm, rows):
            d0, d1 = dcol_ref[t0:t0 + rows, 0:1], dcol_ref[t0:t0 + rows, 1:2]
            qm = jnp.where(lane == d0, 1.0, jnp.where(lane == d1, 1.0, 0.0)).astype(BF16)
            y = jnp.dot(qm, yb_ref[...], preferred_element_type=F32)
            z = alpha * x1_ref[t0:t0 + rows, :] + y
            x2_ref[t0:t0 + rows, :] = _layer_norm(z, lg_ref[...], lb_ref[...])


def _moe(lidx, x1, x1b, drow, dcol, meta, p, tm, alpha):
    t = x1.shape[0]
    n_slots = _n_slots(tm)
    tile = lambda shape, imap: pl.BlockSpec(shape, imap)
    return pl.pallas_call(
        functools.partial(_moe_kernel, alpha=alpha),
        out_shape=jax.ShapeDtypeStruct((t, D_MODEL), F32),
        grid_spec=pltpu.PrefetchScalarGridSpec(
            num_scalar_prefetch=2, grid=(t // tm, N_EXPERTS // _EXPERTS_PER_STEP),
            in_specs=[tile((tm, D_MODEL), lambda i, e, m, l: (i, 0)),
                      tile((tm, D_MODEL), lambda i, e, m, l: (i, 0)),
                      tile((_ROUTE_W, tm), lambda i, e, m, l: (0, i)),
                      tile((tm, _ROUTE_W), lambda i, e, m, l: (i, 0)),
                      tile((None, _EXPERTS_PER_STEP, D_MODEL, D_EXPERT), lambda i, e, m, l: (l[0], e, 0, 0)),
                      tile((None, _EXPERTS_PER_STEP, D_MODEL, D_EXPERT), lambda i, e, m, l: (l[0], e, 0, 0)),
                      tile((None, _EXPERTS_PER_STEP, D_EXPERT, D_MODEL), lambda i, e, m, l: (l[0], e, 0, 0)),
                      _lspec(1, D_MODEL), _lspec(1, D_MODEL)],
            out_specs=tile((tm, D_MODEL), lambda i, e, m, l: (i, 0)),
            scratch_shapes=[pltpu.VMEM((n_slots, D_MODEL), BF16), pltpu.VMEM((n_slots, D_MODEL), BF16),
                            pltpu.VMEM((n_slots, 1), F32)]),
        compiler_params=_params("parallel", "arbitrary"),
        name="moe_experts_ln2",
    )(meta.reshape(-1), lidx, x1b, x1, drow, dcol, p["w_gate"], p["w_up"], p["w_down"], p["ln2_g"], p["ln2_b"])


def _token_tile(t, want):
    return want if t % want == 0 else t


def _decoder_layer(lidx, x, prefix, attend, p, alpha):
    nb, s, _ = x.shape
    t = nb * s
    x2d = x.reshape(t, D_MODEL)
    u, q_b, k, v, qi_b, ki, wi, g, k_b, v_b, ki_b = _inproj(lidx, x2d, p["w_in"], p["b_in"], _token_tile(t, 512))

    ctm = 512 if s % 512 == 0 else _HALO
    u3 = u.reshape(nb, s, C_CONV)
    history = jnp.pad(prefix, ((0, 0), (_HALO - (CONV_W - 1), 0), (0, 0)))
    u_rows = jnp.pad(u3, ((0, 0), (0, (-s) % ctm), (0, 0)))
    yc = _conv_branch(lidx, history, u_rows, p["conv_w"], p["conv_b"], p["conv_ln_g"], p["conv_ln_b"],
                      p["w_conv_out"], ctm)
    yc = yc[:, :s].reshape(t, D_MODEL)

    o = attend(q_b, k_b, v_b, qi_b, ki_b, wi)
    x1, x1b, route = _merge(lidx, x2d, yc, o, g, p, _token_tile(t, 256), alpha)
    mtm = _token_tile(t, 1024)
    dcol, drow, meta = _moe_plan(route, mtm)
    x2 = _moe(lidx, x1, x1b, drow, dcol, meta, p, mtm, alpha)

    new_conv = jnp.concatenate([prefix, u3], axis=1)[:, -(CONV_W - 1):]
    return (x2.reshape(nb, s, D_MODEL), k.reshape(nb, s, N_KV_HEADS, HEAD_DIM),
            v.reshape(nb, s, N_KV_HEADS, HEAD_DIM), ki.reshape(nb, s, IDX_DIM), new_conv)


def _prepare_params(w_in, b_in, conv_w, conv_b, conv_ln_g, conv_ln_b, w_conv_out, w_attn_out, w_out,
                    ln1_g, ln1_b, router_group_w, router_group_b, router_expert_w, router_expert_b,
                    w_gate, w_up, w_down, ln2_g, ln2_b):
    depth = w_in.shape[0]
    pad = _COL_G - _N_IN_HEAD
    w_pad = jnp.concatenate([w_in[..., :_N_IN_HEAD], jnp.zeros((depth, D_MODEL, pad), F32),
                             w_in[..., _N_IN_HEAD:]], axis=-1).astype(BF16)
    b_pad = jnp.concatenate([b_in[..., :_N_IN_HEAD], jnp.zeros((depth, pad), F32),
                             b_in[..., _N_IN_HEAD:]], axis=-1)[:, None, :]
    rpad = LANES - N_EXPERTS - N_GROUPS
    rw = jnp.concatenate([router_expert_w, router_group_w, jnp.zeros((depth, D_MODEL, rpad), F32)], axis=-1)
    rb = jnp.concatenate([router_expert_b, router_group_b, jnp.zeros((depth, rpad), F32)], axis=-1)
    rw_hi = rw.astype(BF16)
    rw_lo = (rw - rw_hi.astype(F32)).astype(BF16)
    vec = lambda a: a[:, None, :]
    return dict(w_in=w_pad, b_in=b_pad, conv_w=conv_w, conv_b=vec(conv_b), conv_ln_g=vec(conv_ln_g),
                conv_ln_b=vec(conv_ln_b), w_conv_out=w_conv_out.astype(BF16),
                w_attn_out=w_attn_out.astype(BF16), w_out=w_out.astype(BF16), ln1_g=vec(ln1_g),
                ln1_b=vec(ln1_b), router_hi=rw_hi, router_lo=rw_lo, router_b=vec(rb),
                w_gate=w_gate.astype(BF16), w_up=w_up.astype(BF16), w_down=w_down.astype(BF16),
                ln2_g=vec(ln2_g), ln2_b=vec(ln2_b))


def kernel(x_prompt, x_sample, cache_k, cache_v, cache_kidx, state_conv, page_table, w_in, b_in, conv_w,
           conv_b, conv_ln_g, conv_ln_b, w_conv_out, w_attn_out, w_out, ln1_g, ln1_b, router_group_w,
           router_group_b, router_expert_w, router_expert_b, w_gate, w_up, w_down, ln2_g, ln2_b):
    params = _prepare_params(w_in, b_in, conv_w, conv_b, conv_ln_g, conv_ln_b, w_conv_out, w_attn_out,
                             w_out, ln1_g, ln1_b, router_group_w, router_group_b, router_expert_w,
                             router_expert_b, w_gate, w_up, w_down, ln2_g, ln2_b)
    nb, s, _ = x_prompt.shape
    db, ds, _ = x_sample.shape
    depth, n_phys = cache_k.shape[:2]
    alpha = (2 * depth) ** 0.25
    k_pool = cache_k.transpose(0, 1, 3, 4, 2).reshape(depth * n_phys, KV_W, PAGE_SIZE)
    v_pool = cache_v.transpose(0, 1, 3, 4, 2).reshape(depth * n_phys, KV_W, PAGE_SIZE)
    ki_pool = cache_kidx.transpose(0, 1, 3, 2).reshape(depth * n_phys, IDX_DIM, PAGE_SIZE)
    conv_zero = jnp.zeros((nb, CONV_W - 1, C_CONV), F32)

    def layer(carry, xs):
        xp, xsm = carry
        l, st = xs
        lidx = l.reshape(1)
        prompt_attend = functools.partial(_dsa_prompt, nb=nb, s=s)
        xp, *new_p = _decoder_layer(lidx, xp, conv_zero, prompt_attend, params, alpha)
        sample_attend = functools.partial(_dsa_sample, k_pool=k_pool, v_pool=v_pool, ki_pool=ki_pool,
                                          page_table=page_table + l * n_phys, nb=db, s=ds)
        xsm, *new_s = _decoder_layer(lidx, xsm, st, sample_attend, params, alpha)
        return (xp, xsm), (tuple(new_p), tuple(new_s))

    (xp, xsm), (new_p, new_s) = lax.scan(layer, (x_prompt, x_sample),
                                         (jnp.arange(depth, dtype=I32), state_conv))
    return (xp, xsm, *new_p, *new_s)
```

```python
import functools

import jax
import jax.numpy as jnp
from jax import lax
from jax.experimental import pallas as pl
from jax.experimental.pallas import tpu as pltpu

F32 = jnp.float32
BF16 = jnp.bfloat16
I32 = jnp.int32

D_MODEL = 1024
PAGE_SIZE = 128
C_CONV = 512
CONV_W = 31
N_HEADS = 8
N_KV_HEADS = 4
HEAD_DIM = 64
Q_PER_KV = N_HEADS // N_KV_HEADS
ATTN_W = N_HEADS * HEAD_DIM
KV_W = N_KV_HEADS * HEAD_DIM
N_IDX_HEADS = 8
IDX_DIM = 64
TOPK_MAX = 256
N_GROUPS = 4
EXP_PER_GROUP = 8
N_EXPERTS = N_GROUPS * EXP_PER_GROUP
D_EXPERT = 256
LN_EPS = 1e-5

LANES = 128
SUBLANES_BF16 = 16
VMEM_LIMIT = 56 * 1024 * 1024

_N_SMALL = IDX_DIM + N_IDX_HEADS
_COL_CA = 0
_COL_CB = _COL_CA + C_CONV
_COL_Q = _COL_CB + C_CONV
_COL_K = _COL_Q + ATTN_W
_COL_V = _COL_K + KV_W
_COL_QI = _COL_V + KV_W
_COL_KI = _COL_QI + N_IDX_HEADS * IDX_DIM
_COL_G = _COL_KI + LANES
_N_IN_PAD = _COL_G + 2 * D_MODEL
_N_IN_HEAD = _COL_KI + _N_SMALL

_NT = (((1,), (1,)), ((), ()))
_NEG = -1e30
_LOG2E = 1.4426950408889634


def _params(*sem):
    return pltpu.CompilerParams(dimension_semantics=sem, vmem_limit_bytes=VMEM_LIMIT)


def _lspec(*shape):
    nd = len(shape)
    return pl.BlockSpec((None,) + shape, lambda *a: (a[-1][0],) + (0,) * nd)


def _layer_norm(x, g, b):
    mu = jnp.mean(x, axis=-1, keepdims=True)
    xc = x - mu
    var = jnp.mean(xc * xc, axis=-1, keepdims=True)
    return xc * lax.rsqrt(var + LN_EPS) * g + b


def _inproj_kernel(l_ref, x_ref, w_ref, b_ref, u_ref, q_ref, k_ref, v_ref, qi_ref, ki_ref, wi_ref, g_ref,
                   kb_ref, vb_ref, kib_ref):
    xb = x_ref[...].astype(BF16)

    def proj(c0, n):
        return jnp.dot(xb, w_ref[:, c0:c0 + n], preferred_element_type=F32) + b_ref[:, c0:c0 + n]

    u_ref[...] = proj(_COL_CA, C_CONV) * jax.nn.sigmoid(proj(_COL_CB, C_CONV))
    q_ref[...] = (proj(_COL_Q, ATTN_W) * (HEAD_DIM ** -0.5 * _LOG2E)).astype(BF16)
    k = proj(_COL_K, KV_W)
    k_ref[...] = k
    kb_ref[...] = k.astype(BF16)
    v = proj(_COL_V, KV_W)
    v_ref[...] = v
    vb_ref[...] = v.astype(BF16)
    qi_ref[...] = (proj(_COL_QI, N_IDX_HEADS * IDX_DIM) * (IDX_DIM ** -0.5)).astype(BF16)
    small = proj(_COL_KI, LANES)
    ki = small[:, :IDX_DIM]
    ki_ref[...] = ki
    kib_ref[...] = ki.astype(BF16)
    wi_ref[...] = small[:, IDX_DIM:_N_SMALL] * (N_IDX_HEADS ** -0.5)
    g_ref[...] = jax.nn.sigmoid(proj(_COL_G, 2 * D_MODEL))


def _inproj(lidx, x2d, w_pad, b_pad, tm):
    t = x2d.shape[0]
    widths = [(C_CONV, F32), (ATTN_W, BF16), (KV_W, F32), (KV_W, F32), (N_IDX_HEADS * IDX_DIM, BF16),
              (IDX_DIM, F32), (N_IDX_HEADS, F32), (2 * D_MODEL, F32), (KV_W, BF16), (KV_W, BF16),
              (IDX_DIM, BF16)]
    return pl.pallas_call(
        _inproj_kernel,
        out_shape=[jax.ShapeDtypeStruct((t, n), dt) for n, dt in widths],
        grid_spec=pltpu.PrefetchScalarGridSpec(
            num_scalar_prefetch=1, grid=(t // tm,),
            in_specs=[pl.BlockSpec((tm, D_MODEL), lambda i, l: (i, 0)),
                      _lspec(D_MODEL, _N_IN_PAD), _lspec(1, _N_IN_PAD)],
            out_specs=[pl.BlockSpec((tm, n), lambda i, l: (i, 0)) for n, _ in widths]),
        compiler_params=_params("parallel"),
        name="inproj",
    )(lidx, x2d, w_pad, b_pad)


_HALO = 32
_CONV_ROWS = 32


def _conv_kernel(l_ref, first_ref, prev_ref, cur_ref, cw_ref, cb_ref, lg_ref, lb_ref, wo_ref, y_ref,
                 hist_ref, acc_ref):
    tm = cur_ref.shape[1]
    at_start = pl.program_id(1) == 0
    hist_ref[0:_HALO, :] = jnp.where(at_start, first_ref[0], prev_ref[0])
    hist_ref[_HALO:_HALO + tm, :] = cur_ref[0]
    first = _HALO - (CONV_W - 1)
    for r0 in range(0, tm, _CONV_ROWS):
        acc = jnp.zeros((_CONV_ROWS, C_CONV), F32)
        for j in range(CONV_W):
            acc = acc + cw_ref[j:j + 1, :] * hist_ref[first + r0 + j:first + r0 + j + _CONV_ROWS, :]
        acc_ref[r0:r0 + _CONV_ROWS, :] = acc + cb_ref[...]
    y = _layer_norm(acc_ref[...], lg_ref[...], lb_ref[...])
    y = y * jax.nn.sigmoid(y)
    y_ref[0] = jnp.dot(y.astype(BF16), wo_ref[...], preferred_element_type=F32)


def _conv_branch(lidx, history, u, conv_w, conv_b, ln_g, ln_b, w_out_bf, tm):
    nb, length, _ = u.shape
    halo_blocks = tm // _HALO
    return pl.pallas_call(
        _conv_kernel,
        out_shape=jax.ShapeDtypeStruct((nb, length, D_MODEL), F32),
        grid_spec=pltpu.PrefetchScalarGridSpec(
            num_scalar_prefetch=1, grid=(nb, length // tm),
            in_specs=[pl.BlockSpec((1, _HALO, C_CONV), lambda b, i, l: (b, 0, 0)),
                      pl.BlockSpec((1, _HALO, C_CONV), lambda b, i, l: (b, jnp.maximum(i * halo_blocks - 1, 0), 0)),
                      pl.BlockSpec((1, tm, C_CONV), lambda b, i, l: (b, i, 0)),
                      _lspec(CONV_W, C_CONV), _lspec(1, C_CONV), _lspec(1, C_CONV), _lspec(1, C_CONV),
                      _lspec(C_CONV, D_MODEL)],
            out_specs=pl.BlockSpec((1, tm, D_MODEL), lambda b, i, l: (b, i, 0)),
            scratch_shapes=[pltpu.VMEM((_HALO + tm, C_CONV), F32), pltpu.VMEM((tm, C_CONV), F32)]),
        compiler_params=_params("parallel", "parallel"),
        name="conv_branch",
    )(lidx, history, u, u, conv_w, conv_b, ln_g, ln_b, w_out_bf)


_BISECT_CAP = 40
_BLIND_STEPS = 12
_STEPS_PER_CHECK = 2
_IDX_STEPS = 14


_SUBLANES = 8
_FOLD_ROWS = 64


def _reduce_rows(x, op):
    rows, cols = x.shape
    if rows > _FOLD_ROWS:
        x = op(x.reshape(rows // _FOLD_ROWS, _FOLD_ROWS, cols), axis=0)
    x = op(x.reshape(x.shape[0] // _SUBLANES, _SUBLANES, cols), axis=0)
    return op(x, axis=0, keepdims=True)


def _fold_keys(sc_ref, n_chunks, tk, init, fn, op, merge):
    def body(c, acc):
        off = pl.multiple_of(c * tk, tk)
        v = fn(sc_ref[pl.ds(off, tk), :], off).reshape(tk // _FOLD_ROWS, _FOLD_ROWS, LANES)
        return merge(acc, op(v, axis=0))

    acc = lax.fori_loop(0, n_chunks, body, jnp.full((_FOLD_ROWS, LANES), init, F32))
    return _reduce_rows(acc, op)


def _count_keys(sc_ref, n_chunks, tk, fn):
    return _fold_keys(sc_ref, n_chunks, tk, 0.0, fn, jnp.sum, jnp.add)


def _count_ge(sc_ref, n_chunks, tk, thr):
    return _count_keys(sc_ref, n_chunks, tk, lambda blk, off: jnp.where(blk >= thr, 1.0, 0.0))


def _any(flag):
    return jnp.max(jnp.where(flag, 1, 0))


def _select_threshold(sc_ref, st_ref, n_chunks, tk, kt, s_min, s_max, n_valid):
    lo_ref, hi_ref, cl_ref, done_ref, jlo_ref, jhi_ref = (st_ref.at[n:n + 1] for n in range(6))
    c_max = _count_ge(sc_ref, n_chunks, tk, s_max)
    top = c_max >= kt
    lo_ref[...] = jnp.where(top, s_max, s_min)
    hi_ref[...] = s_max
    cl0 = jnp.where(top, c_max, n_valid)
    cl_ref[...] = cl0
    done0 = jnp.where(top, 1.0, jnp.where(cl0 == kt, 1.0, 0.0))
    done_ref[...] = done0

    def bisect_step():
        lo, hi, cl, done = lo_ref[...], hi_ref[...], cl_ref[...], done_ref[...]
        mid = 0.5 * lo + 0.5 * hi
        c = _count_ge(sc_ref, n_chunks, tk, mid)
        act = done == 0.0
        up = c >= kt
        lo_ref[...] = jnp.where(act, jnp.where(up, mid, lo), lo)
        cl_ref[...] = jnp.where(act, jnp.where(up, c, cl), cl)
        hi_ref[...] = jnp.where(act, jnp.where(up, hi, mid), hi)
        done_ref[...] = jnp.where(act, jnp.where(c == kt, 1.0, 0.0), done)

    def bisect(carry):
        for _ in range(_STEPS_PER_CHECK):
            bisect_step()
        return carry[0] + _STEPS_PER_CHECK, _any(done_ref[...] == 0.0)

    @pl.when(_any(done0 == 0.0) > 0)
    def _search():
        lax.fori_loop(0, _BLIND_STEPS, lambda _, carry: (bisect_step(), carry)[1], 0)
        lax.while_loop(lambda c: (c[0] < _BISECT_CAP) & (c[1] > 0), bisect,
                       (jnp.int32(_BLIND_STEPS), _any(done_ref[...] == 0.0)))

    def snap(_):
        lo, hi, cl, done = lo_ref[...], hi_ref[...], cl_ref[...], done_ref[...]
        t1 = _fold_keys(sc_ref, n_chunks, tk, -jnp.inf, lambda blk, off: jnp.where(blk < hi, blk, -jnp.inf),
                        jnp.max, jnp.maximum)
        c1 = _count_ge(sc_ref, n_chunks, tk, t1)
        act = done == 0.0
        found = c1 >= kt
        lo_ref[...] = jnp.where(act, jnp.where(found, t1, lo), lo)
        cl_ref[...] = jnp.where(act, jnp.where(found, c1, cl), cl)
        hi_ref[...] = jnp.where(act, jnp.where(found, hi, t1), hi)
        done_new = jnp.where(act, jnp.where(found, 1.0, 0.0), done)
        done_ref[...] = done_new
        return _any(done_new == 0.0)

    lax.while_loop(lambda f: f > 0, snap, _any(done_ref[...] == 0.0))

    thr = lo_ref[...]

    @pl.when(_any(cl_ref[...] != kt) > 0)
    def _cut_ties():
        need = kt - _count_keys(sc_ref, n_chunks, tk, lambda blk, off: jnp.where(blk > thr, 1.0, 0.0))
        jlo_ref[...] = jnp.full_like(thr, -1.0)
        jhi_ref[...] = jnp.full_like(thr, float(sc_ref.shape[0] - 1))
        row = lax.broadcasted_iota(I32, (tk, 1), 0)

        def step(_, carry):
            jlo, jhi = jlo_ref[...], jhi_ref[...]
            mid = jnp.floor(0.5 * (jlo + jhi))
            c = _count_keys(
                sc_ref, n_chunks, tk,
                lambda blk, off: jnp.where(blk == thr, jnp.where((off + row).astype(F32) <= mid, 1.0, 0.0), 0.0))
            ok = c >= need
            jhi_ref[...] = jnp.where(ok, mid, jhi)
            jlo_ref[...] = jnp.where(ok, jlo, mid)
            return carry

        lax.fori_loop(0, _IDX_STEPS, step, 0)
        cut = jhi_ref[...]

        def demote(c, carry):
            off = pl.multiple_of(c * tk, tk)
            blk = sc_ref[pl.ds(off, tk), :]
            beyond = (off + row).astype(F32) > cut
            sc_ref[pl.ds(off, tk), :] = jnp.where(blk == thr, jnp.where(beyond, -jnp.inf, blk), blk)
            return carry

        lax.fori_loop(0, n_chunks, demote, 0)

    return thr


def _min_max_init():
    return (jnp.full((_FOLD_ROWS, LANES), jnp.inf, F32), jnp.full((_FOLD_ROWS, LANES), -jnp.inf, F32))


def _min_max_update(mn, mx, score, valid):
    shape = (score.shape[0] // _FOLD_ROWS, _FOLD_ROWS, LANES)
    mn = jnp.minimum(mn, jnp.min(jnp.where(valid, score, jnp.inf).reshape(shape), axis=0))
    mx = jnp.maximum(mx, jnp.max(jnp.where(valid, score, -jnp.inf).reshape(shape), axis=0))
    return mn, mx


_TQ = 128
_TK = 512


_IDX_PAIRS = N_IDX_HEADS // 2
_ONES_ROWS = SUBLANES_BF16


def _dsa_prompt_kernel(qi_ref, wi_ref, ki_ref, q_ref, k_ref, vt_ref, o_ref,
                       sc_ref, st_ref, m_ref, acc_ref, s_ref, cm_ref, p_ref, al_ref):
    tq, tk = _TQ, _TK
    i = pl.program_id(1)
    n_chunks = ((i + 1) * tq + tk - 1) // tk
    qpos = i * tq + lax.broadcasted_iota(I32, (1, tq), 1)
    krow = lax.broadcasted_iota(I32, (tk, 1), 0)

    def scores(c, carry):
        off = pl.multiple_of(c * tk, tk)
        kc = ki_ref[0, pl.ds(off, tk), :]
        score = None
        for p in range(_IDX_PAIRS):
            d = lax.dot_general(kc, qi_ref[0, 0, p], _NT, preferred_element_type=F32)
            for j in range(2):
                h = 2 * p + j
                t = wi_ref[0, h:h + 1, :] * jnp.maximum(d[:, j * tq:(j + 1) * tq], 0.0)
                score = t if score is None else score + t
        score = jnp.where(score == 0.0, 0.0, score)
        valid = off + krow <= qpos
        sc_ref[pl.ds(off, tk), :] = jnp.where(valid, score, -jnp.inf)
        return _min_max_update(*carry, score, valid)

    mn, mx = lax.fori_loop(0, n_chunks, scores, _min_max_init())
    n_valid = (qpos + 1).astype(F32)
    kt = jnp.minimum(n_valid, float(TOPK_MAX))
    thr = _select_threshold(sc_ref, st_ref, n_chunks, tk, kt, _reduce_rows(mn, jnp.min),
                            _reduce_rows(mx, jnp.max), n_valid)

    m_ref[...] = jnp.full(m_ref.shape, _NEG, F32)
    acc_ref[...] = jnp.zeros(acc_ref.shape, F32)

    def mask_bias(c):
        off = pl.multiple_of(c * tk, tk)
        bias = jnp.where(sc_ref[pl.ds(off, tk), :] >= thr, 0.0, -jnp.inf)
        return jnp.concatenate([bias] * Q_PER_KV, axis=1)

    def qk_stage(c, slot, groups, bias=None):
        off = pl.multiple_of(c * tk, tk)
        bias = mask_bias(c) if bias is None else bias
        for g in groups:
            s = lax.dot_general(k_ref[0, g, pl.ds(off, tk), :], q_ref[0, 0, g], _NT,
                                preferred_element_type=F32) + bias
            s_ref[slot, g] = s
            cm_ref[slot, g] = _reduce_rows(s, jnp.max)

    def exp_stage(slot, groups):
        for g in groups:
            m_old = m_ref[g]
            m_new = jnp.maximum(m_old, cm_ref[slot, g])
            al_ref[slot, g] = jnp.exp2(m_old - m_new)
            p_ref[slot, g] = jnp.exp2(s_ref[slot, g] - m_new).astype(BF16)
            m_ref[g] = m_new

    def pv_stage(c, slot, groups):
        off = pl.multiple_of(c * tk, tk)
        for g in groups:
            acc_ref[g] = al_ref[slot, g] * acc_ref[g] + jnp.dot(vt_ref[0, g, :, pl.ds(off, tk)], p_ref[slot, g],
                                                                preferred_element_type=F32)

    every = range(N_KV_HEADS)

    def attend(c, carry):
        slot = c % 2
        for g in every:
            pv_stage(c, slot, (g,))
            qk_stage(c + 2, slot, (g,))
            exp_stage(1 - slot, (g,))
        return carry

    qk_stage(0, 0, every)
    exp_stage(0, every)

    @pl.when(n_chunks > 1)
    def _fill():
        qk_stage(1, 1, every)

    lax.fori_loop(0, n_chunks - 2, attend, 0)
    last = n_chunks - 1

    @pl.when(n_chunks > 1)
    def _drain():
        exp_stage(last % 2, every)
        pv_stage(last - 1, (last - 1) % 2, every)

    pv_stage(last, last % 2, every)
    for g in range(N_KV_HEADS):
        o_ref[0, 0, g] = acc_ref[g, 0:HEAD_DIM, :] / acc_ref[g, HEAD_DIM:HEAD_DIM + 1, :]


def _dsa_prompt(q_b, k_b, v_b, qi_b, ki_b, wi, nb, s):
    tq = _TQ
    nq = s // tq
    cols = Q_PER_KV * tq

    def pair_major(a, n_pairs):
        a = a.reshape(nb, nq, tq, n_pairs, 2, a.shape[-1] // (2 * n_pairs)).transpose(0, 1, 3, 4, 2, 5)
        return a.reshape(nb, nq, n_pairs, 2 * tq, -1)

    kh = k_b.reshape(nb, s, N_KV_HEADS, HEAD_DIM).transpose(0, 2, 1, 3)
    vt = v_b.reshape(nb, s, N_KV_HEADS, HEAD_DIM).transpose(0, 2, 3, 1)
    vt = jnp.concatenate([vt, jnp.ones((nb, N_KV_HEADS, _ONES_ROWS, s), vt.dtype)], axis=2)
    wit =wi.reshape(nb, s, N_IDX_HEADS).transpose(0, 2, 1)
    once = dict(pipeline_mode=pl.Buffered(1))
    o = pl.pallas_call(
        _dsa_prompt_kernel,
        out_shape=jax.ShapeDtypeStruct((nb, nq, N_KV_HEADS, HEAD_DIM, cols), F32),
        grid=(nb, nq),
        in_specs=[pl.BlockSpec((1, 1, _IDX_PAIRS, 2 * tq, IDX_DIM), lambda b, i: (b, i, 0, 0, 0)),
                  pl.BlockSpec((1, N_IDX_HEADS, tq), lambda b, i: (b, 0, i)),
                  pl.BlockSpec((1, s, IDX_DIM), lambda b, i: (b, 0, 0), **once),
                  pl.BlockSpec((1, 1, N_KV_HEADS, cols, HEAD_DIM), lambda b, i: (b, i, 0, 0, 0)),
                  pl.BlockSpec((1, N_KV_HEADS, s, HEAD_DIM), lambda b, i: (b, 0, 0, 0), **once),
                  pl.BlockSpec((1, N_KV_HEADS, HEAD_DIM + _ONES_ROWS, s), lambda b, i: (b, 0, 0, 0), **once)],
        out_specs=pl.BlockSpec((1, 1, N_KV_HEADS, HEAD_DIM, cols), lambda b, i: (b, i, 0, 0, 0)),
        scratch_shapes=[pltpu.VMEM((s, tq), F32), pltpu.VMEM((_SUBLANES, tq), F32),
                        pltpu.VMEM((N_KV_HEADS, 1, cols), F32),
                        pltpu.VMEM((N_KV_HEADS, HEAD_DIM + _ONES_ROWS, cols), F32),
                        pltpu.VMEM((2, N_KV_HEADS, _TK, cols), F32), pltpu.VMEM((2, N_KV_HEADS, 1, cols), F32),
                        pltpu.VMEM((2, N_KV_HEADS, _TK, cols), BF16), pltpu.VMEM((2, N_KV_HEADS, 1, cols), F32)],
        compiler_params=_params("parallel", "arbitrary"),
        name="dsa_prompt",
    )(pair_major(qi_b, _IDX_PAIRS), wit, ki_b.reshape(nb, s, IDX_DIM), pair_major(q_b, N_KV_HEADS), kh, vt)
    o = o.reshape(nb, nq, N_KV_HEADS, HEAD_DIM, Q_PER_KV, tq).transpose(0, 1, 5, 2, 4, 3)
    return o.reshape(nb * s, ATTN_W)


_QPAD = 8
_PAGES_SC = 16
_PAGES_KV = 8


def _sample_scores_kernel(pt_ref, qi_ref, wi_ref, knt_ref, *rest):
    pages, sc_ref = rest[:_PAGES_SC], rest[_PAGES_SC]
    j = pl.program_id(1)
    last = pl.num_programs(1) - 1

    def scores(kt):
        d = jnp.dot(qi_ref[0], kt, preferred_element_type=F32)
        score = None
        for h in range(N_IDX_HEADS):
            t = wi_ref[0, :, h:h + 1] * jnp.maximum(d[h * _QPAD:(h + 1) * _QPAD, :], 0.0)
            score = t if score is None else score + t
        return jnp.where(score == 0.0, 0.0, score)

    @pl.when(j < last)
    def _past():
        sc_ref[0] = scores(jnp.concatenate([pg[0] for pg in pages], axis=1).astype(BF16))

    @pl.when(j == last)
    def _new():
        sc_ref[0] = jnp.zeros(sc_ref.shape[1:], F32)
        sc_ref[0, :, 0:PAGE_SIZE] = scores(knt_ref[0])


def _sample_select_kernel(sc_in_ref, bias_ref, sc_ref, st_ref, *, past, dec_seq):
    tk = _TK
    n_chunks = sc_ref.shape[0] // tk
    q = lax.broadcasted_iota(I32, (1, LANES), 1) % dec_seq
    krow = lax.broadcasted_iota(I32, (tk, 1), 0)

    def load(c, carry):
        off = pl.multiple_of(c * tk, tk)
        s = sc_in_ref[pl.ds(off, tk), :]
        valid = off + krow <= past + q
        sc_ref[pl.ds(off, tk), :] = jnp.where(valid, s, -jnp.inf)
        return _min_max_update(*carry, s, valid)

    mn, mx = lax.fori_loop(0, n_chunks, load, _min_max_init())
    n_valid = (past + 1 + q).astype(F32)
    kt = jnp.minimum(n_valid, float(min(TOPK_MAX, (past + dec_seq) // 4)))
    thr = _select_threshold(sc_ref, st_ref, n_chunks, tk, kt, _reduce_rows(mn, jnp.min),
                            _reduce_rows(mx, jnp.max), n_valid)

    def emit(c, carry):
        off = pl.multiple_of(c * tk, tk)
        bias_ref[pl.ds(off, tk), :] = jnp.where(sc_ref[pl.ds(off, tk), :] >= thr, 0.0, -jnp.inf)
        return carry

    lax.fori_loop(0, n_chunks, emit, 0)


def _sample_attend_kernel(pt_ref, q_ref, bias_ref, knt_ref, vnt_ref, *rest):
    kpages, vpages = rest[:_PAGES_KV], rest[_PAGES_KV:2 * _PAGES_KV]
    o_ref, m_ref, l_ref, acc_ref = rest[2 * _PAGES_KV:]
    j = pl.program_id(1)
    last = pl.num_programs(1) - 1

    @pl.when(j == 0)
    def _init():
        m_ref[...] = jnp.full(m_ref.shape, _NEG, F32)
        l_ref[...] = jnp.zeros(l_ref.shape, F32)
        acc_ref[...] = jnp.zeros(acc_ref.shape, F32)

    def attend(kt, vt, bias):
        s = jnp.dot(q_ref[0], kt, preferred_element_type=F32)
        s = jnp.concatenate([s[h * _QPAD:(h + 1) * _QPAD] + bias for h in range(N_HEADS)], axis=0)
        m_old = m_ref[...]
        m_new = jnp.maximum(m_old, jnp.max(s, axis=1, keepdims=True))
        alpha = jnp.exp2(m_old - m_new)
        p = jnp.exp2(s - m_new)
        l_ref[...] = alpha * l_ref[...] + jnp.sum(p, axis=1, keepdims=True)
        acc_ref[...] = alpha * acc_ref[...] + lax.dot_general(p.astype(BF16), vt, _NT,
                                                              preferred_element_type=F32)
        m_ref[...] = m_new

    @pl.when(j < last)
    def _past():
        attend(jnp.concatenate([pg[0] for pg in kpages], axis=1).astype(BF16),
               jnp.concatenate([pg[0] for pg in vpages], axis=1).astype(BF16), bias_ref[0])

    @pl.when(j == last)
    def _new():
        attend(knt_ref[0], vnt_ref[0], bias_ref[0, :, 0:PAGE_SIZE])
        o_ref[0] = acc_ref[...] / l_ref[...]


def _dsa_sample(q_b, k_b, v_b, qi_b, ki_b, wi, k_pool, v_pool, ki_pool, page_table, nb, s):
    n_pages = page_table.shape[1]
    past = n_pages * PAGE_SIZE
    assert _QPAD % s == 0 and (nb * _QPAD) % LANES == 0
    dup = jnp.arange(_QPAD) % s
    qi = qi_b.reshape(nb, s, N_IDX_HEADS, IDX_DIM)[:, dup].transpose(0, 2, 1, 3)
    qi = qi.reshape(nb, N_IDX_HEADS * _QPAD, IDX_DIM)
    wi8 = wi.reshape(nb, s, N_IDX_HEADS)[:, dup]
    new_page = lambda a: jnp.pad(a.reshape(nb, s, -1).transpose(0, 2, 1), ((0, 0), (0, 0), (0, PAGE_SIZE - s)))
    kint, knt, vnt = new_page(ki_b), new_page(k_b), new_page(v_b)

    n_sc = n_pages // _PAGES_SC
    sc_w = _PAGES_SC * PAGE_SIZE

    def page_spec(width, per_step, r):
        return pl.BlockSpec((1, width, PAGE_SIZE),
                            lambda b, j, pt: (pt[b, jnp.minimum(j * per_step + r, n_pages - 1)], 0, 0))

    scores = pl.pallas_call(
        _sample_scores_kernel,
        out_shape=jax.ShapeDtypeStruct((nb, _QPAD, (n_sc + 1) * sc_w), F32),
        grid_spec=pltpu.PrefetchScalarGridSpec(
            num_scalar_prefetch=1, grid=(nb, n_sc + 1),
            in_specs=[pl.BlockSpec((1, N_IDX_HEADS * _QPAD, IDX_DIM), lambda b, j, pt: (b, 0, 0)),
                      pl.BlockSpec((1, _QPAD, N_IDX_HEADS), lambda b, j, pt: (b, 0, 0)),
                      pl.BlockSpec((1, IDX_DIM, PAGE_SIZE), lambda b, j, pt: (b, 0, 0))]
                     + [page_spec(IDX_DIM, _PAGES_SC, r) for r in range(_PAGES_SC)],
            out_specs=pl.BlockSpec((1, _QPAD, sc_w), lambda b, j, pt: (b, 0, j))),
        compiler_params=_params("parallel", "arbitrary"),
        name="sample_scores",
    )(page_table, qi, wi8, kint, *([ki_pool] * _PAGES_SC))

    width = scores.shape[2]
    cols = nb * _QPAD
    bias = pl.pallas_call(
        functools.partial(_sample_select_kernel, past=past, dec_seq=s),
        out_shape=jax.ShapeDtypeStruct((width, cols), F32),
        grid=(cols // LANES,),
        in_specs=[pl.BlockSpec((width, LANES), lambda i: (0, i))],
        out_specs=pl.BlockSpec((width, LANES), lambda i: (0, i)),
        scratch_shapes=[pltpu.VMEM((width, LANES), F32), pltpu.VMEM((_SUBLANES, LANES), F32)],
        compiler_params=_params("parallel"),
        name="sample_select",
    )(scores.reshape(cols, width).T).T.reshape(nb, _QPAD, width)

    q4 = q_b.reshape(nb, s, N_KV_HEADS, Q_PER_KV, HEAD_DIM)[:, dup]
    eye = jnp.eye(N_KV_HEADS, dtype=q_b.dtype)
    qx = jnp.einsum("bqgjd,gk->bgjqkd", q4, eye).reshape(nb, N_HEADS * _QPAD, KV_W)
    n_kv = n_pages // _PAGES_KV
    kv_w = _PAGES_KV * PAGE_SIZE
    new_blk = past // kv_w
    hq = N_HEADS * _QPAD
    o = pl.pallas_call(
        _sample_attend_kernel,
        out_shape=jax.ShapeDtypeStruct((nb, hq, KV_W), F32),
        grid_spec=pltpu.PrefetchScalarGridSpec(
            num_scalar_prefetch=1, grid=(nb, n_kv + 1),
            in_specs=[pl.BlockSpec((1, hq, KV_W), lambda b, j, pt: (b, 0, 0)),
                      pl.BlockSpec((1, _QPAD, kv_w), lambda b, j, pt: (b, 0, jnp.minimum(j, new_blk))),
                      pl.BlockSpec((1, KV_W, PAGE_SIZE), lambda b, j, pt: (b, 0, 0)),
                      pl.BlockSpec((1, KV_W, PAGE_SIZE), lambda b, j, pt: (b, 0, 0))]
                     + [page_spec(KV_W, _PAGES_KV, r) for r in range(_PAGES_KV)] * 2,
            out_specs=pl.BlockSpec((1, hq, KV_W), lambda b, j, pt: (b, 0, 0)),
            scratch_shapes=[pltpu.VMEM((hq, 1), F32), pltpu.VMEM((hq, 1), F32), pltpu.VMEM((hq, KV_W), F32)]),
        compiler_params=_params("parallel", "arbitrary"),
        name="sample_attend",
    )(page_table, qx, bias, knt, vnt, *([k_pool] * _PAGES_KV), *([v_pool] * _PAGES_KV))
    o = o.reshape(nb, N_KV_HEADS, Q_PER_KV, _QPAD, N_KV_HEADS, HEAD_DIM)[:, :, :, :s]
    o = jnp.einsum("bgjqkd,gk->bqgjd", o, jnp.eye(N_KV_HEADS, dtype=o.dtype))
    return o.reshape(nb * s, ATTN_W)


_ROUTE_W = 8


def _merge_kernel(l_ref, x_ref, yc_ref, o_ref, g_ref, wao_ref, wout_ref, lg_ref, lb_ref, rwh_ref, rwl_ref,
                  rb_ref, x1_ref, x1b_ref, route_ref, *, alpha):
    y_attn = jnp.dot(o_ref[...].astype(BF16), wao_ref[...], preferred_element_type=F32)
    mix_in = g_ref[:, :D_MODEL] * yc_ref[...] + g_ref[:, D_MODEL:] * y_attn
    mix = jnp.dot(mix_in.astype(BF16), wout_ref[...], preferred_element_type=F32)
    x1 = _layer_norm(alpha * x_ref[...] + mix, lg_ref[...], lb_ref[...])
    x1_ref[...] = x1
    hi = x1.astype(BF16)
    x1b_ref[...] = hi

    lo = (x1 - hi.astype(F32)).astype(BF16)
    logits = (jnp.dot(hi, rwh_ref[...], preferred_element_type=F32)
              + jnp.dot(hi, rwl_ref[...], preferred_element_type=F32)
              + jnp.dot(lo, rwh_ref[...], preferred_element_type=F32)) + rb_ref[...]
    lane = lax.broadcasted_iota(I32, (1, LANES), 1)

    def first_max(v):
        m = jnp.max(v, axis=1, keepdims=True)
        return m, jnp.min(jnp.where(v == m, lane, LANES), axis=1, keepdims=True)

    is_group = (lane >= N_EXPERTS) & (lane < N_EXPERTS + N_GROUPS)
    gl = jnp.where(is_group, logits, -jnp.inf)
    gm, gidx = first_max(gl)
    p_grp = 1.0 / jnp.sum(jnp.exp(gl - gm), axis=1, keepdims=True)
    grp = gidx - N_EXPERTS
    group_of_lane = lane >> (EXP_PER_GROUP.bit_length() - 1)
    el = jnp.where(group_of_lane == grp, logits, -jnp.inf)
    e1, i1 = first_max(el)
    e2, i2 = first_max(jnp.where(lane == i1, -jnp.inf, el))
    t = jnp.exp(e2 - e1)
    g1 = p_grp / (1.0 + t)
    g2 = g1 * t
    rl = lax.broadcasted_iota(I32, (1, _ROUTE_W), 1)
    route_ref[...] = jnp.where(rl == 0, i1.astype(F32), jnp.where(rl == 1, i2.astype(F32),
                               jnp.where(rl == 2, g1, jnp.where(rl == 3, g2, 0.0))))


def _merge(lidx, x2d, yc, o, g, p, tm, alpha):
    t = x2d.shape[0]
    row = lambda n: pl.BlockSpec((tm, n), lambda i, l: (i, 0))
    return pl.pallas_call(
        functools.partial(_merge_kernel, alpha=alpha),
        out_shape=[jax.ShapeDtypeStruct((t, D_MODEL), F32), jax.ShapeDtypeStruct((t, D_MODEL), BF16),
                   jax.ShapeDtypeStruct((t, _ROUTE_W), F32)],
        grid_spec=pltpu.PrefetchScalarGridSpec(
            num_scalar_prefetch=1, grid=(t // tm,),
            in_specs=[row(D_MODEL), row(D_MODEL), row(ATTN_W), row(2 * D_MODEL),
                      _lspec(ATTN_W, D_MODEL), _lspec(D_MODEL, D_MODEL), _lspec(1, D_MODEL),
                      _lspec(1, D_MODEL), _lspec(D_MODEL, LANES), _lspec(D_MODEL, LANES), _lspec(1, LANES)],
            out_specs=[row(D_MODEL), row(D_MODEL), row(_ROUTE_W)]),
        compiler_params=_params("parallel"),
        name="merge_ln1_router",
    )(lidx, x2d, yc, o, g, p["w_attn_out"], p["w_out"], p["ln1_g"], p["ln1_b"], p["router_hi"],
      p["router_lo"], p["router_b"])


_SLOT_ALIGN = SUBLANES_BF16
_ROW_BLK = 64
_EXPERTS_PER_STEP = 4
_MOE_CHUNK = 256
_META_W = LANES
_MOE_TILE_MAX = 256 * _SLOT_ALIGN


def _moe_plan_kernel(route_ref, dcol_ref, drow_ref, meta_ref):
    tm = route_ref.shape[0]
    blk_rows = min(_MOE_CHUNK, tm)
    lane = lax.broadcasted_iota(I32, (1, LANES), 1).astype(F32)
    hit0 = lane == route_ref[:, 0:1]
    hit1 = lane == route_ref[:, 1:2]
    onehot = jnp.where(hit0, 1.0, jnp.where(hit1, 1.0, 0.0))
    r = lax.broadcasted_iota(I32, (blk_rows, blk_rows), 0)
    c = lax.broadcasted_iota(I32, (blk_rows, blk_rows), 1)
    tri = jnp.where(c < r, 1.0, 0.0).astype(BF16)
    carry = jnp.zeros((1, LANES), F32)
    prefix = []
    for b0 in range(0, tm, blk_rows):
        blk = onehot[b0:b0 + blk_rows]
        prefix.append(jnp.dot(tri, blk.astype(BF16), preferred_element_type=F32) + carry)
        carry = carry + jnp.sum(blk, axis=0, keepdims=True)
    prefix = jnp.concatenate(prefix, axis=0)
    units = jnp.ceil(carry * (1.0 / _SLOT_ALIGN))
    rr = lax.broadcasted_iota(I32, (LANES, LANES), 0)
    cc = lax.broadcasted_iota(I32, (LANES, LANES), 1)
    upper = jnp.where(rr < cc, 1.0, 0.0).astype(BF16)
    units8 = jnp.broadcast_to(units, (8, LANES)).astype(BF16)
    offs = jnp.dot(units8, upper, preferred_element_type=F32)[0:1] * _SLOT_ALIGN
    slot = offs + prefix
    d0 = jnp.sum(jnp.where(hit0, slot, 0.0), axis=1, keepdims=True)
    d1 = jnp.sum(jnp.where(hit1, slot, 0.0), axis=1, keepdims=True)
    rl = lax.broadcasted_iota(I32, (1, LANES), 1)
    rec = jnp.where(rl == 0, d0, jnp.where(rl == 1, d1, jnp.where(rl == 2, route_ref[:, 2:3],
                    jnp.where(rl == 3, route_ref[:, 3:4], 0.0))))
    dcol_ref[...] = rec[:, :_ROUTE_W]
    drow_ref[...] = rec.T[:_ROUTE_W, :]
    nblk = jnp.ceil(units * (_SLOT_ALIGN / _ROW_BLK))
    shift = jnp.where(cc == rr + N_EXPERTS, 1.0, 0.0).astype(BF16)
    nblk_sh = jnp.dot(jnp.broadcast_to(nblk, (8, LANES)).astype(BF16), shift,
                      preferred_element_type=F32)[0:1]
    meta_ref[0] = jnp.where(rl < N_EXPERTS, offs, nblk_sh).astype(I32)


def _moe_plan(route, tm):
    t = route.shape[0]
    nt = t // tm
    assert tm <= _MOE_TILE_MAX
    return pl.pallas_call(
        _moe_plan_kernel,
        out_shape=[jax.ShapeDtypeStruct((t, _ROUTE_W), F32), jax.ShapeDtypeStruct((_ROUTE_W, t), F32),
                   jax.ShapeDtypeStruct((nt, 1, _META_W), I32)],
        grid=(nt,),
        in_specs=[pl.BlockSpec((tm, _ROUTE_W), lambda i: (i, 0))],
        out_specs=[pl.BlockSpec((tm, _ROUTE_W), lambda i: (i, 0)),
                   pl.BlockSpec((_ROUTE_W, tm), lambda i: (0, i)),
                   pl.BlockSpec((1, 1, _META_W), lambda i: (i, 0, 0))],
        compiler_params=_params("parallel"),
        name="moe_plan",
    )(route)


def _n_slots(tm):
    n = 2 * tm + N_EXPERTS * (_SLOT_ALIGN - 1) + _ROW_BLK
    return -(-n // _MOE_CHUNK) * _MOE_CHUNK


def _moe_kernel(meta_ref, l_ref, x1b_ref, x1_ref, drow_ref, dcol_ref, wg_ref, wu_ref, wd_ref, lg_ref, lb_ref,
                x2_ref, xb_ref, yb_ref, gs_ref, *, alpha):
    i, e = pl.program_id(0), pl.program_id(1)
    tm = x1_ref.shape[0]
    n_slots = xb_ref.shape[0]

    @pl.when(e == 0)
    def _dispatch():
        d0, d1 = drow_ref[0:1, :], drow_ref[1:2, :]
        g0, g1 = drow_ref[2:3, :], drow_ref[3:4, :]
        for s0 in range(0, n_slots, _MOE_CHUNK):
            sl = (s0 + lax.broadcasted_iota(I32, (_MOE_CHUNK, 1), 0)).astype(F32)
            a, b = sl == d0, sl == d1
            p = jnp.where(a, 1.0, jnp.where(b, 1.0, 0.0)).astype(BF16)
            xb_ref[s0:s0 + _MOE_CHUNK, :] = jnp.dot(p, x1b_ref[...], preferred_element_type=F32).astype(BF16)
            gs_ref[s0:s0 + _MOE_CHUNK, :] = jnp.sum(jnp.where(a, g0, jnp.where(b, g1, 0.0)), axis=1,
                                                    keepdims=True)
        yb_ref[...] = jnp.zeros(yb_ref.shape, BF16)

    for k in range(_EXPERTS_PER_STEP):
        expert = e * _EXPERTS_PER_STEP + k
        off = meta_ref[i * _META_W + expert]
        nblk = meta_ref[i * _META_W + N_EXPERTS + expert]

        def block(r, carry, k=k, off=off):
            start = pl.multiple_of(off + r * _ROW_BLK, _SLOT_ALIGN)
            xs = xb_ref[pl.ds(start, _ROW_BLK), :]
            hg = jnp.dot(xs, wg_ref[k], preferred_element_type=F32)
            hu = jnp.dot(xs, wu_ref[k], preferred_element_type=F32)
            h = (hg * jax.nn.sigmoid(hg)) * hu
            y = jnp.dot(h.astype(BF16), wd_ref[k], preferred_element_type=F32)
            yb_ref[pl.ds(start, _ROW_BLK), :] = (y * gs_ref[pl.ds(start, _ROW_BLK), :]).astype(BF16)
            return carry

        lax.fori_loop(0, nblk, block, 0)

    @pl.when(e == pl.num_programs(1) - 1)
    def _combine():
        lane = lax.broadcasted_iota(I32, (1, n_slots), 1).astype(F32)
        rows = min(_MOE_CHUNK, tm)
        for t0 in range(0, tm, rows):
            d0, d1 = dcol_ref[t0:t0 + rows, 0:1], dcol_ref[t0:t0 + rows, 1:2]
            qm = jnp.where(lane == d0, 1.0, jnp.where(lane == d1, 1.0, 0.0)).astype(BF16)
            y = jnp.dot(qm, yb_ref[...], preferred_element_type=F32)
            z = alpha * x1_ref[t0:t0 + rows, :] + y
            x2_ref[t0:t0 + rows, :] = _layer_norm(z, lg_ref[...], lb_ref[...])


def _moe(lidx, x1, x1b, drow, dcol, meta, p, tm, alpha):
    t = x1.shape[0]
    n_slots = _n_slots(tm)
    tile = lambda shape, imap: pl.BlockSpec(shape, imap)
    return pl.pallas_call(
        functools.partial(_moe_kernel, alpha=alpha),
        out_shape=jax.ShapeDtypeStruct((t, D_MODEL), F32),
        grid_spec=pltpu.PrefetchScalarGridSpec(
            num_scalar_prefetch=2, grid=(t // tm, N_EXPERTS // _EXPERTS_PER_STEP),
            in_specs=[tile((tm, D_MODEL), lambda i, e, m, l: (i, 0)),
                      tile((tm, D_MODEL), lambda i, e, m, l: (i, 0)),
                      tile((_ROUTE_W, tm), lambda i, e, m, l: (0, i)),
                      tile((tm, _ROUTE_W), lambda i, e, m, l: (i, 0)),
                      tile((None, _EXPERTS_PER_STEP, D_MODEL, D_EXPERT), lambda i, e, m, l: (l[0], e, 0, 0)),
                      tile((None, _EXPERTS_PER_STEP, D_MODEL, D_EXPERT), lambda i, e, m, l: (l[0], e, 0, 0)),
                      tile((None, _EXPERTS_PER_STEP, D_EXPERT, D_MODEL), lambda i, e, m, l: (l[0], e, 0, 0)),
                      _lspec(1, D_MODEL), _lspec(1, D_MODEL)],
            out_specs=tile((tm, D_MODEL), lambda i, e, m, l: (i, 0)),
            scratch_shapes=[pltpu.VMEM((n_slots, D_MODEL), BF16), pltpu.VMEM((n_slots, D_MODEL), BF16),
                            pltpu.VMEM((n_slots, 1), F32)]),
        compiler_params=_params("parallel", "arbitrary"),
        name="moe_experts_ln2",
    )(meta.reshape(-1), lidx, x1b, x1, drow, dcol, p["w_gate"], p["w_up"], p["w_down"], p["ln2_g"], p["ln2_b"])


def _token_tile(t, want):
    return want if t % want == 0 else t


def _decoder_layer(lidx, x, prefix, attend, p, alpha):
    nb, s, _ = x.shape
    t = nb * s
    x2d = x.reshape(t, D_MODEL)
    u, q_b, k, v, qi_b, ki, wi, g, k_b, v_b, ki_b = _inproj(lidx, x2d, p["w_in"], p["b_in"], _token_tile(t, 512))

    ctm = 512 if s % 512 == 0 else _HALO
    u3 = u.reshape(nb, s, C_CONV)
    history = jnp.pad(prefix, ((0, 0), (_HALO - (CONV_W - 1), 0), (0, 0)))
    u_rows = jnp.pad(u3, ((0, 0), (0, (-s) % ctm), (0, 0)))
    yc = _conv_branch(lidx, history, u_rows, p["conv_w"], p["conv_b"], p["conv_ln_g"], p["conv_ln_b"],
                      p["w_conv_out"], ctm)
    yc = yc[:, :s].reshape(t, D_MODEL)

    o = attend(q_b, k_b, v_b, qi_b, ki_b, wi)
    x1, x1b, route = _merge(lidx, x2d, yc, o, g, p, _token_tile(t, 256), alpha)
    mtm = _token_tile(t, 1024)
    dcol, drow, meta = _moe_plan(route, mtm)
    x2 = _moe(lidx, x1, x1b, drow, dcol, meta, p, mtm, alpha)

    new_conv = jnp.concatenate([prefix, u3], axis=1)[:, -(CONV_W - 1):]
    return (x2.reshape(nb, s, D_MODEL), k.reshape(nb, s, N_KV_HEADS, HEAD_DIM),
            v.reshape(nb, s, N_KV_HEADS, HEAD_DIM), ki.reshape(nb, s, IDX_DIM), new_conv)


def _prepare_params(w_in, b_in, conv_w, conv_b, conv_ln_g, conv_ln_b, w_conv_out, w_attn_out, w_out,
                    ln1_g, ln1_b, router_group_w, router_group_b, router_expert_w, router_expert_b,
                    w_gate, w_up, w_down, ln2_g, ln2_b):
    depth = w_in.shape[0]
    pad = _COL_G - _N_IN_HEAD
    w_pad = jnp.concatenate([w_in[..., :_N_IN_HEAD], jnp.zeros((depth, D_MODEL, pad), F32),
                             w_in[..., _N_IN_HEAD:]], axis=-1).astype(BF16)
    b_pad = jnp.concatenate([b_in[..., :_N_IN_HEAD], jnp.zeros((depth, pad), F32),
                             b_in[..., _N_IN_HEAD:]], axis=-1)[:, None, :]
    rpad = LANES - N_EXPERTS - N_GROUPS
    rw = jnp.concatenate([router_expert_w, router_group_w, jnp.zeros((depth, D_MODEL, rpad), F32)], axis=-1)
    rb = jnp.concatenate([router_expert_b, router_group_b, jnp.zeros((depth, rpad), F32)], axis=-1)
    rw_hi = rw.astype(BF16)
    rw_lo = (rw - rw_hi.astype(F32)).astype(BF16)
    vec = lambda a: a[:, None, :]
    return dict(w_in=w_pad, b_in=b_pad, conv_w=conv_w, conv_b=vec(conv_b), conv_ln_g=vec(conv_ln_g),
                conv_ln_b=vec(conv_ln_b), w_conv_out=w_conv_out.astype(BF16),
                w_attn_out=w_attn_out.astype(BF16), w_out=w_out.astype(BF16), ln1_g=vec(ln1_g),
                ln1_b=vec(ln1_b), router_hi=rw_hi, router_lo=rw_lo, router_b=vec(rb),
                w_gate=w_gate.astype(BF16), w_up=w_up.astype(BF16), w_down=w_down.astype(BF16),
                ln2_g=vec(ln2_g), ln2_b=vec(ln2_b))


def kernel(x_prompt, x_sample, cache_k, cache_v, cache_kidx, state_conv, page_table, w_in, b_in, conv_w,
           conv_b, conv_ln_g, conv_ln_b, w_conv_out, w_attn_out, w_out, ln1_g, ln1_b, router_group_w,
           router_group_b, router_expert_w, router_expert_b, w_gate, w_up, w_down, ln2_g, ln2_b):
    params = _prepare_params(w_in, b_in, conv_w, conv_b, conv_ln_g, conv_ln_b, w_conv_out, w_attn_out,
                             w_out, ln1_g, ln1_b, router_group_w, router_group_b, router_expert_w,
                             router_expert_b, w_gate, w_up, w_down, ln2_g, ln2_b)
    nb, s, _ = x_prompt.shape
    db, ds, _ = x_sample.shape
    depth, n_phys = cache_k.shape[:2]
    alpha = (2 * depth) ** 0.25
    k_pool = cache_k.transpose(0, 1, 3, 4, 2).reshape(depth * n_phys, KV_W, PAGE_SIZE)
    v_pool = cache_v.transpose(0, 1, 3, 4, 2).reshape(depth * n_phys, KV_W, PAGE_SIZE)
    ki_pool = cache_kidx.transpose(0, 1, 3, 2).reshape(depth * n_phys, IDX_DIM, PAGE_SIZE)
    conv_zero = jnp.zeros((nb, CONV_W - 1, C_CONV), F32)

    def layer(carry, xs):
        xp, xsm = carry
        l, st = xs
        lidx = l.reshape(1)
        prompt_attend = functools.partial(_dsa_prompt, nb=nb, s=s)
        xp, *new_p = _decoder_layer(lidx, xp, conv_zero, prompt_attend, params, alpha)
        sample_attend = functools.partial(_dsa_sample, k_pool=k_pool, v_pool=v_pool, ki_pool=ki_pool,
                                          page_table=page_table + l * n_phys, nb=db, s=ds)
        xsm, *new_s = _decoder_layer(lidx, xsm, st, sample_attend, params, alpha)
        return (xp, xsm), (tuple(new_p), tuple(new_s))

    (xp, xsm), (new_p, new_s) = lax.scan(layer, (x_prompt, x_sample),
                                         (jnp.arange(depth, dtype=I32), state_conv))
    return (xp, xsm, *new_p, *new_s)
```

```python
import functools

import jax
import jax.numpy as jnp
from jax import lax
from jax.experimental import pallas as pl
from jax.experimental.pallas import tpu as pltpu

F32 = jnp.float32
BF16 = jnp.bfloat16
I32 = jnp.int32

D_MODEL = 1024
PAGE_SIZE = 128
C_CONV = 512
CONV_W = 31
N_HEADS = 8
N_KV_HEADS = 4
HEAD_DIM = 64
Q_PER_KV = N_HEADS // N_KV_HEADS
ATTN_W = N_HEADS * HEAD_DIM
KV_W = N_KV_HEADS * HEAD_DIM
N_IDX_HEADS = 8
IDX_DIM = 64
TOPK_MAX = 256
N_GROUPS = 4
EXP_PER_GROUP = 8
N_EXPERTS = N_GROUPS * EXP_PER_GROUP
D_EXPERT = 256
LN_EPS = 1e-5

LANES = 128
_SUBLANES = 8
SUBLANES_BF16 = 16
VMEM_LIMIT = 56 * 1024 * 1024

_N_SMALL = IDX_DIM + N_IDX_HEADS
_COL_CA = 0
_COL_CB = _COL_CA + C_CONV
_COL_Q = _COL_CB + C_CONV
_COL_K = _COL_Q + ATTN_W
_COL_V = _COL_K + KV_W
_COL_QI = _COL_V + KV_W
_COL_KI = _COL_QI + N_IDX_HEADS * IDX_DIM
_COL_G = _COL_KI + LANES
_N_IN_PAD = _COL_G + 2 * D_MODEL
_N_IN_HEAD = _COL_KI + _N_SMALL

_NT = (((1,), (1,)), ((), ()))
_NEG = -1e30
_LOG2E = 1.4426950408889634


def _params(*sem):
    return pltpu.CompilerParams(dimension_semantics=sem, vmem_limit_bytes=VMEM_LIMIT)


def _lspec(*shape):
    nd = len(shape)
    return pl.BlockSpec((None,) + shape, lambda *a: (a[-1][0],) + (0,) * nd)


def _layer_norm(x, g, b):
    mu = jnp.mean(x, axis=-1, keepdims=True)
    xc = x - mu
    var = jnp.mean(xc * xc, axis=-1, keepdims=True)
    return xc * lax.rsqrt(var + LN_EPS) * g + b


def _inproj_kernel(l_ref, x_ref, w_ref, b_ref, u_ref, q_ref, k_ref, v_ref, qi_ref, ki_ref, wi_ref, g_ref,
                   kb_ref, vb_ref, kib_ref):
    xb = x_ref[...].astype(BF16)

    def proj(c0, n):
        return jnp.dot(xb, w_ref[:, c0:c0 + n], preferred_element_type=F32) + b_ref[:, c0:c0 + n]

    u_ref[...] = proj(_COL_CA, C_CONV) * jax.nn.sigmoid(proj(_COL_CB, C_CONV))
    q_ref[...] = (proj(_COL_Q, ATTN_W) * (HEAD_DIM ** -0.5 * _LOG2E)).astype(BF16)
    k = proj(_COL_K, KV_W)
    k_ref[...] = k
    kb_ref[...] = k.astype(BF16)
    v = proj(_COL_V, KV_W)
    v_ref[...] = v
    vb_ref[...] = v.astype(BF16)
    qi_ref[...] = (proj(_COL_QI, N_IDX_HEADS * IDX_DIM) * (IDX_DIM ** -0.5)).astype(BF16)
    small = proj(_COL_KI, LANES)
    ki = small[:, :IDX_DIM]
    ki_ref[...] = ki
    kib_ref[...] = ki.astype(BF16)
    wi_ref[...] = small[:, IDX_DIM:_N_SMALL] * (N_IDX_HEADS ** -0.5)
    g_ref[...] = jax.nn.sigmoid(proj(_COL_G, 2 * D_MODEL))


def _inproj(lidx, x2d, w_pad, b_pad, tm):
    t = x2d.shape[0]
    widths = [(C_CONV, F32), (ATTN_W, BF16), (KV_W, F32), (KV_W, F32), (N_IDX_HEADS * IDX_DIM, BF16),
              (IDX_DIM, F32), (N_IDX_HEADS, F32), (2 * D_MODEL, F32), (KV_W, BF16), (KV_W, BF16),
              (IDX_DIM, BF16)]
    return pl.pallas_call(
        _inproj_kernel,
        out_shape=[jax.ShapeDtypeStruct((t, n), dt) for n, dt in widths],
        grid_spec=pltpu.PrefetchScalarGridSpec(
            num_scalar_prefetch=1, grid=(t // tm,),
            in_specs=[pl.BlockSpec((tm, D_MODEL), lambda i, l: (i, 0)),
                      _lspec(D_MODEL, _N_IN_PAD), _lspec(1, _N_IN_PAD)],
            out_specs=[pl.BlockSpec((tm, n), lambda i, l: (i, 0)) for n, _ in widths]),
        compiler_params=_params("parallel"),
        name="inproj",
    )(lidx, x2d, w_pad, b_pad)


_HALO = 32
_CONV_ROWS = 32


def _conv_kernel(l_ref, first_ref, prev_ref, cur_ref, cw_ref, cb_ref, lg_ref, lb_ref, wo_ref, y_ref,
                 hist_ref, acc_ref, shift_ref):
    tm = cur_ref.shape[1]
    at_start = pl.program_id(1) == 0
    hist_ref[0:_HALO, :] = jnp.where(at_start, first_ref[0], prev_ref[0])
    hist_ref[_HALO:_HALO + tm, :] = cur_ref[0]
    first = _HALO - (CONV_W - 1)
    for b in range(_SUBLANES):
        rows = tm + _SUBLANES * ((CONV_W - 1 - b) // _SUBLANES)
        shift_ref[b, 0:rows, :] = hist_ref[first + b:first + b + rows, :]
    for r0 in range(0, tm, _CONV_ROWS):
        acc = jnp.zeros((_CONV_ROWS, C_CONV), F32)
        for j in range(CONV_W):
            a, b = divmod(j, _SUBLANES)
            acc = acc + cw_ref[j:j + 1, :] * shift_ref[b, r0 + a * _SUBLANES:r0 + a * _SUBLANES + _CONV_ROWS, :]
        acc_ref[r0:r0 + _CONV_ROWS, :] = acc + cb_ref[...]
    y = _layer_norm(acc_ref[...], lg_ref[...], lb_ref[...])
    y = y * jax.nn.sigmoid(y)
    y_ref[0] = jnp.dot(y.astype(BF16), wo_ref[...], preferred_element_type=F32)


def _conv_branch(lidx, history, u, conv_w, conv_b, ln_g, ln_b, w_out_bf, tm):
    nb, length, _ = u.shape
    halo_blocks = tm // _HALO
    return pl.pallas_call(
        _conv_kernel,
        out_shape=jax.ShapeDtypeStruct((nb, length, D_MODEL), F32),
        grid_spec=pltpu.PrefetchScalarGridSpec(
            num_scalar_prefetch=1, grid=(nb, length // tm),
            in_specs=[pl.BlockSpec((1, _HALO, C_CONV), lambda b, i, l: (b, 0, 0)),
                      pl.BlockSpec((1, _HALO, C_CONV), lambda b, i, l: (b, jnp.maximum(i * halo_blocks - 1, 0), 0)),
                      pl.BlockSpec((1, tm, C_CONV), lambda b, i, l: (b, i, 0)),
                      _lspec(CONV_W, C_CONV), _lspec(1, C_CONV), _lspec(1, C_CONV), _lspec(1, C_CONV),
                      _lspec(C_CONV, D_MODEL)],
            out_specs=pl.BlockSpec((1, tm, D_MODEL), lambda b, i, l: (b, i, 0)),
            scratch_shapes=[pltpu.VMEM((_HALO + tm, C_CONV), F32), pltpu.VMEM((tm, C_CONV), F32),
                            pltpu.VMEM((_SUBLANES, tm + _HALO - _SUBLANES, C_CONV), F32)]),
        compiler_params=_params("parallel", "parallel"),
        name="conv_branch",
    )(lidx, history, u, u, conv_w, conv_b, ln_g, ln_b, w_out_bf)


_BISECT_CAP = 40
_BLIND_STEPS = 12
_STEPS_PER_CHECK = 2
_IDX_STEPS = 14


_FOLD_ROWS = 64


def _reduce_rows(x, op):
    rows, cols = x.shape
    if rows > _FOLD_ROWS:
        x = op(x.reshape(rows // _FOLD_ROWS, _FOLD_ROWS, cols), axis=0)
    x = op(x.reshape(x.shape[0] // _SUBLANES, _SUBLANES, cols), axis=0)
    return op(x, axis=0, keepdims=True)


def _fold_keys(sc_ref, n_chunks, tk, init, fn, op, merge):
    def body(c, acc):
        off = pl.multiple_of(c * tk, tk)
        v = fn(sc_ref[pl.ds(off, tk), :], off).reshape(tk // _FOLD_ROWS, _FOLD_ROWS, cols)
        return merge(acc, op(v, axis=0))

    cols = sc_ref.shape[1]
    acc = lax.fori_loop(0, n_chunks, body, jnp.full((_FOLD_ROWS, cols), init, F32))
    return _reduce_rows(acc, op)


def _count_keys(sc_ref, n_chunks, tk, fn):
    return _fold_keys(sc_ref, n_chunks, tk, 0.0, fn, jnp.sum, jnp.add)


def _count_ge(sc_ref, n_chunks, tk, thr):
    return _count_keys(sc_ref, n_chunks, tk, lambda blk, off: jnp.where(blk >= thr, 1.0, 0.0))


def _any(flag):
    return jnp.max(jnp.where(flag, 1, 0))


def _select_threshold(sc_ref, st_ref, n_chunks, tk, kt, s_min, s_max, n_valid):
    lo_ref, hi_ref, cl_ref, done_ref, jlo_ref, jhi_ref = (st_ref.at[n:n + 1] for n in range(6))
    c_max = _count_ge(sc_ref, n_chunks, tk, s_max)
    top = c_max >= kt
    lo_ref[...] = jnp.where(top, s_max, s_min)
    hi_ref[...] = s_max
    cl0 = jnp.where(top, c_max, n_valid)
    cl_ref[...] = cl0
    done0 = jnp.where(top, 1.0, jnp.where(cl0 == kt, 1.0, 0.0))
    done_ref[...] = done0

    def bisect_step():
        lo, hi, cl, done = lo_ref[...], hi_ref[...], cl_ref[...], done_ref[...]
        mid = 0.5 * lo + 0.5 * hi
        c = _count_ge(sc_ref, n_chunks, tk, mid)
        act = done == 0.0
        up = c >= kt
        lo_ref[...] = jnp.where(act, jnp.where(up, mid, lo), lo)
        cl_ref[...] = jnp.where(act, jnp.where(up, c, cl), cl)
        hi_ref[...] = jnp.where(act, jnp.where(up, hi, mid), hi)
        done_ref[...] = jnp.where(act, jnp.where(c == kt, 1.0, 0.0), done)

    def bisect(carry):
        for _ in range(_STEPS_PER_CHECK):
            bisect_step()
        return carry[0] + _STEPS_PER_CHECK, _any(done_ref[...] == 0.0)

    @pl.when(_any(done0 == 0.0) > 0)
    def _search():
        lax.fori_loop(0, _BLIND_STEPS, lambda _, carry: (bisect_step(), carry)[1], 0)
        lax.while_loop(lambda c: (c[0] < _BISECT_CAP) & (c[1] > 0), bisect,
                       (jnp.int32(_BLIND_STEPS), _any(done_ref[...] == 0.0)))

    def snap(_):
        lo, hi, cl, done = lo_ref[...], hi_ref[...], cl_ref[...], done_ref[...]
        t1 = _fold_keys(sc_ref, n_chunks, tk, -jnp.inf, lambda blk, off: jnp.where(blk < hi, blk, -jnp.inf),
                        jnp.max, jnp.maximum)
        c1 = _count_ge(sc_ref, n_chunks, tk, t1)
        act = done == 0.0
        found = c1 >= kt
        lo_ref[...] = jnp.where(act, jnp.where(found, t1, lo), lo)
        cl_ref[...] = jnp.where(act, jnp.where(found, c1, cl), cl)
        hi_ref[...] = jnp.where(act, jnp.where(found, hi, t1), hi)
        done_new = jnp.where(act, jnp.where(found, 1.0, 0.0), done)
        done_ref[...] = done_new
        return _any(done_new == 0.0)

    lax.while_loop(lambda f: f > 0, snap, _any(done_ref[...] == 0.0))

    thr = lo_ref[...]

    @pl.when(_any(cl_ref[...] != kt) > 0)
    def _cut_ties():
        need = kt - _count_keys(sc_ref, n_chunks, tk, lambda blk, off: jnp.where(blk > thr, 1.0, 0.0))
        jlo_ref[...] = jnp.full_like(thr, -1.0)
        jhi_ref[...] = jnp.full_like(thr, float(sc_ref.shape[0] - 1))
        row = lax.broadcasted_iota(I32, (tk, 1), 0)

        def step(_, carry):
            jlo, jhi = jlo_ref[...], jhi_ref[...]
            mid = jnp.floor(0.5 * (jlo + jhi))
            c = _count_keys(
                sc_ref, n_chunks, tk,
                lambda blk, off: jnp.where(blk == thr, jnp.where((off + row).astype(F32) <= mid, 1.0, 0.0), 0.0))
            ok = c >= need
            jhi_ref[...] = jnp.where(ok, mid, jhi)
            jlo_ref[...] = jnp.where(ok, jlo, mid)
            return carry

        lax.fori_loop(0, _IDX_STEPS, step, 0)
        cut = jhi_ref[...]

        def demote(c, carry):
            off = pl.multiple_of(c * tk, tk)
            blk = sc_ref[pl.ds(off, tk), :]
            beyond = (off + row).astype(F32) > cut
            sc_ref[pl.ds(off, tk), :] = jnp.where(blk == thr, jnp.where(beyond, -jnp.inf, blk), blk)
            return carry

        lax.fori_loop(0, n_chunks, demote, 0)

    return thr


def _min_max_init(cols):
    return (jnp.full((_FOLD_ROWS, cols), jnp.inf, F32), jnp.full((_FOLD_ROWS, cols), -jnp.inf, F32))


def _min_max_update(mn, mx, score, valid):
    shape = (score.shape[0] // _FOLD_ROWS, _FOLD_ROWS, score.shape[1])
    mn = jnp.minimum(mn, jnp.min(jnp.where(valid, score, jnp.inf).reshape(shape), axis=0))
    mx = jnp.maximum(mx, jnp.max(jnp.where(valid, score, -jnp.inf).reshape(shape), axis=0))
    return mn, mx


_TQ = 256
_TK = 512


_IDX_PAIRS = N_IDX_HEADS // 2
_ONES_ROWS = SUBLANES_BF16


def _dsa_prompt_kernel(qi_ref, wi_ref, ki_ref, q_ref, k_ref, vt_ref, o_ref,
                       sc_ref, st_ref, m_ref, acc_ref, s_ref, cm_ref, p_ref, al_ref):
    tq, tk = _TQ, _TK
    i = pl.program_id(1)
    n_chunks = ((i + 1) * tq + tk - 1) // tk
    qpos = i * tq + lax.broadcasted_iota(I32, (1, tq), 1)
    krow = lax.broadcasted_iota(I32, (tk, 1), 0)

    def scores(c, carry):
        off = pl.multiple_of(c * tk, tk)
        kc = ki_ref[0, pl.ds(off, tk), :]
        score = None
        for p in range(_IDX_PAIRS):
            d = lax.dot_general(kc, qi_ref[0, 0, p], _NT, preferred_element_type=F32)
            for j in range(2):
                h = 2 * p + j
                t = wi_ref[0, h:h + 1, :] * jnp.maximum(d[:, j * tq:(j + 1) * tq], 0.0)
                score = t if score is None else score + t
        score = jnp.where(score == 0.0, 0.0, score)
        valid = off + krow <= qpos
        sc_ref[pl.ds(off, tk), :] = jnp.where(valid, score, -jnp.inf)
        return _min_max_update(*carry, score, valid)

    mn, mx = lax.fori_loop(0, n_chunks, scores, _min_max_init(tq))
    n_valid = (qpos + 1).astype(F32)
    kt = jnp.minimum(n_valid, float(TOPK_MAX))
    thr = _select_threshold(sc_ref, st_ref, n_chunks, tk, kt, _reduce_rows(mn, jnp.min),
                            _reduce_rows(mx, jnp.max), n_valid)

    m_ref[...] = jnp.full(m_ref.shape, _NEG, F32)
    acc_ref[...] = jnp.zeros(acc_ref.shape, F32)

    def mask_bias(c):
        off = pl.multiple_of(c * tk, tk)
        bias = jnp.where(sc_ref[pl.ds(off, tk), :] >= thr, 0.0, -jnp.inf)
        return jnp.concatenate([bias] * Q_PER_KV, axis=1)

    def qk_stage(c, slot, groups, bias=None):
        off = pl.multiple_of(c * tk, tk)
        bias = mask_bias(c) if bias is None else bias
        for g in groups:
            s = lax.dot_general(k_ref[0, g, pl.ds(off, tk), :], q_ref[0, 0, g], _NT,
                                preferred_element_type=F32) + bias
            s_ref[slot, g] = s
            cm_ref[slot, g] = _reduce_rows(s, jnp.max)

    def exp_stage(slot, groups):
        for g in groups:
            m_old = m_ref[g]
            m_new = jnp.maximum(m_old, cm_ref[slot, g])
            al_ref[slot, g] = jnp.exp2(m_old - m_new)
            p_ref[slot, g] = jnp.exp2(s_ref[slot, g] - m_new).astype(BF16)
            m_ref[g] = m_new

    def pv_stage(c, slot, groups):
        off = pl.multiple_of(c * tk, tk)
        for g in groups:
            acc_ref[g] = al_ref[slot, g] * acc_ref[g] + jnp.dot(vt_ref[0, g, :, pl.ds(off, tk)], p_ref[slot, g],
                                                                preferred_element_type=F32)

    every = range(N_KV_HEADS)

    def attend(c, carry):
        slot = c % 2
        for g in every:
            pv_stage(c, slot, (g,))
            qk_stage(c + 2, slot, (g,))
            exp_stage(1 - slot, (g,))
        return carry

    qk_stage(0, 0, every)
    exp_stage(0, every)

    @pl.when(n_chunks > 1)
    def _fill():
        qk_stage(1, 1, every)

    lax.fori_loop(0, n_chunks - 2, attend, 0)
    last = n_chunks - 1

    @pl.when(n_chunks > 1)
    def _drain():
        exp_stage(last % 2, every)
        pv_stage(last - 1, (last - 1) % 2, every)

    pv_stage(last, last % 2, every)
    for g in range(N_KV_HEADS):
        o_ref[0, 0, g] = acc_ref[g, 0:HEAD_DIM, :] / acc_ref[g, HEAD_DIM:HEAD_DIM + 1, :]


def _dsa_prompt(q_b, k_b, v_b, qi_b, ki_b, wi, nb, s):
    tq = _TQ
    nq = s // tq
    cols = Q_PER_KV * tq

    def pair_major(a, n_pairs):
        a = a.reshape(nb, nq, tq, n_pairs, 2, a.shape[-1] // (2 * n_pairs)).transpose(0, 1, 3, 4, 2, 5)
        return a.reshape(nb, nq, n_pairs, 2 * tq, -1)

    kh = k_b.reshape(nb, s, N_KV_HEADS, HEAD_DIM).transpose(0, 2, 1, 3)
    vt = v_b.reshape(nb, s, N_KV_HEADS, HEAD_DIM).transpose(0, 2, 3, 1)
    vt = jnp.concatenate([vt, jnp.ones((nb, N_KV_HEADS, _ONES_ROWS, s), vt.dtype)], axis=2)
    wit =wi.reshape(nb, s, N_IDX_HEADS).transpose(0, 2, 1)
    once = dict(pipeline_mode=pl.Buffered(1))
    o = pl.pallas_call(
        _dsa_prompt_kernel,
        out_shape=jax.ShapeDtypeStruct((nb, nq, N_KV_HEADS, HEAD_DIM, cols), F32),
        grid=(nb, nq),
        in_specs=[pl.BlockSpec((1, 1, _IDX_PAIRS, 2 * tq, IDX_DIM), lambda b, i: (b, i, 0, 0, 0)),
                  pl.BlockSpec((1, N_IDX_HEADS, tq), lambda b, i: (b, 0, i)),
                  pl.BlockSpec((1, s, IDX_DIM), lambda b, i: (b, 0, 0), **once),
                  pl.BlockSpec((1, 1, N_KV_HEADS, cols, HEAD_DIM), lambda b, i: (b, i, 0, 0, 0)),
                  pl.BlockSpec((1, N_KV_HEADS, s, HEAD_DIM), lambda b, i: (b, 0, 0, 0), **once),
                  pl.BlockSpec((1, N_KV_HEADS, HEAD_DIM + _ONES_ROWS, s), lambda b, i: (b, 0, 0, 0), **once)],
        out_specs=pl.BlockSpec((1, 1, N_KV_HEADS, HEAD_DIM, cols), lambda b, i: (b, i, 0, 0, 0)),
        scratch_shapes=[pltpu.VMEM((s, tq), F32), pltpu.VMEM((_SUBLANES, tq), F32),
                        pltpu.VMEM((N_KV_HEADS, 1, cols), F32),
                        pltpu.VMEM((N_KV_HEADS, HEAD_DIM + _ONES_ROWS, cols), F32),
                        pltpu.VMEM((2, N_KV_HEADS, _TK, cols), F32), pltpu.VMEM((2, N_KV_HEADS, 1, cols), F32),
                        pltpu.VMEM((2, N_KV_HEADS, _TK, cols), BF16), pltpu.VMEM((2, N_KV_HEADS, 1, cols), F32)],
        compiler_params=_params("parallel", "arbitrary"),
        name="dsa_prompt",
    )(pair_major(qi_b, _IDX_PAIRS), wit, ki_b.reshape(nb, s, IDX_DIM), pair_major(q_b, N_KV_HEADS), kh, vt)
    o = o.reshape(nb, nq, N_KV_HEADS, HEAD_DIM, Q_PER_KV, tq).transpose(0, 1, 5, 2, 4, 3)
    return o.reshape(nb * s, ATTN_W)


_QPAD = 8
_PAGES_SC = 32
_PAGES_KV = 16


def _sample_scores_kernel(pt_ref, qi_ref, wi_ref, knt_ref, *rest):
    pages, sc_ref = rest[:_PAGES_SC], rest[_PAGES_SC]
    j = pl.program_id(1)
    last = pl.num_programs(1) - 1

    def scores(kt):
        d = jnp.dot(qi_ref[0], kt, preferred_element_type=F32)
        score = None
        for h in range(N_IDX_HEADS):
            t = wi_ref[0, :, h:h + 1] * jnp.maximum(d[h * _QPAD:(h + 1) * _QPAD, :], 0.0)
            score = t if score is None else score + t
        return jnp.where(score == 0.0, 0.0, score)

    @pl.when(j < last)
    def _past():
        sc_ref[0] = scores(jnp.concatenate([pg[0] for pg in pages], axis=1).astype(BF16))

    @pl.when(j == last)
    def _new():
        sc_ref[0] = jnp.zeros(sc_ref.shape[1:], F32)
        sc_ref[0, :, 0:PAGE_SIZE] = scores(knt_ref[0])


def _sample_select_kernel(sc_in_ref, bias_ref, sc_ref, st_ref, *, past, dec_seq):
    tk = _TK
    n_chunks = sc_ref.shape[0] // tk
    q = lax.broadcasted_iota(I32, (1, LANES), 1) % dec_seq
    krow = lax.broadcasted_iota(I32, (tk, 1), 0)

    def load(c, carry):
        off = pl.multiple_of(c * tk, tk)
        s = sc_in_ref[pl.ds(off, tk), :]
        valid = off + krow <= past + q
        sc_ref[pl.ds(off, tk), :] = jnp.where(valid, s, -jnp.inf)
        return _min_max_update(*carry, s, valid)

    mn, mx = lax.fori_loop(0, n_chunks, load, _min_max_init(LANES))
    n_valid = (past + 1 + q).astype(F32)
    kt = jnp.minimum(n_valid, float(min(TOPK_MAX, (past + dec_seq) // 4)))
    thr = _select_threshold(sc_ref, st_ref, n_chunks, tk, kt, _reduce_rows(mn, jnp.min),
                            _reduce_rows(mx, jnp.max), n_valid)

    def emit(c, carry):
        off = pl.multiple_of(c * tk, tk)
        bias_ref[pl.ds(off, tk), :] = jnp.where(sc_ref[pl.ds(off, tk), :] >= thr, 0.0, -jnp.inf)
        return carry

    lax.fori_loop(0, n_chunks, emit, 0)


def _sample_attend_kernel(pt_ref, q_ref, bias_ref, knt_ref, vnt_ref, *rest):
    kpages, vpages = rest[:_PAGES_KV], rest[_PAGES_KV:2 * _PAGES_KV]
    o_ref, m_ref, l_ref, acc_ref = rest[2 * _PAGES_KV:]
    j = pl.program_id(1)
    last = pl.num_programs(1) - 1

    @pl.when(j == 0)
    def _init():
        m_ref[...] = jnp.full(m_ref.shape, _NEG, F32)
        l_ref[...] = jnp.zeros(l_ref.shape, F32)
        acc_ref[...] = jnp.zeros(acc_ref.shape, F32)

    def attend(kt, vt, bias):
        s = jnp.dot(q_ref[0], kt, preferred_element_type=F32)
        s = jnp.concatenate([s[h * _QPAD:(h + 1) * _QPAD] + bias for h in range(N_HEADS)], axis=0)
        m_old = m_ref[...]
        m_new = jnp.maximum(m_old, jnp.max(s, axis=1, keepdims=True))
        alpha = jnp.exp2(m_old - m_new)
        p = jnp.exp2(s - m_new)
        l_ref[...] = alpha * l_ref[...] + jnp.sum(p, axis=1, keepdims=True)
        acc_ref[...] = alpha * acc_ref[...] + lax.dot_general(p.astype(BF16), vt, _NT,
                                                              preferred_element_type=F32)
        m_ref[...] = m_new

    @pl.when(j < last)
    def _past():
        attend(jnp.concatenate([pg[0] for pg in kpages], axis=1).astype(BF16),
               jnp.concatenate([pg[0] for pg in vpages], axis=1).astype(BF16), bias_ref[0])

    @pl.when(j == last)
    def _new():
        attend(knt_ref[0], vnt_ref[0], bias_ref[0, :, 0:PAGE_SIZE])
        o_ref[0] = acc_ref[...] / l_ref[...]


def _dsa_sample(q_b, k_b, v_b, qi_b, ki_b, wi, k_pool, v_pool, ki_pool, page_table, nb, s):
    n_pages = page_table.shape[1]
    past = n_pages * PAGE_SIZE
    assert _QPAD % s == 0 and (nb * _QPAD) % LANES == 0
    assert n_pages % _PAGES_SC == 0 and n_pages % _PAGES_KV == 0
    dup = jnp.arange(_QPAD) % s
    qi = qi_b.reshape(nb, s, N_IDX_HEADS, IDX_DIM)[:, dup].transpose(0, 2, 1, 3)
    qi = qi.reshape(nb, N_IDX_HEADS * _QPAD, IDX_DIM)
    wi8 = wi.reshape(nb, s, N_IDX_HEADS)[:, dup]
    new_page = lambda a: jnp.pad(a.reshape(nb, s, -1).transpose(0, 2, 1), ((0, 0), (0, 0), (0, PAGE_SIZE - s)))
    kint, knt, vnt = new_page(ki_b), new_page(k_b), new_page(v_b)

    n_sc = n_pages // _PAGES_SC
    sc_w = _PAGES_SC * PAGE_SIZE

    def page_spec(width, per_step, r):
        return pl.BlockSpec((1, width, PAGE_SIZE),
                            lambda b, j, pt: (pt[b, jnp.minimum(j * per_step + r, n_pages - 1)], 0, 0))

    scores = pl.pallas_call(
        _sample_scores_kernel,
        out_shape=jax.ShapeDtypeStruct((nb, _QPAD, (n_sc + 1) * sc_w), F32),
        grid_spec=pltpu.PrefetchScalarGridSpec(
            num_scalar_prefetch=1, grid=(nb, n_sc + 1),
            in_specs=[pl.BlockSpec((1, N_IDX_HEADS * _QPAD, IDX_DIM), lambda b, j, pt: (b, 0, 0)),
                      pl.BlockSpec((1, _QPAD, N_IDX_HEADS), lambda b, j, pt: (b, 0, 0)),
                      pl.BlockSpec((1, IDX_DIM, PAGE_SIZE), lambda b, j, pt: (b, 0, 0))]
                     + [page_spec(IDX_DIM, _PAGES_SC, r) for r in range(_PAGES_SC)],
            out_specs=pl.BlockSpec((1, _QPAD, sc_w), lambda b, j, pt: (b, 0, j))),
        compiler_params=_params("parallel", "arbitrary"),
        name="sample_scores",
    )(page_table, qi, wi8, kint, *([ki_pool] * _PAGES_SC))

    width = scores.shape[2]
    cols = nb * _QPAD
    bias = pl.pallas_call(
        functools.partial(_sample_select_kernel, past=past, dec_seq=s),
        out_shape=jax.ShapeDtypeStruct((width, cols), F32),
        grid=(cols // LANES,),
        in_specs=[pl.BlockSpec((width, LANES), lambda i: (0, i))],
        out_specs=pl.BlockSpec((width, LANES), lambda i: (0, i)),
        scratch_shapes=[pltpu.VMEM((width, LANES), F32), pltpu.VMEM((_SUBLANES, LANES), F32)],
        compiler_params=_params("parallel"),
        name="sample_select",
    )(scores.reshape(cols, width).T).T.reshape(nb, _QPAD, width)

    q4 = q_b.reshape(nb, s, N_KV_HEADS, Q_PER_KV, HEAD_DIM)[:, dup]
    eye = jnp.eye(N_KV_HEADS, dtype=q_b.dtype)
    qx = jnp.einsum("bqgjd,gk->bgjqkd", q4, eye).reshape(nb, N_HEADS * _QPAD, KV_W)
    n_kv = n_pages // _PAGES_KV
    kv_w = _PAGES_KV * PAGE_SIZE
    new_blk = past // kv_w
    hq = N_HEADS * _QPAD
    o = pl.pallas_call(
        _sample_attend_kernel,
        out_shape=jax.ShapeDtypeStruct((nb, hq, KV_W), F32),
        grid_spec=pltpu.PrefetchScalarGridSpec(
            num_scalar_prefetch=1, grid=(nb, n_kv + 1),
            in_specs=[pl.BlockSpec((1, hq, KV_W), lambda b, j, pt: (b, 0, 0)),
                      pl.BlockSpec((1, _QPAD, kv_w), lambda b, j, pt: (b, 0, jnp.minimum(j, new_blk))),
                      pl.BlockSpec((1, KV_W, PAGE_SIZE), lambda b, j, pt: (b, 0, 0)),
                      pl.BlockSpec((1, KV_W, PAGE_SIZE), lambda b, j, pt: (b, 0, 0))]
                     + [page_spec(KV_W, _PAGES_KV, r) for r in range(_PAGES_KV)] * 2,
            out_specs=pl.BlockSpec((1, hq, KV_W), lambda b, j, pt: (b, 0, 0)),
            scratch_shapes=[pltpu.VMEM((hq, 1), F32), pltpu.VMEM((hq, 1), F32), pltpu.VMEM((hq, KV_W), F32)]),
        compiler_params=_params("parallel", "arbitrary"),
        name="sample_attend",
    )(page_table, qx, bias, knt, vnt, *([k_pool] * _PAGES_KV), *([v_pool] * _PAGES_KV))
    o = o.reshape(nb, N_KV_HEADS, Q_PER_KV, _QPAD, N_KV_HEADS, HEAD_DIM)[:, :, :, :s]
    o = jnp.einsum("bgjqkd,gk->bqgjd", o, jnp.eye(N_KV_HEADS, dtype=o.dtype))
    return o.reshape(nb * s, ATTN_W)


_ROUTE_W = 8


def _merge_kernel(l_ref, x_ref, yc_ref, o_ref, g_ref, wao_ref, wout_ref, lg_ref, lb_ref, rwh_ref, rwl_ref,
                  rb_ref, x1_ref, x1b_ref, route_ref, *, alpha):
    y_attn = jnp.dot(o_ref[...].astype(BF16), wao_ref[...], preferred_element_type=F32)
    mix_in = g_ref[:, :D_MODEL] * yc_ref[...] + g_ref[:, D_MODEL:] * y_attn
    mix = jnp.dot(mix_in.astype(BF16), wout_ref[...], preferred_element_type=F32)
    x1 = _layer_norm(alpha * x_ref[...] + mix, lg_ref[...], lb_ref[...])
    x1_ref[...] = x1
    hi = x1.astype(BF16)
    x1b_ref[...] = hi

    lo = (x1 - hi.astype(F32)).astype(BF16)
    logits = (jnp.dot(hi, rwh_ref[...], preferred_element_type=F32)
              + jnp.dot(hi, rwl_ref[...], preferred_element_type=F32)
              + jnp.dot(lo, rwh_ref[...], preferred_element_type=F32)) + rb_ref[...]
    lane = lax.broadcasted_iota(I32, (1, LANES), 1)

    def first_max(v):
        m = jnp.max(v, axis=1, keepdims=True)
        return m, jnp.min(jnp.where(v == m, lane, LANES), axis=1, keepdims=True)

    is_group = (lane >= N_EXPERTS) & (lane < N_EXPERTS + N_GROUPS)
    gl = jnp.where(is_group, logits, -jnp.inf)
    gm, gidx = first_max(gl)
    p_grp = 1.0 / jnp.sum(jnp.exp(gl - gm), axis=1, keepdims=True)
    grp = gidx - N_EXPERTS
    group_of_lane = lane >> (EXP_PER_GROUP.bit_length() - 1)
    el = jnp.where(group_of_lane == grp, logits, -jnp.inf)
    e1, i1 = first_max(el)
    e2, i2 = first_max(jnp.where(lane == i1, -jnp.inf, el))
    t = jnp.exp(e2 - e1)
    g1 = p_grp / (1.0 + t)
    g2 = g1 * t
    rl = lax.broadcasted_iota(I32, (1, _ROUTE_W), 1)
    route_ref[...] = jnp.where(rl == 0, i1.astype(F32), jnp.where(rl == 1, i2.astype(F32),
                               jnp.where(rl == 2, g1, jnp.where(rl == 3, g2, 0.0))))


def _merge(lidx, x2d, yc, o, g, p, tm, alpha):
    t = x2d.shape[0]
    row = lambda n: pl.BlockSpec((tm, n), lambda i, l: (i, 0))
    return pl.pallas_call(
        functools.partial(_merge_kernel, alpha=alpha),
        out_shape=[jax.ShapeDtypeStruct((t, D_MODEL), F32), jax.ShapeDtypeStruct((t, D_MODEL), BF16),
                   jax.ShapeDtypeStruct((t, _ROUTE_W), F32)],
        grid_spec=pltpu.PrefetchScalarGridSpec(
            num_scalar_prefetch=1, grid=(t // tm,),
            in_specs=[row(D_MODEL), row(D_MODEL), row(ATTN_W), row(2 * D_MODEL),
                      _lspec(ATTN_W, D_MODEL), _lspec(D_MODEL, D_MODEL), _lspec(1, D_MODEL),
                      _lspec(1, D_MODEL), _lspec(D_MODEL, LANES), _lspec(D_MODEL, LANES), _lspec(1, LANES)],
            out_specs=[row(D_MODEL), row(D_MODEL), row(_ROUTE_W)]),
        compiler_params=_params("parallel"),
        name="merge_ln1_router",
    )(lidx, x2d, yc, o, g, p["w_attn_out"], p["w_out"], p["ln1_g"], p["ln1_b"], p["router_hi"],
      p["router_lo"], p["router_b"])


_SLOT_ALIGN = SUBLANES_BF16
_ROW_BLK = 128
_EXPERTS_PER_STEP = 4
_MOE_CHUNK = 256
_META_W = LANES
_MOE_TILE_MAX = 256 * _SLOT_ALIGN


def _moe_plan_kernel(route_ref, dcol_ref, drow_ref, meta_ref):
    tm = route_ref.shape[0]
    blk_rows = min(_MOE_CHUNK, tm)
    lane = lax.broadcasted_iota(I32, (1, LANES), 1).astype(F32)
    hit0 = lane == route_ref[:, 0:1]
    hit1 = lane == route_ref[:, 1:2]
    onehot = jnp.where(hit0, 1.0, jnp.where(hit1, 1.0, 0.0))
    r = lax.broadcasted_iota(I32, (blk_rows, blk_rows), 0)
    c = lax.broadcasted_iota(I32, (blk_rows, blk_rows), 1)
    tri = jnp.where(c < r, 1.0, 0.0).astype(BF16)
    carry = jnp.zeros((1, LANES), F32)
    prefix = []
    for b0 in range(0, tm, blk_rows):
        blk = onehot[b0:b0 + blk_rows]
        prefix.append(jnp.dot(tri, blk.astype(BF16), preferred_element_type=F32) + carry)
        carry = carry + jnp.sum(blk, axis=0, keepdims=True)
    prefix = jnp.concatenate(prefix, axis=0)
    units = jnp.ceil(carry * (1.0 / _SLOT_ALIGN))
    rr = lax.broadcasted_iota(I32, (LANES, LANES), 0)
    cc = lax.broadcasted_iota(I32, (LANES, LANES), 1)
    upper = jnp.where(rr < cc, 1.0, 0.0).astype(BF16)
    units8 = jnp.broadcast_to(units, (8, LANES)).astype(BF16)
    offs = jnp.dot(units8, upper, preferred_element_type=F32)[0:1] * _SLOT_ALIGN
    slot = offs + prefix
    d0 = jnp.sum(jnp.where(hit0, slot, 0.0), axis=1, keepdims=True)
    d1 = jnp.sum(jnp.where(hit1, slot, 0.0), axis=1, keepdims=True)
    rl = lax.broadcasted_iota(I32, (1, LANES), 1)
    rec = jnp.where(rl == 0, d0, jnp.where(rl == 1, d1, jnp.where(rl == 2, route_ref[:, 2:3],
                    jnp.where(rl == 3, route_ref[:, 3:4], 0.0))))
    dcol_ref[...] = rec[:, :_ROUTE_W]
    drow_ref[...] = rec.T[:_ROUTE_W, :]
    nblk = jnp.ceil(units * (_SLOT_ALIGN / _ROW_BLK))
    shift = jnp.where(cc == rr + N_EXPERTS, 1.0, 0.0).astype(BF16)
    nblk_sh = jnp.dot(jnp.broadcast_to(nblk, (8, LANES)).astype(BF16), shift,
                      preferred_element_type=F32)[0:1]
    meta_ref[0] = jnp.where(rl < N_EXPERTS, offs, nblk_sh).astype(I32)


def _moe_plan(route, tm):
    t = route.shape[0]
    nt = t // tm
    assert tm <= _MOE_TILE_MAX
    return pl.pallas_call(
        _moe_plan_kernel,
        out_shape=[jax.ShapeDtypeStruct((t, _ROUTE_W), F32), jax.ShapeDtypeStruct((_ROUTE_W, t), F32),
                   jax.ShapeDtypeStruct((nt, 1, _META_W), I32)],
        grid=(nt,),
        in_specs=[pl.BlockSpec((tm, _ROUTE_W), lambda i: (i, 0))],
        out_specs=[pl.BlockSpec((tm, _ROUTE_W), lambda i: (i, 0)),
                   pl.BlockSpec((_ROUTE_W, tm), lambda i: (0, i)),
                   pl.BlockSpec((1, 1, _META_W), lambda i: (i, 0, 0))],
        compiler_params=_params("parallel"),
        name="moe_plan",
    )(route)


def _n_slots(tm):
    n = 2 * tm + N_EXPERTS * (_SLOT_ALIGN - 1) + _ROW_BLK
    return -(-n // _MOE_CHUNK) * _MOE_CHUNK


def _moe_kernel(meta_ref, l_ref, x1b_ref, x1_ref, drow_ref, dcol_ref, wg_ref, wu_ref, wd_ref, lg_ref, lb_ref,
                x2_ref, xb_ref, yb_ref, gs_ref, *, alpha):
    i, e = pl.program_id(0), pl.program_id(1)
    tm = x1_ref.shape[0]
    n_slots = xb_ref.shape[0]

    @pl.when(e == 0)
    def _dispatch():
        d0, d1 = drow_ref[0:1, :], drow_ref[1:2, :]
        g0, g1 = drow_ref[2:3, :], drow_ref[3:4, :]
        for s0 in range(0, n_slots, _MOE_CHUNK):
            sl = (s0 + lax.broadcasted_iota(I32, (_MOE_CHUNK, 1), 0)).astype(F32)
            a, b = sl == d0, sl == d1
            p = jnp.where(a, 1.0, jnp.where(b, 1.0, 0.0)).astype(BF16)
            xb_ref[s0:s0 + _MOE_CHUNK, :] = jnp.dot(p, x1b_ref[...], preferred_element_type=F32).astype(BF16)
            gs_ref[s0:s0 + _MOE_CHUNK, :] = jnp.sum(jnp.where(a, g0, jnp.where(b, g1, 0.0)), axis=1,
                                                    keepdims=True)
        yb_ref[...] = jnp.zeros(yb_ref.shape, BF16)

    for k in range(_EXPERTS_PER_STEP):
        expert = e * _EXPERTS_PER_STEP + k
        off = meta_ref[i * _META_W + expert]
        nblk = meta_ref[i * _META_W + N_EXPERTS + expert]

        def block(r, carry, k=k, off=off):
            start = pl.multiple_of(off + r * _ROW_BLK, _SLOT_ALIGN)
            xs = xb_ref[pl.ds(start, _ROW_BLK), :]
            hg = jnp.dot(xs, wg_ref[k], preferred_element_type=F32)
            hu = jnp.dot(xs, wu_ref[k], preferred_element_type=F32)
            h = (hg * jax.nn.sigmoid(hg)) * hu
            y = jnp.dot(h.astype(BF16), wd_ref[k], preferred_element_type=F32)
            yb_ref[pl.ds(start, _ROW_BLK), :] = (y * gs_ref[pl.ds(start, _ROW_BLK), :]).astype(BF16)
            return carry

        lax.fori_loop(0, nblk, block, 0)

    @pl.when(e == pl.num_programs(1) - 1)
    def _combine():
        lane = lax.broadcasted_iota(I32, (1, n_slots), 1).astype(F32)
        rows = min(_MOE_CHUNK, tm)
        for t0 in range(0, tm, rows):
            d0, d1 = dcol_ref[t0:t0 + rows, 0:1], dcol_ref[t0:t0 + rows, 1:2]
            qm = jnp.where(lane == d0, 1.0, jnp.where(lane == d1, 1.0, 0.0)).astype(BF16)
            y = jnp.dot(qm, yb_ref[...], preferred_element_type=F32)
            z = alpha * x1_ref[t0:t0 + rows, :] + y
            x2_ref[t0:t0 + rows, :] = _layer_norm(z, lg_ref[...], lb_ref[...])


def _moe(lidx, x1, x1b, drow, dcol, meta, p, tm, alpha):
    t = x1.shape[0]
    n_slots = _n_slots(tm)
    tile = lambda shape, imap: pl.BlockSpec(shape, imap)
    return pl.pallas_call(
        functools.partial(_moe_kernel, alpha=alpha),
        out_shape=jax.ShapeDtypeStruct((t, D_MODEL), F32),
        grid_spec=pltpu.PrefetchScalarGridSpec(
            num_scalar_prefetch=2, grid=(t // tm, N_EXPERTS // _EXPERTS_PER_STEP),
            in_specs=[tile((tm, D_MODEL), lambda i, e, m, l: (i, 0)),
                      tile((tm, D_MODEL), lambda i, e, m, l: (i, 0)),
                      tile((_ROUTE_W, tm), lambda i, e, m, l: (0, i)),
                      tile((tm, _ROUTE_W), lambda i, e, m, l: (i, 0)),
                      tile((None, _EXPERTS_PER_STEP, D_MODEL, D_EXPERT), lambda i, e, m, l: (l[0], e, 0, 0)),
                      tile((None, _EXPERTS_PER_STEP, D_MODEL, D_EXPERT), lambda i, e, m, l: (l[0], e, 0, 0)),
                      tile((None, _EXPERTS_PER_STEP, D_EXPERT, D_MODEL), lambda i, e, m, l: (l[0], e, 0, 0)),
                      _lspec(1, D_MODEL), _lspec(1, D_MODEL)],
            out_specs=tile((tm, D_MODEL), lambda i, e, m, l: (i, 0)),
            scratch_shapes=[pltpu.VMEM((n_slots, D_MODEL), BF16), pltpu.VMEM((n_slots, D_MODEL), BF16),
                            pltpu.VMEM((n_slots, 1), F32)]),
        compiler_params=_params("parallel", "arbitrary"),
        name="moe_experts_ln2",
    )(meta.reshape(-1), lidx, x1b, x1, drow, dcol, p["w_gate"], p["w_up"], p["w_down"], p["ln2_g"], p["ln2_b"])


def _token_tile(t, want):
    return want if t % want == 0 else t


def _decoder_layer(lidx, x, prefix, attend, p, alpha):
    nb, s, _ = x.shape
    t = nb * s
    x2d = x.reshape(t, D_MODEL)
    u, q_b, k, v, qi_b, ki, wi, g, k_b, v_b, ki_b = _inproj(lidx, x2d, p["w_in"], p["b_in"], _token_tile(t, 512))

    ctm = 512 if s % 512 == 0 else _HALO
    u3 = u.reshape(nb, s, C_CONV)
    history = jnp.pad(prefix, ((0, 0), (_HALO - (CONV_W - 1), 0), (0, 0)))
    u_rows = jnp.pad(u3, ((0, 0), (0, (-s) % ctm), (0, 0)))
    yc = _conv_branch(lidx, history, u_rows, p["conv_w"], p["conv_b"], p["conv_ln_g"], p["conv_ln_b"],
                      p["w_conv_out"], ctm)
    yc = yc[:, :s].reshape(t, D_MODEL)

    o = attend(q_b, k_b, v_b, qi_b, ki_b, wi)
    x1, x1b, route = _merge(lidx, x2d, yc, o, g, p, _token_tile(t, 256), alpha)
    mtm = _token_tile(t, 1024)
    dcol, drow, meta = _moe_plan(route, mtm)
    x2 = _moe(lidx, x1, x1b, drow, dcol, meta, p, mtm, alpha)

    new_conv = jnp.concatenate([prefix, u3], axis=1)[:, -(CONV_W - 1):]
    return (x2.reshape(nb, s, D_MODEL), k.reshape(nb, s, N_KV_HEADS, HEAD_DIM),
            v.reshape(nb, s, N_KV_HEADS, HEAD_DIM), ki.reshape(nb, s, IDX_DIM), new_conv)


def _prepare_params(w_in, b_in, conv_w, conv_b, conv_ln_g, conv_ln_b, w_conv_out, w_attn_out, w_out,
                    ln1_g, ln1_b, router_group_w, router_group_b, router_expert_w, router_expert_b,
                    w_gate, w_up, w_down, ln2_g, ln2_b):
    depth = w_in.shape[0]
    pad = _COL_G - _N_IN_HEAD
    w_pad = jnp.concatenate([w_in[..., :_N_IN_HEAD], jnp.zeros((depth, D_MODEL, pad), F32),
                             w_in[..., _N_IN_HEAD:]], axis=-1).astype(BF16)
    b_pad = jnp.concatenate([b_in[..., :_N_IN_HEAD], jnp.zeros((depth, pad), F32),
                             b_in[..., _N_IN_HEAD:]], axis=-1)[:, None, :]
    rpad = LANES - N_EXPERTS - N_GROUPS
    rw = jnp.concatenate([router_expert_w, router_group_w, jnp.zeros((depth, D_MODEL, rpad), F32)], axis=-1)
    rb = jnp.concatenate([router_expert_b, router_group_b, jnp.zeros((depth, rpad), F32)], axis=-1)
    rw_hi = rw.astype(BF16)
    rw_lo = (rw - rw_hi.astype(F32)).astype(BF16)
    vec = lambda a: a[:, None, :]
    return dict(w_in=w_pad, b_in=b_pad, conv_w=conv_w, conv_b=vec(conv_b), conv_ln_g=vec(conv_ln_g),
                conv_ln_b=vec(conv_ln_b), w_conv_out=w_conv_out.astype(BF16),
                w_attn_out=w_attn_out.astype(BF16), w_out=w_out.astype(BF16), ln1_g=vec(ln1_g),
                ln1_b=vec(ln1_b), router_hi=rw_hi, router_lo=rw_lo, router_b=vec(rb),
                w_gate=w_gate.astype(BF16), w_up=w_up.astype(BF16), w_down=w_down.astype(BF16),
                ln2_g=vec(ln2_g), ln2_b=vec(ln2_b))


def kernel(x_prompt, x_sample, cache_k, cache_v, cache_kidx, state_conv, page_table, w_in, b_in, conv_w,
           conv_b, conv_ln_g, conv_ln_b, w_conv_out, w_attn_out, w_out, ln1_g, ln1_b, router_group_w,
           router_group_b, router_expert_w, router_expert_b, w_gate, w_up, w_down, ln2_g, ln2_b):
    params = _prepare_params(w_in, b_in, conv_w, conv_b, conv_ln_g, conv_ln_b, w_conv_out, w_attn_out,
                             w_out, ln1_g, ln1_b, router_group_w, router_group_b, router_expert_w,
                             router_expert_b, w_gate, w_up, w_down, ln2_g, ln2_b)
    nb, s, _ = x_prompt.shape
    db, ds, _ = x_sample.shape
    depth, n_phys = cache_k.shape[:2]
    alpha = (2 * depth) ** 0.25
    k_pool = cache_k.transpose(0, 1, 3, 4, 2).reshape(depth * n_phys, KV_W, PAGE_SIZE)
    v_pool = cache_v.transpose(0, 1, 3, 4, 2).reshape(depth * n_phys, KV_W, PAGE_SIZE)
    ki_pool = cache_kidx.transpose(0, 1, 3, 2).reshape(depth * n_phys, IDX_DIM, PAGE_SIZE)
    conv_zero = jnp.zeros((nb, CONV_W - 1, C_CONV), F32)

    def layer(carry, xs):
        xp, xsm = carry
        l, st = xs
        lidx = l.reshape(1)
        prompt_attend = functools.partial(_dsa_prompt, nb=nb, s=s)
        xp, *new_p = _decoder_layer(lidx, xp, conv_zero, prompt_attend, params, alpha)
        sample_attend = functools.partial(_dsa_sample, k_pool=k_pool, v_pool=v_pool, ki_pool=ki_pool,
                                          page_table=page_table + l * n_phys, nb=db, s=ds)
        xsm, *new_s = _decoder_layer(lidx, xsm, st, sample_attend, params, alpha)
        return (xp, xsm), (tuple(new_p), tuple(new_s))

    (xp, xsm), (new_p, new_s) = lax.scan(layer, (x_prompt, x_sample),
                                         (jnp.arange(depth, dtype=I32), state_conv))
    return (xp, xsm, *new_p, *new_s)
```

```python
import functools

import jax
import jax.numpy as jnp
from jax import lax
from jax.experimental import pallas as pl
from jax.experimental.pallas import tpu as pltpu

F32 = jnp.float32
BF16 = jnp.bfloat16
I32 = jnp.int32

D_MODEL = 1024
PAGE_SIZE = 128
C_CONV = 512
CONV_W = 31
N_HEADS = 8
N_KV_HEADS = 4
HEAD_DIM = 64
Q_PER_KV = N_HEADS // N_KV_HEADS
ATTN_W = N_HEADS * HEAD_DIM
KV_W = N_KV_HEADS * HEAD_DIM
N_IDX_HEADS = 8
IDX_DIM = 64
TOPK_MAX = 256
N_GROUPS = 4
EXP_PER_GROUP = 8
N_EXPERTS = N_GROUPS * EXP_PER_GROUP
D_EXPERT = 256
LN_EPS = 1e-5

LANES = 128
_SUBLANES = 8
SUBLANES_BF16 = 16
VMEM_LIMIT = 56 * 1024 * 1024

_N_SMALL = IDX_DIM + N_IDX_HEADS
_COL_CA = 0
_COL_CB = _COL_CA + C_CONV
_COL_Q = _COL_CB + C_CONV
_COL_K = _COL_Q + ATTN_W
_COL_V = _COL_K + KV_W
_COL_QI = _COL_V + KV_W
_COL_KI = _COL_QI + N_IDX_HEADS * IDX_DIM
_COL_G = _COL_KI + LANES
_N_IN_PAD = _COL_G + 2 * D_MODEL
_N_IN_HEAD = _COL_KI + _N_SMALL

_NT = (((1,), (1,)), ((), ()))
_NEG = -1e30
_LOG2E = 1.4426950408889634


def _params(*sem):
    return pltpu.CompilerParams(dimension_semantics=sem, vmem_limit_bytes=VMEM_LIMIT)


def _lspec(*shape):
    nd = len(shape)
    return pl.BlockSpec((None,) + shape, lambda *a: (a[-1][0],) + (0,) * nd)


def _layer_norm(x, g, b):
    mu = jnp.mean(x, axis=-1, keepdims=True)
    xc = x - mu
    var = jnp.mean(xc * xc, axis=-1, keepdims=True)
    return xc * lax.rsqrt(var + LN_EPS) * g + b


def _inproj_kernel(l_ref, x_ref, w_ref, b_ref, u_ref, q_ref, k_ref, v_ref, qi_ref, ki_ref, wi_ref, g_ref,
                   kb_ref, vb_ref, kib_ref):
    xb = x_ref[...].astype(BF16)

    def proj(c0, n):
        return jnp.dot(xb, w_ref[:, c0:c0 + n], preferred_element_type=F32) + b_ref[:, c0:c0 + n]

    u_ref[...] = proj(_COL_CA, C_CONV) * jax.nn.sigmoid(proj(_COL_CB, C_CONV))
    q_ref[...] = (proj(_COL_Q, ATTN_W) * (HEAD_DIM ** -0.5 * _LOG2E)).astype(BF16)
    k = proj(_COL_K, KV_W)
    k_ref[...] = k
    kb_ref[...] = k.astype(BF16)
    v = proj(_COL_V, KV_W)
    v_ref[...] = v
    vb_ref[...] = v.astype(BF16)
    qi_ref[...] = (proj(_COL_QI, N_IDX_HEADS * IDX_DIM) * (IDX_DIM ** -0.5)).astype(BF16)
    small = proj(_COL_KI, LANES)
    ki = small[:, :IDX_DIM]
    ki_ref[...] = ki
    kib_ref[...] = ki.astype(BF16)
    wi_ref[...] = small[:, IDX_DIM:_N_SMALL] * (N_IDX_HEADS ** -0.5)
    g_ref[...] = jax.nn.sigmoid(proj(_COL_G, 2 * D_MODEL))


def _inproj(lidx, x2d, w_pad, b_pad, tm):
    t = x2d.shape[0]
    widths = [(C_CONV, F32), (ATTN_W, BF16), (KV_W, F32), (KV_W, F32), (N_IDX_HEADS * IDX_DIM, BF16),
              (IDX_DIM, F32), (N_IDX_HEADS, F32), (2 * D_MODEL, F32), (KV_W, BF16), (KV_W, BF16),
              (IDX_DIM, BF16)]
    return pl.pallas_call(
        _inproj_kernel,
        out_shape=[jax.ShapeDtypeStruct((t, n), dt) for n, dt in widths],
        grid_spec=pltpu.PrefetchScalarGridSpec(
            num_scalar_prefetch=1, grid=(t // tm,),
            in_specs=[pl.BlockSpec((tm, D_MODEL), lambda i, l: (i, 0)),
                      _lspec(D_MODEL, _N_IN_PAD), _lspec(1, _N_IN_PAD)],
            out_specs=[pl.BlockSpec((tm, n), lambda i, l: (i, 0)) for n, _ in widths]),
        compiler_params=_params("parallel"),
        name="inproj",
    )(lidx, x2d, w_pad, b_pad)


_HALO = 32
_CONV_ROWS = 32


def _conv_kernel(l_ref, first_ref, prev_ref, cur_ref, cw_ref, cb_ref, lg_ref, lb_ref, wo_ref, y_ref,
                 hist_ref, acc_ref, shift_ref):
    tm = cur_ref.shape[1]
    at_start = pl.program_id(1) == 0
    hist_ref[0:_HALO, :] = jnp.where(at_start, first_ref[0], prev_ref[0])
    hist_ref[_HALO:_HALO + tm, :] = cur_ref[0]
    first = _HALO - (CONV_W - 1)
    for b in range(_SUBLANES):
        rows = tm + _SUBLANES * ((CONV_W - 1 - b) // _SUBLANES)
        shift_ref[b, 0:rows, :] = hist_ref[first + b:first + b + rows, :]
    for r0 in range(0, tm, _CONV_ROWS):
        acc = jnp.zeros((_CONV_ROWS, C_CONV), F32)
        for j in range(CONV_W):
            a, b = divmod(j, _SUBLANES)
            acc = acc + cw_ref[j:j + 1, :] * shift_ref[b, r0 + a * _SUBLANES:r0 + a * _SUBLANES + _CONV_ROWS, :]
        acc_ref[r0:r0 + _CONV_ROWS, :] = acc + cb_ref[...]
    y = _layer_norm(acc_ref[...], lg_ref[...], lb_ref[...])
    y = y * jax.nn.sigmoid(y)
    y_ref[0] = jnp.dot(y.astype(BF16), wo_ref[...], preferred_element_type=F32)


def _conv_branch(lidx, history, u, conv_w, conv_b, ln_g, ln_b, w_out_bf, tm):
    nb, length, _ = u.shape
    halo_blocks = tm // _HALO
    return pl.pallas_call(
        _conv_kernel,
        out_shape=jax.ShapeDtypeStruct((nb, length, D_MODEL), F32),
        grid_spec=pltpu.PrefetchScalarGridSpec(
            num_scalar_prefetch=1, grid=(nb, length // tm),
            in_specs=[pl.BlockSpec((1, _HALO, C_CONV), lambda b, i, l: (b, 0, 0)),
                      pl.BlockSpec((1, _HALO, C_CONV), lambda b, i, l: (b, jnp.maximum(i * halo_blocks - 1, 0), 0)),
                      pl.BlockSpec((1, tm, C_CONV), lambda b, i, l: (b, i, 0)),
                      _lspec(CONV_W, C_CONV), _lspec(1, C_CONV), _lspec(1, C_CONV), _lspec(1, C_CONV),
                      _lspec(C_CONV, D_MODEL)],
            out_specs=pl.BlockSpec((1, tm, D_MODEL), lambda b, i, l: (b, i, 0)),
            scratch_shapes=[pltpu.VMEM((_HALO + tm, C_CONV), F32), pltpu.VMEM((tm, C_CONV), F32),
                            pltpu.VMEM((_SUBLANES, tm + _HALO - _SUBLANES, C_CONV), F32)]),
        compiler_params=_params("parallel", "parallel"),
        name="conv_branch",
    )(lidx, history, u, u, conv_w, conv_b, ln_g, ln_b, w_out_bf)


_BISECT_STEPS = 15


_FOLD_ROWS = 64


def _reduce_rows(x, op):
    rows, cols = x.shape
    if rows > _FOLD_ROWS:
        x = op(x.reshape(rows // _FOLD_ROWS, _FOLD_ROWS, cols), axis=0)
    x = op(x.reshape(x.shape[0] // _SUBLANES, _SUBLANES, cols), axis=0)
    return op(x, axis=0, keepdims=True)


def _fold_keys(sc_ref, n_chunks, tk, init, fn, op, merge, first=0):
    def body(c, acc):
        off = pl.multiple_of(c * tk, tk)
        v = fn(sc_ref[pl.ds(off, tk), :], off).reshape(tk // _FOLD_ROWS, _FOLD_ROWS, cols)
        return merge(acc, op(v, axis=0))

    cols = sc_ref.shape[1]
    acc = lax.fori_loop(first, n_chunks, body, jnp.full((_FOLD_ROWS, cols), init, F32))
    return _reduce_rows(acc, op)


def _count_keys(sc_ref, n_chunks, tk, fn, first=0):
    return _fold_keys(sc_ref, n_chunks, tk, 0.0, fn, jnp.sum, jnp.add, first)


def _count_ge(sc_ref, n_chunks, tk, thr):
    return _count_keys(sc_ref, n_chunks, tk, lambda blk, off: jnp.where(blk >= thr, 1.0, 0.0))


def _any(flag):
    return jnp.max(jnp.where(flag, 1, 0))


def _select_threshold(sc_ref, st_ref, n_chunks, tk, kt, s_min, s_max, n_valid):
    lo_ref, hi_ref, cl_ref, done_ref, jlo_ref, jhi_ref = (st_ref.at[n:n + 1] for n in range(6))
    c_max = _count_ge(sc_ref, n_chunks, tk, s_max)
    top = c_max >= kt
    lo_ref[...] = jnp.where(top, s_max, s_min)
    hi_ref[...] = s_max
    cl0 = jnp.where(top, c_max, n_valid)
    cl_ref[...] = cl0
    done0 = jnp.where(top, 1.0, jnp.where(cl0 == kt, 1.0, 0.0))
    done_ref[...] = done0

    def bisect_step():
        lo, hi, cl, done = lo_ref[...], hi_ref[...], cl_ref[...], done_ref[...]
        mid = 0.5 * lo + 0.5 * hi
        c = _count_ge(sc_ref, n_chunks, tk, mid)
        act = done == 0.0
        up = c >= kt
        lo_ref[...] = jnp.where(act, jnp.where(up, mid, lo), lo)
        cl_ref[...] = jnp.where(act, jnp.where(up, c, cl), cl)
        hi_ref[...] = jnp.where(act, jnp.where(up, hi, mid), hi)
        done_ref[...] = jnp.where(act, jnp.where(c == kt, 1.0, 0.0), done)

    @pl.when(_any(done0 == 0.0) > 0)
    def _narrow():
        lax.fori_loop(0, _BISECT_STEPS, lambda _, carry: (bisect_step(), carry)[1], 0)

    def snap(_):
        lo, hi, cl, done = lo_ref[...], hi_ref[...], cl_ref[...], done_ref[...]
        t1 = _fold_keys(sc_ref, n_chunks, tk, -jnp.inf, lambda blk, off: jnp.where(blk < hi, blk, -jnp.inf),
                        jnp.max, jnp.maximum)
        c1 = _count_ge(sc_ref, n_chunks, tk, t1)
        act = done == 0.0
        found = c1 >= kt
        lo_ref[...] = jnp.where(act, jnp.where(found, t1, lo), lo)
        cl_ref[...] = jnp.where(act, jnp.where(found, c1, cl), cl)
        hi_ref[...] = jnp.where(act, jnp.where(found, hi, t1), hi)
        done_new = jnp.where(act, jnp.where(found, 1.0, 0.0), done)
        done_ref[...] = done_new
        return _any(done_new == 0.0)

    lax.while_loop(lambda f: f > 0, snap, _any(done_ref[...] == 0.0))

    thr = lo_ref[...]

    @pl.when(_any(cl_ref[...] != kt) > 0)
    def _cut_ties():
        tied = cl_ref[...] != kt
        need = kt - _count_keys(sc_ref, n_chunks, tk, lambda blk, off: jnp.where(blk > thr, 1.0, 0.0))
        row = lax.broadcasted_iota(I32, (tk, 1), 0)

        def locate(c, carry):
            seen, chunk, seen_before = carry
            off = pl.multiple_of(c * tk, tk)
            cnt = _reduce_rows(jnp.where(sc_ref[pl.ds(off, tk), :] == thr, 1.0, 0.0), jnp.sum)
            here = jnp.where(seen < need, jnp.where(seen + cnt >= need, 1.0, 0.0), 0.0)
            chunk = jnp.where(here > 0.0, c.astype(F32), chunk)
            seen_before = jnp.where(here > 0.0, seen, seen_before)
            return seen + cnt, chunk, seen_before

        zero = jnp.zeros_like(thr)
        _, chunk, seen_before = lax.fori_loop(0, n_chunks, locate, (zero, zero, zero))
        first_key = chunk * float(tk)
        need_here = need - seen_before
        c_first = jnp.min(jnp.where(tied, chunk, float(sc_ref.shape[0]))).astype(I32)
        c_last = jnp.max(jnp.where(tied, chunk, -1.0)).astype(I32)

        jlo_ref[...] = first_key - 1.0
        jhi_ref[...] = first_key + float(tk - 1)

        def step(_, carry):
            jlo, jhi = jlo_ref[...], jhi_ref[...]
            mid = jnp.floor(0.5 * (jlo + jhi))

            def in_range(blk, off):
                idx = (off + row).astype(F32)
                return jnp.where(blk == thr, jnp.where(idx <= mid, jnp.where(idx >= first_key, 1.0, 0.0), 0.0), 0.0)

            ok = _count_keys(sc_ref, c_last + 1, tk, in_range, c_first) >= need_here
            jhi_ref[...] = jnp.where(ok, mid, jhi)
            jlo_ref[...] = jnp.where(ok, jlo, mid)
            return carry

        lax.fori_loop(0, tk.bit_length() - 1, step, 0)
        cut = jnp.where(tied, jhi_ref[...], float(sc_ref.shape[0]))

        def demote(c, carry):
            off = pl.multiple_of(c * tk, tk)
            blk = sc_ref[pl.ds(off, tk), :]
            beyond = (off + row).astype(F32) > cut
            sc_ref[pl.ds(off, tk), :] = jnp.where(blk == thr, jnp.where(beyond, -jnp.inf, blk), blk)
            return carry

        lax.fori_loop(c_first, n_chunks, demote, 0)

    return thr


def _min_max_init(cols):
    return (jnp.full((_FOLD_ROWS, cols), jnp.inf, F32), jnp.full((_FOLD_ROWS, cols), -jnp.inf, F32))


def _min_max_update(mn, mx, score, valid):
    shape = (score.shape[0] // _FOLD_ROWS, _FOLD_ROWS, score.shape[1])
    mn = jnp.minimum(mn, jnp.min(jnp.where(valid, score, jnp.inf).reshape(shape), axis=0))
    mx = jnp.maximum(mx, jnp.max(jnp.where(valid, score, -jnp.inf).reshape(shape), axis=0))
    return mn, mx


_TQ = 256
_TK = 512


_IDX_PAIRS = N_IDX_HEADS // 2
_ONES_ROWS = SUBLANES_BF16


def _dsa_prompt_kernel(qi_ref, wi_ref, ki_ref, q_ref, k_ref, vt_ref, o_ref,
                       sc_ref, st_ref, m_ref, acc_ref, s_ref, cm_ref, p_ref, al_ref):
    tq, tk = _TQ, _TK
    i = pl.program_id(1)
    n_chunks = ((i + 1) * tq + tk - 1) // tk
    qpos = i * tq + lax.broadcasted_iota(I32, (1, tq), 1)
    krow = lax.broadcasted_iota(I32, (tk, 1), 0)

    def scores(c, carry):
        off = pl.multiple_of(c * tk, tk)
        kc = ki_ref[0, pl.ds(off, tk), :]
        score = None
        for p in range(_IDX_PAIRS):
            d = lax.dot_general(kc, qi_ref[0, 0, p], _NT, preferred_element_type=F32)
            for j in range(2):
                h = 2 * p + j
                t = wi_ref[0, h:h + 1, :] * jnp.maximum(d[:, j * tq:(j + 1) * tq], 0.0)
                score = t if score is None else score + t
        score = jnp.where(score == 0.0, 0.0, score)
        valid = off + krow <= qpos
        sc_ref[pl.ds(off, tk), :] = jnp.where(valid, score, -jnp.inf)
        return _min_max_update(*carry, score, valid)

    mn, mx = lax.fori_loop(0, n_chunks, scores, _min_max_init(tq))
    n_valid = (qpos + 1).astype(F32)
    kt = jnp.minimum(n_valid, float(TOPK_MAX))
    thr = _select_threshold(sc_ref, st_ref, n_chunks, tk, kt, _reduce_rows(mn, jnp.min),
                            _reduce_rows(mx, jnp.max), n_valid)

    m_ref[...] = jnp.full(m_ref.shape, _NEG, F32)
    acc_ref[...] = jnp.zeros(acc_ref.shape, F32)

    def mask_bias(c):
        off = pl.multiple_of(c * tk, tk)
        bias = jnp.where(sc_ref[pl.ds(off, tk), :] >= thr, 0.0, -jnp.inf)
        return jnp.concatenate([bias] * Q_PER_KV, axis=1)

    def qk_stage(c, slot, groups, bias=None):
        off = pl.multiple_of(c * tk, tk)
        bias = mask_bias(c) if bias is None else bias
        for g in groups:
            s = lax.dot_general(k_ref[0, g, pl.ds(off, tk), :], q_ref[0, 0, g], _NT,
                                preferred_element_type=F32) + bias
            s_ref[slot, g] = s
            cm_ref[slot, g] = _reduce_rows(s, jnp.max)

    def exp_stage(slot, groups):
        for g in groups:
            m_old = m_ref[g]
            m_new = jnp.maximum(m_old, cm_ref[slot, g])
            al_ref[slot, g] = jnp.exp2(m_old - m_new)
            p_ref[slot, g] = jnp.exp2(s_ref[slot, g] - m_new).astype(BF16)
            m_ref[g] = m_new

    def pv_stage(c, slot, groups):
        off = pl.multiple_of(c * tk, tk)
        for g in groups:
            acc_ref[g] = al_ref[slot, g] * acc_ref[g] + jnp.dot(vt_ref[0, g, :, pl.ds(off, tk)], p_ref[slot, g],
                                                                preferred_element_type=F32)

    every = range(N_KV_HEADS)

    def attend(c, carry):
        slot = c % 2
        for g in every:
            pv_stage(c, slot, (g,))
            qk_stage(c + 2, slot, (g,))
            exp_stage(1 - slot, (g,))
        return carry

    qk_stage(0, 0, every)
    exp_stage(0, every)

    @pl.when(n_chunks > 1)
    def _fill():
        qk_stage(1, 1, every)

    lax.fori_loop(0, n_chunks - 2, attend, 0)
    last = n_chunks - 1

    @pl.when(n_chunks > 1)
    def _drain():
        exp_stage(last % 2, every)
        pv_stage(last - 1, (last - 1) % 2, every)

    pv_stage(last, last % 2, every)
    for g in range(N_KV_HEADS):
        o_ref[0, 0, g] = acc_ref[g, 0:HEAD_DIM, :] / acc_ref[g, HEAD_DIM:HEAD_DIM + 1, :]


def _dsa_prompt(q_b, k_b, v_b, qi_b, ki_b, wi, nb, s):
    tq = _TQ
    nq = s // tq
    cols = Q_PER_KV * tq

    def pair_major(a, n_pairs):
        a = a.reshape(nb, nq, tq, n_pairs, 2, a.shape[-1] // (2 * n_pairs)).transpose(0, 1, 3, 4, 2, 5)
        return a.reshape(nb, nq, n_pairs, 2 * tq, -1)

    kh = k_b.reshape(nb, s, N_KV_HEADS, HEAD_DIM).transpose(0, 2, 1, 3)
    vt = v_b.reshape(nb, s, N_KV_HEADS, HEAD_DIM).transpose(0, 2, 3, 1)
    vt = jnp.concatenate([vt, jnp.ones((nb, N_KV_HEADS, _ONES_ROWS, s), vt.dtype)], axis=2)
    wit =wi.reshape(nb, s, N_IDX_HEADS).transpose(0, 2, 1)
    once = dict(pipeline_mode=pl.Buffered(1))
    o = pl.pallas_call(
        _dsa_prompt_kernel,
        out_shape=jax.ShapeDtypeStruct((nb, nq, N_KV_HEADS, HEAD_DIM, cols), F32),
        grid=(nb, nq),
        in_specs=[pl.BlockSpec((1, 1, _IDX_PAIRS, 2 * tq, IDX_DIM), lambda b, i: (b, i, 0, 0, 0)),
                  pl.BlockSpec((1, N_IDX_HEADS, tq), lambda b, i: (b, 0, i)),
                  pl.BlockSpec((1, s, IDX_DIM), lambda b, i: (b, 0, 0), **once),
                  pl.BlockSpec((1, 1, N_KV_HEADS, cols, HEAD_DIM), lambda b, i: (b, i, 0, 0, 0)),
                  pl.BlockSpec((1, N_KV_HEADS, s, HEAD_DIM), lambda b, i: (b, 0, 0, 0), **once),
                  pl.BlockSpec((1, N_KV_HEADS, HEAD_DIM + _ONES_ROWS, s), lambda b, i: (b, 0, 0, 0), **once)],
        out_specs=pl.BlockSpec((1, 1, N_KV_HEADS, HEAD_DIM, cols), lambda b, i: (b, i, 0, 0, 0)),
        scratch_shapes=[pltpu.VMEM((s, tq), F32), pltpu.VMEM((_SUBLANES, tq), F32),
                        pltpu.VMEM((N_KV_HEADS, 1, cols), F32),
                        pltpu.VMEM((N_KV_HEADS, HEAD_DIM + _ONES_ROWS, cols), F32),
                        pltpu.VMEM((2, N_KV_HEADS, _TK, cols), F32), pltpu.VMEM((2, N_KV_HEADS, 1, cols), F32),
                        pltpu.VMEM((2, N_KV_HEADS, _TK, cols), BF16), pltpu.VMEM((2, N_KV_HEADS, 1, cols), F32)],
        compiler_params=_params("parallel", "arbitrary"),
        name="dsa_prompt",
    )(pair_major(qi_b, _IDX_PAIRS), wit, ki_b.reshape(nb, s, IDX_DIM), pair_major(q_b, N_KV_HEADS), kh, vt)
    o = o.reshape(nb, nq, N_KV_HEADS, HEAD_DIM, Q_PER_KV, tq).transpose(0, 1, 5, 2, 4, 3)
    return o.reshape(nb * s, ATTN_W)


_QPAD = 8
_PAGES_SC = 32
_PAGES_KV = 16


def _sample_scores_kernel(pt_ref, qi_ref, wi_ref, knt_ref, *rest):
    pages, sc_ref = rest[:_PAGES_SC], rest[_PAGES_SC]
    j = pl.program_id(1)
    last = pl.num_programs(1) - 1

    def scores(kt):
        d = jnp.dot(qi_ref[0], kt, preferred_element_type=F32)
        score = None
        for h in range(N_IDX_HEADS):
            t = wi_ref[0, :, h:h + 1] * jnp.maximum(d[h * _QPAD:(h + 1) * _QPAD, :], 0.0)
            score = t if score is None else score + t
        return jnp.where(score == 0.0, 0.0, score)

    @pl.when(j < last)
    def _past():
        sc_ref[0] = scores(jnp.concatenate([pg[0] for pg in pages], axis=1).astype(BF16))

    @pl.when(j == last)
    def _new():
        sc_ref[0] = jnp.zeros(sc_ref.shape[1:], F32)
        sc_ref[0, :, 0:PAGE_SIZE] = scores(knt_ref[0])


def _sample_select_kernel(sc_in_ref, bias_ref, sc_ref, st_ref, *, past, dec_seq):
    tk = _TK
    n_chunks = sc_ref.shape[0] // tk
    q = lax.broadcasted_iota(I32, (1, LANES), 1) % dec_seq
    krow = lax.broadcasted_iota(I32, (tk, 1), 0)

    def load(c, carry):
        off = pl.multiple_of(c * tk, tk)
        s = sc_in_ref[pl.ds(off, tk), :]
        valid = off + krow <= past + q
        sc_ref[pl.ds(off, tk), :] = jnp.where(valid, s, -jnp.inf)
        return _min_max_update(*carry, s, valid)

    mn, mx = lax.fori_loop(0, n_chunks, load, _min_max_init(LANES))
    n_valid = (past + 1 + q).astype(F32)
    kt = jnp.minimum(n_valid, float(min(TOPK_MAX, (past + dec_seq) // 4)))
    thr = _select_threshold(sc_ref, st_ref, n_chunks, tk, kt, _reduce_rows(mn, jnp.min),
                            _reduce_rows(mx, jnp.max), n_valid)

    def emit(c, carry):
        off = pl.multiple_of(c * tk, tk)
        bias_ref[pl.ds(off, tk), :] = jnp.where(sc_ref[pl.ds(off, tk), :] >= thr, 0.0, -jnp.inf)
        return carry

    lax.fori_loop(0, n_chunks, emit, 0)


def _sample_attend_kernel(pt_ref, q_ref, bias_ref, knt_ref, vnt_ref, *rest):
    kpages, vpages = rest[:_PAGES_KV], rest[_PAGES_KV:2 * _PAGES_KV]
    o_ref, m_ref, l_ref, acc_ref = rest[2 * _PAGES_KV:]
    j = pl.program_id(1)
    last = pl.num_programs(1) - 1

    @pl.when(j == 0)
    def _init():
        m_ref[...] = jnp.full(m_ref.shape, _NEG, F32)
        l_ref[...] = jnp.zeros(l_ref.shape, F32)
        acc_ref[...] = jnp.zeros(acc_ref.shape, F32)

    def attend(kt, vt, bias):
        s = jnp.dot(q_ref[0], kt, preferred_element_type=F32)
        s = jnp.concatenate([s[h * _QPAD:(h + 1) * _QPAD] + bias for h in range(N_HEADS)], axis=0)
        m_old = m_ref[...]
        m_new = jnp.maximum(m_old, jnp.max(s, axis=1, keepdims=True))
        alpha = jnp.exp2(m_old - m_new)
        p = jnp.exp2(s - m_new)
        l_ref[...] = alpha * l_ref[...] + jnp.sum(p, axis=1, keepdims=True)
        acc_ref[...] = alpha * acc_ref[...] + lax.dot_general(p.astype(BF16), vt, _NT,
                                                              preferred_element_type=F32)
        m_ref[...] = m_new

    @pl.when(j < last)
    def _past():
        attend(jnp.concatenate([pg[0] for pg in kpages], axis=1).astype(BF16),
               jnp.concatenate([pg[0] for pg in vpages], axis=1).astype(BF16), bias_ref[0])

    @pl.when(j == last)
    def _new():
        attend(knt_ref[0], vnt_ref[0], bias_ref[0, :, 0:PAGE_SIZE])
        o_ref[0] = acc_ref[...] / l_ref[...]


def _dsa_sample(q_b, k_b, v_b, qi_b, ki_b, wi, k_pool, v_pool, ki_pool, page_table, nb, s):
    n_pages = page_table.shape[1]
    past = n_pages * PAGE_SIZE
    assert _QPAD % s == 0 and (nb * _QPAD) % LANES == 0
    assert n_pages % _PAGES_SC == 0 and n_pages % _PAGES_KV == 0
    dup = jnp.arange(_QPAD) % s
    qi = qi_b.reshape(nb, s, N_IDX_HEADS, IDX_DIM)[:, dup].transpose(0, 2, 1, 3)
    qi = qi.reshape(nb, N_IDX_HEADS * _QPAD, IDX_DIM)
    wi8 = wi.reshape(nb, s, N_IDX_HEADS)[:, dup]
    new_page = lambda a: jnp.pad(a.reshape(nb, s, -1).transpose(0, 2, 1), ((0, 0), (0, 0), (0, PAGE_SIZE - s)))
    kint, knt, vnt = new_page(ki_b), new_page(k_b), new_page(v_b)

    n_sc = n_pages // _PAGES_SC
    sc_w = _PAGES_SC * PAGE_SIZE

    def page_spec(width, per_step, r):
        return pl.BlockSpec((1, width, PAGE_SIZE),
                            lambda b, j, pt: (pt[b, jnp.minimum(j * per_step + r, n_pages - 1)], 0, 0))

    scores = pl.pallas_call(
        _sample_scores_kernel,
        out_shape=jax.ShapeDtypeStruct((nb, _QPAD, (n_sc + 1) * sc_w), F32),
        grid_spec=pltpu.PrefetchScalarGridSpec(
            num_scalar_prefetch=1, grid=(nb, n_sc + 1),
            in_specs=[pl.BlockSpec((1, N_IDX_HEADS * _QPAD, IDX_DIM), lambda b, j, pt: (b, 0, 0)),
                      pl.BlockSpec((1, _QPAD, N_IDX_HEADS), lambda b, j, pt: (b, 0, 0)),
                      pl.BlockSpec((1, IDX_DIM, PAGE_SIZE), lambda b, j, pt: (b, 0, 0))]
                     + [page_spec(IDX_DIM, _PAGES_SC, r) for r in range(_PAGES_SC)],
            out_specs=pl.BlockSpec((1, _QPAD, sc_w), lambda b, j, pt: (b, 0, j))),
        compiler_params=_params("parallel", "arbitrary"),
        name="sample_scores",
    )(page_table, qi, wi8, kint, *([ki_pool] * _PAGES_SC))

    width = scores.shape[2]
    cols = nb * _QPAD
    bias = pl.pallas_call(
        functools.partial(_sample_select_kernel, past=past, dec_seq=s),
        out_shape=jax.ShapeDtypeStruct((width, cols), F32),
        grid=(cols // LANES,),
        in_specs=[pl.BlockSpec((width, LANES), lambda i: (0, i))],
        out_specs=pl.BlockSpec((width, LANES), lambda i: (0, i)),
        scratch_shapes=[pltpu.VMEM((width, LANES), F32), pltpu.VMEM((_SUBLANES, LANES), F32)],
        compiler_params=_params("parallel"),
        name="sample_select",
    )(scores.reshape(cols, width).T).T.reshape(nb, _QPAD, width)

    q4 = q_b.reshape(nb, s, N_KV_HEADS, Q_PER_KV, HEAD_DIM)[:, dup]
    eye = jnp.eye(N_KV_HEADS, dtype=q_b.dtype)
    qx = jnp.einsum("bqgjd,gk->bgjqkd", q4, eye).reshape(nb, N_HEADS * _QPAD, KV_W)
    n_kv = n_pages // _PAGES_KV
    kv_w = _PAGES_KV * PAGE_SIZE
    new_blk = past // kv_w
    hq = N_HEADS * _QPAD
    o = pl.pallas_call(
        _sample_attend_kernel,
        out_shape=jax.ShapeDtypeStruct((nb, hq, KV_W), F32),
        grid_spec=pltpu.PrefetchScalarGridSpec(
            num_scalar_prefetch=1, grid=(nb, n_kv + 1),
            in_specs=[pl.BlockSpec((1, hq, KV_W), lambda b, j, pt: (b, 0, 0)),
                      pl.BlockSpec((1, _QPAD, kv_w), lambda b, j, pt: (b, 0, jnp.minimum(j, new_blk))),
                      pl.BlockSpec((1, KV_W, PAGE_SIZE), lambda b, j, pt: (b, 0, 0)),
                      pl.BlockSpec((1, KV_W, PAGE_SIZE), lambda b, j, pt: (b, 0, 0))]
                     + [page_spec(KV_W, _PAGES_KV, r) for r in range(_PAGES_KV)] * 2,
            out_specs=pl.BlockSpec((1, hq, KV_W), lambda b, j, pt: (b, 0, 0)),
            scratch_shapes=[pltpu.VMEM((hq, 1), F32), pltpu.VMEM((hq, 1), F32), pltpu.VMEM((hq, KV_W), F32)]),
        compiler_params=_params("parallel", "arbitrary"),
        name="sample_attend",
    )(page_table, qx, bias, knt, vnt, *([k_pool] * _PAGES_KV), *([v_pool] * _PAGES_KV))
    o = o.reshape(nb, N_KV_HEADS, Q_PER_KV, _QPAD, N_KV_HEADS, HEAD_DIM)[:, :, :, :s]
    o = jnp.einsum("bgjqkd,gk->bqgjd", o, jnp.eye(N_KV_HEADS, dtype=o.dtype))
    return o.reshape(nb * s, ATTN_W)


_ROUTE_W = 8


def _merge_kernel(l_ref, x_ref, yc_ref, o_ref, g_ref, wao_ref, wout_ref, lg_ref, lb_ref, rwh_ref, rwl_ref,
                  rb_ref, x1_ref, x1b_ref, route_ref, *, alpha):
    y_attn = jnp.dot(o_ref[...].astype(BF16), wao_ref[...], preferred_element_type=F32)
    mix_in = g_ref[:, :D_MODEL] * yc_ref[...] + g_ref[:, D_MODEL:] * y_attn
    mix = jnp.dot(mix_in.astype(BF16), wout_ref[...], preferred_element_type=F32)
    x1 = _layer_norm(alpha * x_ref[...] + mix, lg_ref[...], lb_ref[...])
    x1_ref[...] = x1
    hi = x1.astype(BF16)
    x1b_ref[...] = hi

    lo = (x1 - hi.astype(F32)).astype(BF16)
    logits = (jnp.dot(hi, rwh_ref[...], preferred_element_type=F32)
              + jnp.dot(hi, rwl_ref[...], preferred_element_type=F32)
              + jnp.dot(lo, rwh_ref[...], preferred_element_type=F32)) + rb_ref[...]
    lane = lax.broadcasted_iota(I32, (1, LANES), 1)

    def first_max(v):
        m = jnp.max(v, axis=1, keepdims=True)
        return m, jnp.min(jnp.where(v == m, lane, LANES), axis=1, keepdims=True)

    is_group = (lane >= N_EXPERTS) & (lane < N_EXPERTS + N_GROUPS)
    gl = jnp.where(is_group, logits, -jnp.inf)
    gm, gidx = first_max(gl)
    p_grp = 1.0 / jnp.sum(jnp.exp(gl - gm), axis=1, keepdims=True)
    grp = gidx - N_EXPERTS
    group_of_lane = lane >> (EXP_PER_GROUP.bit_length() - 1)
    el = jnp.where(group_of_lane == grp, logits, -jnp.inf)
    e1, i1 = first_max(el)
    e2, i2 = first_max(jnp.where(lane == i1, -jnp.inf, el))
    t = jnp.exp(e2 - e1)
    g1 = p_grp / (1.0 + t)
    g2 = g1 * t
    rl = lax.broadcasted_iota(I32, (1, _ROUTE_W), 1)
    route_ref[...] = jnp.where(rl == 0, i1.astype(F32), jnp.where(rl == 1, i2.astype(F32),
                               jnp.where(rl == 2, g1, jnp.where(rl == 3, g2, 0.0))))


def _merge(lidx, x2d, yc, o, g, p, tm, alpha):
    t = x2d.shape[0]
    row = lambda n: pl.BlockSpec((tm, n), lambda i, l: (i, 0))
    return pl.pallas_call(
        functools.partial(_merge_kernel, alpha=alpha),
        out_shape=[jax.ShapeDtypeStruct((t, D_MODEL), F32), jax.ShapeDtypeStruct((t, D_MODEL), BF16),
                   jax.ShapeDtypeStruct((t, _ROUTE_W), F32)],
        grid_spec=pltpu.PrefetchScalarGridSpec(
            num_scalar_prefetch=1, grid=(t // tm,),
            in_specs=[row(D_MODEL), row(D_MODEL), row(ATTN_W), row(2 * D_MODEL),
                      _lspec(ATTN_W, D_MODEL), _lspec(D_MODEL, D_MODEL), _lspec(1, D_MODEL),
                      _lspec(1, D_MODEL), _lspec(D_MODEL, LANES), _lspec(D_MODEL, LANES), _lspec(1, LANES)],
            out_specs=[row(D_MODEL), row(D_MODEL), row(_ROUTE_W)]),
        compiler_params=_params("parallel"),
        name="merge_ln1_router",
    )(lidx, x2d, yc, o, g, p["w_attn_out"], p["w_out"], p["ln1_g"], p["ln1_b"], p["router_hi"],
      p["router_lo"], p["router_b"])


_SLOT_ALIGN = SUBLANES_BF16
_ROW_BLK = 128
_EXPERTS_PER_STEP = 4
_MOE_CHUNK = 256
_META_W = LANES
_MOE_TILE_MAX = 256 * _SLOT_ALIGN


def _moe_plan_kernel(route_ref, dcol_ref, drow_ref, meta_ref):
    tm = route_ref.shape[0]
    blk_rows = min(_MOE_CHUNK, tm)
    lane = lax.broadcasted_iota(I32, (1, LANES), 1).astype(F32)
    hit0 = lane == route_ref[:, 0:1]
    hit1 = lane == route_ref[:, 1:2]
    onehot = jnp.where(hit0, 1.0, jnp.where(hit1, 1.0, 0.0))
    r = lax.broadcasted_iota(I32, (blk_rows, blk_rows), 0)
    c = lax.broadcasted_iota(I32, (blk_rows, blk_rows), 1)
    tri = jnp.where(c < r, 1.0, 0.0).astype(BF16)
    carry = jnp.zeros((1, LANES), F32)
    prefix = []
    for b0 in range(0, tm, blk_rows):
        blk = onehot[b0:b0 + blk_rows]
        prefix.append(jnp.dot(tri, blk.astype(BF16), preferred_element_type=F32) + carry)
        carry = carry + jnp.sum(blk, axis=0, keepdims=True)
    prefix = jnp.concatenate(prefix, axis=0)
    units = jnp.ceil(carry * (1.0 / _SLOT_ALIGN))
    rr = lax.broadcasted_iota(I32, (LANES, LANES), 0)
    cc = lax.broadcasted_iota(I32, (LANES, LANES), 1)
    upper = jnp.where(rr < cc, 1.0, 0.0).astype(BF16)
    units8 = jnp.broadcast_to(units, (8, LANES)).astype(BF16)
    offs = jnp.dot(units8, upper, preferred_element_type=F32)[0:1] * _SLOT_ALIGN
    slot = offs + prefix
    d0 = jnp.sum(jnp.where(hit0, slot, 0.0), axis=1, keepdims=True)
    d1 = jnp.sum(jnp.where(hit1, slot, 0.0), axis=1, keepdims=True)
    rl = lax.broadcasted_iota(I32, (1, LANES), 1)
    rec = jnp.where(rl == 0, d0, jnp.where(rl == 1, d1, jnp.where(rl == 2, route_ref[:, 2:3],
                    jnp.where(rl == 3, route_ref[:, 3:4], 0.0))))
    dcol_ref[...] = rec[:, :_ROUTE_W]
    drow_ref[...] = rec.T[:_ROUTE_W, :]
    nblk = jnp.ceil(units * (_SLOT_ALIGN / _ROW_BLK))
    shift = jnp.where(cc == rr + N_EXPERTS, 1.0, 0.0).astype(BF16)
    nblk_sh = jnp.dot(jnp.broadcast_to(nblk, (8, LANES)).astype(BF16), shift,
                      preferred_element_type=F32)[0:1]
    meta_ref[0] = jnp.where(rl < N_EXPERTS, offs, nblk_sh).astype(I32)


def _moe_plan(route, tm):
    t = route.shape[0]
    nt = t // tm
    assert tm <= _MOE_TILE_MAX
    return pl.pallas_call(
        _moe_plan_kernel,
        out_shape=[jax.ShapeDtypeStruct((t, _ROUTE_W), F32), jax.ShapeDtypeStruct((_ROUTE_W, t), F32),
                   jax.ShapeDtypeStruct((nt, 1, _META_W), I32)],
        grid=(nt,),
        in_specs=[pl.BlockSpec((tm, _ROUTE_W), lambda i: (i, 0))],
        out_specs=[pl.BlockSpec((tm, _ROUTE_W), lambda i: (i, 0)),
                   pl.BlockSpec((_ROUTE_W, tm), lambda i: (0, i)),
                   pl.BlockSpec((1, 1, _META_W), lambda i: (i, 0, 0))],
        compiler_params=_params("parallel"),
        name="moe_plan",
    )(route)


def _n_slots(tm):
    n = 2 * tm + N_EXPERTS * (_SLOT_ALIGN - 1) + _ROW_BLK
    return -(-n // _MOE_CHUNK) * _MOE_CHUNK


def _moe_kernel(meta_ref, l_ref, x1b_ref, x1_ref, drow_ref, dcol_ref, wg_ref, wu_ref, wd_ref, lg_ref, lb_ref,
                x2_ref, xb_ref, yb_ref, gs_ref, *, alpha):
    i, e = pl.program_id(0), pl.program_id(1)
    tm = x1_ref.shape[0]
    n_slots = xb_ref.shape[0]

    @pl.when(e == 0)
    def _dispatch():
        d0, d1 = drow_ref[0:1, :], drow_ref[1:2, :]
        g0, g1 = drow_ref[2:3, :], drow_ref[3:4, :]
        for s0 in range(0, n_slots, _MOE_CHUNK):
            sl = (s0 + lax.broadcasted_iota(I32, (_MOE_CHUNK, 1), 0)).astype(F32)
            a, b = sl == d0, sl == d1
            p = jnp.where(a, 1.0, jnp.where(b, 1.0, 0.0)).astype(BF16)
            xb_ref[s0:s0 + _MOE_CHUNK, :] = jnp.dot(p, x1b_ref[...], preferred_element_type=F32).astype(BF16)
            gs_ref[s0:s0 + _MOE_CHUNK, :] = jnp.sum(jnp.where(a, g0, jnp.where(b, g1, 0.0)), axis=1,
                                                    keepdims=True)
        yb_ref[...] = jnp.zeros(yb_ref.shape, BF16)

    for k in range(_EXPERTS_PER_STEP):
        expert = e * _EXPERTS_PER_STEP + k
        off = meta_ref[i * _META_W + expert]
        nblk = meta_ref[i * _META_W + N_EXPERTS + expert]

        def block(r, carry, k=k, off=off):
            start = pl.multiple_of(off + r * _ROW_BLK, _SLOT_ALIGN)
            xs = xb_ref[pl.ds(start, _ROW_BLK), :]
            hg = jnp.dot(xs, wg_ref[k], preferred_element_type=F32)
            hu = jnp.dot(xs, wu_ref[k], preferred_element_type=F32)
            h = (hg * jax.nn.sigmoid(hg)) * hu
            y = jnp.dot(h.astype(BF16), wd_ref[k], preferred_element_type=F32)
            yb_ref[pl.ds(start, _ROW_BLK), :] = (y * gs_ref[pl.ds(start, _ROW_BLK), :]).astype(BF16)
            return carry

        lax.fori_loop(0, nblk, block, 0)

    @pl.when(e == pl.num_programs(1) - 1)
    def _combine():
        lane = lax.broadcasted_iota(I32, (1, n_slots), 1).astype(F32)
        rows = min(_MOE_CHUNK, tm)
        for t0 in range(0, tm, rows):
            d0, d1 = dcol_ref[t0:t0 + rows, 0:1], dcol_ref[t0:t0 + rows, 1:2]
            qm = jnp.where(lane == d0, 1.0, jnp.where(lane == d1, 1.0, 0.0)).astype(BF16)
            y = jnp.dot(qm, yb_ref[...], preferred_element_type=F32)
            z = alpha * x1_ref[t0:t0 + rows, :] + y
            x2_ref[t0:t0 + rows, :] = _layer_norm(z, lg_ref[...], lb_ref[...])


def _moe(lidx, x1, x1b, drow, dcol, meta, p, tm, alpha):
    t = x1.shape[0]
    n_slots = _n_slots(tm)
    tile = lambda shape, imap: pl.BlockSpec(shape, imap)
    return pl.pallas_call(
        functools.partial(_moe_kernel, alpha=alpha),
        out_shape=jax.ShapeDtypeStruct((t, D_MODEL), F32),
        grid_spec=pltpu.PrefetchScalarGridSpec(
            num_scalar_prefetch=2, grid=(t // tm, N_EXPERTS // _EXPERTS_PER_STEP),
            in_specs=[tile((tm, D_MODEL), lambda i, e, m, l: (i, 0)),
                      tile((tm, D_MODEL), lambda i, e, m, l: (i, 0)),
                      tile((_ROUTE_W, tm), lambda i, e, m, l: (0, i)),
                      tile((tm, _ROUTE_W), lambda i, e, m, l: (i, 0)),
                      tile((None, _EXPERTS_PER_STEP, D_MODEL, D_EXPERT), lambda i, e, m, l: (l[0], e, 0, 0)),
                      tile((None, _EXPERTS_PER_STEP, D_MODEL, D_EXPERT), lambda i, e, m, l: (l[0], e, 0, 0)),
                      tile((None, _EXPERTS_PER_STEP, D_EXPERT, D_MODEL), lambda i, e, m, l: (l[0], e, 0, 0)),
                      _lspec(1, D_MODEL), _lspec(1, D_MODEL)],
            out_specs=tile((tm, D_MODEL), lambda i, e, m, l: (i, 0)),
            scratch_shapes=[pltpu.VMEM((n_slots, D_MODEL), BF16), pltpu.VMEM((n_slots, D_MODEL), BF16),
                            pltpu.VMEM((n_slots, 1), F32)]),
        compiler_params=_params("parallel", "arbitrary"),
        name="moe_experts_ln2",
    )(meta.reshape(-1), lidx, x1b, x1, drow, dcol, p["w_gate"], p["w_up"], p["w_down"], p["ln2_g"], p["ln2_b"])


def _token_tile(t, want):
    return want if t % want == 0 else t


def _decoder_layer(lidx, x, prefix, attend, p, alpha):
    nb, s, _ = x.shape
    t = nb * s
    x2d = x.reshape(t, D_MODEL)
    u, q_b, k, v, qi_b, ki, wi, g, k_b, v_b, ki_b = _inproj(lidx, x2d, p["w_in"], p["b_in"], _token_tile(t, 512))

    ctm = 512 if s % 512 == 0 else _HALO
    u3 = u.reshape(nb, s, C_CONV)
    history = jnp.pad(prefix, ((0, 0), (_HALO - (CONV_W - 1), 0), (0, 0)))
    u_rows = jnp.pad(u3, ((0, 0), (0, (-s) % ctm), (0, 0)))
    yc = _conv_branch(lidx, history, u_rows, p["conv_w"], p["conv_b"], p["conv_ln_g"], p["conv_ln_b"],
                      p["w_conv_out"], ctm)
    yc = yc[:, :s].reshape(t, D_MODEL)

    o = attend(q_b, k_b, v_b, qi_b, ki_b, wi)
    x1, x1b, route = _merge(lidx, x2d, yc, o, g, p, _token_tile(t, 256), alpha)
    mtm = _token_tile(t, 1024)
    dcol, drow, meta = _moe_plan(route, mtm)
    x2 = _moe(lidx, x1, x1b, drow, dcol, meta, p, mtm, alpha)

    new_conv = jnp.concatenate([prefix, u3], axis=1)[:, -(CONV_W - 1):]
    return (x2.reshape(nb, s, D_MODEL), k.reshape(nb, s, N_KV_HEADS, HEAD_DIM),
            v.reshape(nb, s, N_KV_HEADS, HEAD_DIM), ki.reshape(nb, s, IDX_DIM), new_conv)


def _prepare_params(w_in, b_in, conv_w, conv_b, conv_ln_g, conv_ln_b, w_conv_out, w_attn_out, w_out,
                    ln1_g, ln1_b, router_group_w, router_group_b, router_expert_w, router_expert_b,
                    w_gate, w_up, w_down, ln2_g, ln2_b):
    depth = w_in.shape[0]
    pad = _COL_G - _N_IN_HEAD
    w_pad = jnp.concatenate([w_in[..., :_N_IN_HEAD], jnp.zeros((depth, D_MODEL, pad), F32),
                             w_in[..., _N_IN_HEAD:]], axis=-1).astype(BF16)
    b_pad = jnp.concatenate([b_in[..., :_N_IN_HEAD], jnp.zeros((depth, pad), F32),
                             b_in[..., _N_IN_HEAD:]], axis=-1)[:, None, :]
    rpad = LANES - N_EXPERTS - N_GROUPS
    rw = jnp.concatenate([router_expert_w, router_group_w, jnp.zeros((depth, D_MODEL, rpad), F32)], axis=-1)
    rb = jnp.concatenate([router_expert_b, router_group_b, jnp.zeros((depth, rpad), F32)], axis=-1)
    rw_hi = rw.astype(BF16)
    rw_lo = (rw - rw_hi.astype(F32)).astype(BF16)
    vec = lambda a: a[:, None, :]
    return dict(w_in=w_pad, b_in=b_pad, conv_w=conv_w, conv_b=vec(conv_b), conv_ln_g=vec(conv_ln_g),
                conv_ln_b=vec(conv_ln_b), w_conv_out=w_conv_out.astype(BF16),
                w_attn_out=w_attn_out.astype(BF16), w_out=w_out.astype(BF16), ln1_g=vec(ln1_g),
                ln1_b=vec(ln1_b), router_hi=rw_hi, router_lo=rw_lo, router_b=vec(rb),
                w_gate=w_gate.astype(BF16), w_up=w_up.astype(BF16), w_down=w_down.astype(BF16),
                ln2_g=vec(ln2_g), ln2_b=vec(ln2_b))


def kernel(x_prompt, x_sample, cache_k, cache_v, cache_kidx, state_conv, page_table, w_in, b_in, conv_w,
           conv_b, conv_ln_g, conv_ln_b, w_conv_out, w_attn_out, w_out, ln1_g, ln1_b, router_group_w,
           router_group_b, router_expert_w, router_expert_b, w_gate, w_up, w_down, ln2_g, ln2_b):
    params = _prepare_params(w_in, b_in, conv_w, conv_b, conv_ln_g, conv_ln_b, w_conv_out, w_attn_out,
                             w_out, ln1_g, ln1_b, router_group_w, router_group_b, router_expert_w,
                             router_expert_b, w_gate, w_up, w_down, ln2_g, ln2_b)
    nb, s, _ = x_prompt.shape
    db, ds, _ = x_sample.shape
    depth, n_phys = cache_k.shape[:2]
    alpha = (2 * depth) ** 0.25
    k_pool = cache_k.transpose(0, 1, 3, 4, 2).reshape(depth * n_phys, KV_W, PAGE_SIZE)
    v_pool = cache_v.transpose(0, 1, 3, 4, 2).reshape(depth * n_phys, KV_W, PAGE_SIZE)
    ki_pool = cache_kidx.transpose(0, 1, 3, 2).reshape(depth * n_phys, IDX_DIM, PAGE_SIZE)
    conv_zero = jnp.zeros((nb, CONV_W - 1, C_CONV), F32)

    def layer(carry, xs):
        xp, xsm = carry
        l, st = xs
        lidx = l.reshape(1)
        prompt_attend = functools.partial(_dsa_prompt, nb=nb, s=s)
        xp, *new_p = _decoder_layer(lidx, xp, conv_zero, prompt_attend, params, alpha)
        sample_attend = functools.partial(_dsa_sample, k_pool=k_pool, v_pool=v_pool, ki_pool=ki_pool,
                                          page_table=page_table + l * n_phys, nb=db, s=ds)
        xsm, *new_s = _decoder_layer(lidx, xsm, st, sample_attend, params, alpha)
        return (xp, xsm), (tuple(new_p), tuple(new_s))

    (xp, xsm), (new_p, new_s) = lax.scan(layer, (x_prompt, x_sample),
                                         (jnp.arange(depth, dtype=I32), state_conv))
    return (xp, xsm, *new_p, *new_s)
```

```python
import functools

import jax
import jax.numpy as jnp
from jax import lax
from jax.experimental import pallas as pl
from jax.experimental.pallas import tpu as pltpu

F32 = jnp.float32
BF16 = jnp.bfloat16
I32 = jnp.int32

D_MODEL = 1024
PAGE_SIZE = 128
C_CONV = 512
CONV_W = 31
N_HEADS = 8
N_KV_HEADS = 4
HEAD_DIM = 64
Q_PER_KV = N_HEADS // N_KV_HEADS
ATTN_W = N_HEADS * HEAD_DIM
KV_W = N_KV_HEADS * HEAD_DIM
N_IDX_HEADS = 8
IDX_DIM = 64
TOPK_MAX = 256
N_GROUPS = 4
EXP_PER_GROUP = 8
N_EXPERTS = N_GROUPS * EXP_PER_GROUP
D_EXPERT = 256
LN_EPS = 1e-5

LANES = 128
_SUBLANES = 8
SUBLANES_BF16 = 16
VMEM_LIMIT = 56 * 1024 * 1024

_N_SMALL = IDX_DIM + N_IDX_HEADS
_COL_CA = 0
_COL_CB = _COL_CA + C_CONV
_COL_Q = _COL_CB + C_CONV
_COL_K = _COL_Q + ATTN_W
_COL_V = _COL_K + KV_W
_COL_QI = _COL_V + KV_W
_COL_KI = _COL_QI + N_IDX_HEADS * IDX_DIM
_COL_G = _COL_KI + LANES
_N_IN_PAD = _COL_G + 2 * D_MODEL
_N_IN_HEAD = _COL_KI + _N_SMALL

_NT = (((1,), (1,)), ((), ()))
_NEG = -1e30
_LOG2E = 1.4426950408889634


def _params(*sem):
    return pltpu.CompilerParams(dimension_semantics=sem, vmem_limit_bytes=VMEM_LIMIT)


def _lspec(*shape):
    nd = len(shape)
    return pl.BlockSpec((None,) + shape, lambda *a: (a[-1][0],) + (0,) * nd)


def _layer_norm(x, g, b):
    mu = jnp.mean(x, axis=-1, keepdims=True)
    xc = x - mu
    var = jnp.mean(xc * xc, axis=-1, keepdims=True)
    return xc * lax.rsqrt(var + LN_EPS) * g + b


def _inproj_kernel(l_ref, x_ref, w_ref, b_ref, u_ref, q_ref, k_ref, v_ref, qi_ref, ki_ref, wi_ref, g_ref,
                   kb_ref, vb_ref, kib_ref, *, key_transposed):
    xb = x_ref[...].astype(BF16)

    def proj(c0, n):
        return jnp.dot(xb, w_ref[:, c0:c0 + n], preferred_element_type=F32) + b_ref[:, c0:c0 + n]

    u_ref[...] = proj(_COL_CA, C_CONV) * jax.nn.sigmoid(proj(_COL_CB, C_CONV))
    q_ref[...] = (proj(_COL_Q, ATTN_W) * (HEAD_DIM ** -0.5 * _LOG2E)).astype(BF16)
    k = proj(_COL_K, KV_W)
    v = proj(_COL_V, KV_W)
    if key_transposed:
        k_ref[0] = k.T
        v_t = v.T
        v_ref[0] = v_t
        vb_ref[0] = v_t.astype(BF16)
        for g in range(N_KV_HEADS):
            kb_ref[0, g] = k[:, g * HEAD_DIM:(g + 1) * HEAD_DIM].astype(BF16)
    else:
        k_ref[...] = k
        kb_ref[...] = k.astype(BF16)
        v_ref[...] = v
        vb_ref[...] = v.astype(BF16)
    qi_ref[...] = (proj(_COL_QI, N_IDX_HEADS * IDX_DIM) * (IDX_DIM ** -0.5)).astype(BF16)
    small = proj(_COL_KI, LANES)
    ki = small[:, :IDX_DIM]
    ki_ref[...] = ki
    kib_ref[...] = ki.astype(BF16)
    wi_ref[...] = small[:, IDX_DIM:_N_SMALL] * (N_IDX_HEADS ** -0.5)
    g_ref[...] = jax.nn.sigmoid(proj(_COL_G, 2 * D_MODEL))


def _inproj(lidx, x2d, w_pad, b_pad, tm, seq=None):
    t = x2d.shape[0]
    rows = lambda n, dt: (jax.ShapeDtypeStruct((t, n), dt), pl.BlockSpec((tm, n), lambda i, l: (i, 0)))
    if seq is None:
        k, v, kb, vb = rows(KV_W, F32), rows(KV_W, F32), rows(KV_W, BF16), rows(KV_W, BF16)
    else:
        nb, s = seq
        tps = s // tm
        kt = lambda dt: (jax.ShapeDtypeStruct((nb, KV_W, s), dt),
                         pl.BlockSpec((1, KV_W, tm), lambda i, l: (i // tps, 0, i % tps)))
        k, v, vb = kt(F32), kt(F32), kt(BF16)
        kb = (jax.ShapeDtypeStruct((nb, N_KV_HEADS, s, HEAD_DIM), BF16),
              pl.BlockSpec((1, N_KV_HEADS, tm, HEAD_DIM), lambda i, l: (i // tps, 0, i % tps, 0)))
    outs = [rows(C_CONV, F32), rows(ATTN_W, BF16), k, v, rows(N_IDX_HEADS * IDX_DIM, BF16), rows(IDX_DIM, F32),
            rows(N_IDX_HEADS, F32), rows(2 * D_MODEL, F32), kb, vb, rows(IDX_DIM, BF16)]
    return pl.pallas_call(
        functools.partial(_inproj_kernel, key_transposed=seq is not None),
        out_shape=[o[0] for o in outs],
        grid_spec=pltpu.PrefetchScalarGridSpec(
            num_scalar_prefetch=1, grid=(t // tm,),
            in_specs=[pl.BlockSpec((tm, D_MODEL), lambda i, l: (i, 0)),
                      _lspec(D_MODEL, _N_IN_PAD), _lspec(1, _N_IN_PAD)],
            out_specs=[o[1] for o in outs]),
        compiler_params=_params("parallel"),
        name="inproj",
    )(lidx, x2d, w_pad, b_pad)


_HALO = 32
_CONV_ROWS = 32


def _conv_kernel(l_ref, first_ref, prev_ref, cur_ref, cw_ref, cb_ref, lg_ref, lb_ref, wo_ref, y_ref,
                 hist_ref, acc_ref, shift_ref):
    tm = cur_ref.shape[1]
    at_start = pl.program_id(1) == 0
    hist_ref[0:_HALO, :] = jnp.where(at_start, first_ref[0], prev_ref[0])
    hist_ref[_HALO:_HALO + tm, :] = cur_ref[0]
    first = _HALO - (CONV_W - 1)
    for b in range(_SUBLANES):
        rows = tm + _SUBLANES * ((CONV_W - 1 - b) // _SUBLANES)
        shift_ref[b, 0:rows, :] = hist_ref[first + b:first + b + rows, :]
    for r0 in range(0, tm, _CONV_ROWS):
        acc = jnp.zeros((_CONV_ROWS, C_CONV), F32)
        for j in range(CONV_W):
            a, b = divmod(j, _SUBLANES)
            acc = acc + cw_ref[j:j + 1, :] * shift_ref[b, r0 + a * _SUBLANES:r0 + a * _SUBLANES + _CONV_ROWS, :]
        acc_ref[r0:r0 + _CONV_ROWS, :] = acc + cb_ref[...]
    y = _layer_norm(acc_ref[...], lg_ref[...], lb_ref[...])
    y = y * jax.nn.sigmoid(y)
    y_ref[0] = jnp.dot(y.astype(BF16), wo_ref[...], preferred_element_type=F32)


def _conv_branch(lidx, history, u, conv_w, conv_b, ln_g, ln_b, w_out_bf, tm):
    nb, length, _ = u.shape
    halo_blocks = tm // _HALO
    return pl.pallas_call(
        _conv_kernel,
        out_shape=jax.ShapeDtypeStruct((nb, length, D_MODEL), F32),
        grid_spec=pltpu.PrefetchScalarGridSpec(
            num_scalar_prefetch=1, grid=(nb, length // tm),
            in_specs=[pl.BlockSpec((1, _HALO, C_CONV), lambda b, i, l: (b, 0, 0)),
                      pl.BlockSpec((1, _HALO, C_CONV), lambda b, i, l: (b, jnp.maximum(i * halo_blocks - 1, 0), 0)),
                      pl.BlockSpec((1, tm, C_CONV), lambda b, i, l: (b, i, 0)),
                      _lspec(CONV_W, C_CONV), _lspec(1, C_CONV), _lspec(1, C_CONV), _lspec(1, C_CONV),
                      _lspec(C_CONV, D_MODEL)],
            out_specs=pl.BlockSpec((1, tm, D_MODEL), lambda b, i, l: (b, i, 0)),
            scratch_shapes=[pltpu.VMEM((_HALO + tm, C_CONV), F32), pltpu.VMEM((tm, C_CONV), F32),
                            pltpu.VMEM((_SUBLANES, tm + _HALO - _SUBLANES, C_CONV), F32)]),
        compiler_params=_params("parallel", "parallel"),
        name="conv_branch",
    )(lidx, history, u, u, conv_w, conv_b, ln_g, ln_b, w_out_bf)


_BISECT_STEPS = 15


_FOLD_ROWS = 64


def _reduce_rows(x, op):
    rows, cols = x.shape
    if rows > _FOLD_ROWS:
        x = op(x.reshape(rows // _FOLD_ROWS, _FOLD_ROWS, cols), axis=0)
    x = op(x.reshape(x.shape[0] // _SUBLANES, _SUBLANES, cols), axis=0)
    return op(x, axis=0, keepdims=True)


def _fold_keys(sc_ref, n_chunks, tk, init, fn, op, merge, first=0):
    def body(c, acc):
        off = pl.multiple_of(c * tk, tk)
        v = fn(sc_ref[pl.ds(off, tk), :], off).reshape(tk // _FOLD_ROWS, _FOLD_ROWS, cols)
        return merge(acc, op(v, axis=0))

    cols = sc_ref.shape[1]
    acc = lax.fori_loop(first, n_chunks, body, jnp.full((_FOLD_ROWS, cols), init, F32))
    return _reduce_rows(acc, op)


def _count_keys(sc_ref, n_chunks, tk, fn, first=0):
    return _fold_keys(sc_ref, n_chunks, tk, 0.0, fn, jnp.sum, jnp.add, first)


def _count_ge(sc_ref, n_chunks, tk, thr):
    return _count_keys(sc_ref, n_chunks, tk, lambda blk, off: jnp.where(blk >= thr, 1.0, 0.0))


def _any(flag):
    return jnp.max(jnp.where(flag, 1, 0))


def _select_threshold(sc_ref, st_ref, n_chunks, tk, kt, s_min, s_max, n_valid):
    lo_ref, hi_ref, cl_ref, done_ref, jlo_ref, jhi_ref = (st_ref.at[n:n + 1] for n in range(6))
    c_max = _count_ge(sc_ref, n_chunks, tk, s_max)
    top = c_max >= kt
    lo_ref[...] = jnp.where(top, s_max, s_min)
    hi_ref[...] = s_max
    cl0 = jnp.where(top, c_max, n_valid)
    cl_ref[...] = cl0
    done0 = jnp.where(top, 1.0, jnp.where(cl0 == kt, 1.0, 0.0))
    done_ref[...] = done0

    def bisect_step():
        lo, hi, cl, done = lo_ref[...], hi_ref[...], cl_ref[...], done_ref[...]
        mid = 0.5 * lo + 0.5 * hi
        c = _count_ge(sc_ref, n_chunks, tk, mid)
        act = done == 0.0
        up = c >= kt
        lo_ref[...] = jnp.where(act, jnp.where(up, mid, lo), lo)
        cl_ref[...] = jnp.where(act, jnp.where(up, c, cl), cl)
        hi_ref[...] = jnp.where(act, jnp.where(up, hi, mid), hi)
        done_ref[...] = jnp.where(act, jnp.where(c == kt, 1.0, 0.0), done)

    @pl.when(_any(done0 == 0.0) > 0)
    def _narrow():
        lax.fori_loop(0, _BISECT_STEPS, lambda _, carry: (bisect_step(), carry)[1], 0)

    def snap(_):
        lo, hi, cl, done = lo_ref[...], hi_ref[...], cl_ref[...], done_ref[...]
        t1 = _fold_keys(sc_ref, n_chunks, tk, -jnp.inf, lambda blk, off: jnp.where(blk < hi, blk, -jnp.inf),
                        jnp.max, jnp.maximum)
        c1 = _count_ge(sc_ref, n_chunks, tk, t1)
        act = done == 0.0
        found = c1 >= kt
        lo_ref[...] = jnp.where(act, jnp.where(found, t1, lo), lo)
        cl_ref[...] = jnp.where(act, jnp.where(found, c1, cl), cl)
        hi_ref[...] = jnp.where(act, jnp.where(found, hi, t1), hi)
        done_new = jnp.where(act, jnp.where(found, 1.0, 0.0), done)
        done_ref[...] = done_new
        return _any(done_new == 0.0)

    lax.while_loop(lambda f: f > 0, snap, _any(done_ref[...] == 0.0))

    thr = lo_ref[...]

    @pl.when(_any(cl_ref[...] != kt) > 0)
    def _cut_ties():
        tied = cl_ref[...] != kt
        need = kt - _count_keys(sc_ref, n_chunks, tk, lambda blk, off: jnp.where(blk > thr, 1.0, 0.0))
        row = lax.broadcasted_iota(I32, (tk, 1), 0)

        def locate(c, carry):
            seen, chunk, seen_before = carry
            off = pl.multiple_of(c * tk, tk)
            cnt = _reduce_rows(jnp.where(sc_ref[pl.ds(off, tk), :] == thr, 1.0, 0.0), jnp.sum)
            here = jnp.where(seen < need, jnp.where(seen + cnt >= need, 1.0, 0.0), 0.0)
            chunk = jnp.where(here > 0.0, c.astype(F32), chunk)
            seen_before = jnp.where(here > 0.0, seen, seen_before)
            return seen + cnt, chunk, seen_before

        zero = jnp.zeros_like(thr)
        _, chunk, seen_before = lax.fori_loop(0, n_chunks, locate, (zero, zero, zero))
        first_key = chunk * float(tk)
        need_here = need - seen_before
        c_first = jnp.min(jnp.where(tied, chunk, float(sc_ref.shape[0]))).astype(I32)
        c_last = jnp.max(jnp.where(tied, chunk, -1.0)).astype(I32)

        jlo_ref[...] = first_key - 1.0
        jhi_ref[...] = first_key + float(tk - 1)

        def step(_, carry):
            jlo, jhi = jlo_ref[...], jhi_ref[...]
            mid = jnp.floor(0.5 * (jlo + jhi))

            def in_range(blk, off):
                idx = (off + row).astype(F32)
                return jnp.where(blk == thr, jnp.where(idx <= mid, jnp.where(idx >= first_key, 1.0, 0.0), 0.0), 0.0)

            ok = _count_keys(sc_ref, c_last + 1, tk, in_range, c_first) >= need_here
            jhi_ref[...] = jnp.where(ok, mid, jhi)
            jlo_ref[...] = jnp.where(ok, jlo, mid)
            return carry

        lax.fori_loop(0, tk.bit_length() - 1, step, 0)
        cut = jnp.where(tied, jhi_ref[...], float(sc_ref.shape[0]))

        def demote(c, carry):
            off = pl.multiple_of(c * tk, tk)
            blk = sc_ref[pl.ds(off, tk), :]
            beyond = (off + row).astype(F32) > cut
            sc_ref[pl.ds(off, tk), :] = jnp.where(blk == thr, jnp.where(beyond, -jnp.inf, blk), blk)
            return carry

        lax.fori_loop(c_first, n_chunks, demote, 0)

    return thr


def _min_max_init(cols):
    return (jnp.full((_FOLD_ROWS, cols), jnp.inf, F32), jnp.full((_FOLD_ROWS, cols), -jnp.inf, F32))


def _min_max_update(mn, mx, score, valid):
    shape = (score.shape[0] // _FOLD_ROWS, _FOLD_ROWS, score.shape[1])
    mn = jnp.minimum(mn, jnp.min(jnp.where(valid, score, jnp.inf).reshape(shape), axis=0))
    mx = jnp.maximum(mx, jnp.max(jnp.where(valid, score, -jnp.inf).reshape(shape), axis=0))
    return mn, mx


_TQ = 256
_TK = 512


_IDX_PAIRS = N_IDX_HEADS // 2
_ONES_ROWS = SUBLANES_BF16


def _dsa_prompt_kernel(l_ref, qi_ref, wi_ref, ki_ref, q_ref, k_ref, vt_ref, wao_ref, y_ref,
                       sc_ref, st_ref, m_ref, acc_ref, s_ref, cm_ref, p_ref, al_ref):
    tq, tk = _TQ, _TK
    i = pl.program_id(1)
    n_chunks = ((i + 1) * tq + tk - 1) // tk
    qpos = i * tq + lax.broadcasted_iota(I32, (1, tq), 1)
    krow = lax.broadcasted_iota(I32, (tk, 1), 0)

    def scores(c, carry):
        off = pl.multiple_of(c * tk, tk)
        kc = ki_ref[0, pl.ds(off, tk), :]
        score = None
        for p in range(_IDX_PAIRS):
            d = lax.dot_general(kc, qi_ref[0, 0, p], _NT, preferred_element_type=F32)
            for j in range(2):
                h = 2 * p + j
                t = wi_ref[0, h:h + 1, :] * jnp.maximum(d[:, j * tq:(j + 1) * tq], 0.0)
                score = t if score is None else score + t
        score = jnp.where(score == 0.0, 0.0, score)
        valid = off + krow <= qpos
        sc_ref[pl.ds(off, tk), :] = jnp.where(valid, score, -jnp.inf)
        return _min_max_update(*carry, score, valid)

    mn, mx = lax.fori_loop(0, n_chunks, scores, _min_max_init(tq))
    n_valid = (qpos + 1).astype(F32)
    kt = jnp.minimum(n_valid, float(TOPK_MAX))
    thr = _select_threshold(sc_ref, st_ref, n_chunks, tk, kt, _reduce_rows(mn, jnp.min),
                            _reduce_rows(mx, jnp.max), n_valid)

    m_ref[...] = jnp.full(m_ref.shape, _NEG, F32)
    acc_ref[...] = jnp.zeros(acc_ref.shape, F32)

    def mask_bias(c):
        off = pl.multiple_of(c * tk, tk)
        bias = jnp.where(sc_ref[pl.ds(off, tk), :] >= thr, 0.0, -jnp.inf)
        return jnp.concatenate([bias] * Q_PER_KV, axis=1)

    def qk_stage(c, slot, groups, bias=None):
        off = pl.multiple_of(c * tk, tk)
        bias = mask_bias(c) if bias is None else bias
        for g in groups:
            s = lax.dot_general(k_ref[0, g, pl.ds(off, tk), :], q_ref[0, 0, g], _NT,
                                preferred_element_type=F32) + bias
            s_ref[slot, g] = s
            cm_ref[slot, g] = _reduce_rows(s, jnp.max)

    def exp_stage(slot, groups):
        for g in groups:
            m_old = m_ref[g]
            m_new = jnp.maximum(m_old, cm_ref[slot, g])
            al_ref[slot, g] = jnp.exp2(m_old - m_new)
            p_ref[slot, g] = jnp.exp2(s_ref[slot, g] - m_new).astype(BF16)
            m_ref[g] = m_new

    def pv_stage(c, slot, groups):
        off = pl.multiple_of(c * tk, tk)
        for g in groups:
            v_ones =jnp.concatenate([vt_ref[0, g, :, pl.ds(off, tk)], jnp.ones((_ONES_ROWS, tk), BF16)], axis=0)
            acc_ref[g] = al_ref[slot, g] * acc_ref[g] + jnp.dot(v_ones, p_ref[slot, g],
                                                                preferred_element_type=F32)

    every = range(N_KV_HEADS)

    def attend(c, carry):
        slot = c % 2
        for g in every:
            pv_stage(c, slot, (g,))
            qk_stage(c + 2, slot, (g,))
            exp_stage(1 - slot, (g,))
        return carry

    qk_stage(0, 0, every)
    exp_stage(0, every)

    @pl.when(n_chunks > 1)
    def _fill():
        qk_stage(1, 1, every)

    lax.fori_loop(0, n_chunks - 2, attend, 0)
    last = n_chunks - 1

    @pl.when(n_chunks > 1)
    def _drain():
        exp_stage(last % 2, every)
        pv_stage(last - 1, (last - 1) % 2, every)

    pv_stage(last, last % 2, every)
    heads = []
    for g in range(N_KV_HEADS):
        o_g = acc_ref[g, 0:HEAD_DIM, :] / acc_ref[g, HEAD_DIM:HEAD_DIM + 1, :]
        heads += [o_g[:, j * tq:(j + 1) * tq] for j in range(Q_PER_KV)]
    o_t = jnp.concatenate(heads, axis=0).astype(BF16)
    y_ref[0] = lax.dot_general(o_t, wao_ref[...], (((0,), (0,)), ((), ())), preferred_element_type=F32)


def _dsa_prompt(q_b, k_hm, vt_b, qi_b, ki_b, wi, nb, s, lidx, w_attn_out):
    tq = _TQ
    nq = s // tq
    cols = Q_PER_KV * tq

    def pair_major(a, n_pairs):
        a = a.reshape(nb, nq, tq, n_pairs, 2, a.shape[-1] // (2 * n_pairs)).transpose(0, 1, 3, 4, 2, 5)
        return a.reshape(nb, nq, n_pairs, 2 * tq, -1)

    kh = k_hm
    vt = vt_b.reshape(nb, N_KV_HEADS, HEAD_DIM, s)
    wit = wi.reshape(nb, s, N_IDX_HEADS).transpose(0, 2, 1)
    once = dict(pipeline_mode=pl.Buffered(1))
    y = pl.pallas_call(
        _dsa_prompt_kernel,
        out_shape=jax.ShapeDtypeStruct((nb, s, D_MODEL), F32),
        grid_spec=pltpu.PrefetchScalarGridSpec(
            num_scalar_prefetch=1, grid=(nb, nq),
            in_specs=[pl.BlockSpec((1, 1, _IDX_PAIRS, 2 * tq, IDX_DIM), lambda b, i, l: (b, i, 0, 0, 0)),
                      pl.BlockSpec((1, N_IDX_HEADS, tq), lambda b, i, l: (b, 0, i)),
                      pl.BlockSpec((1, s, IDX_DIM), lambda b, i, l: (b, 0, 0), **once),
                      pl.BlockSpec((1, 1, N_KV_HEADS, cols, HEAD_DIM), lambda b, i, l: (b, i, 0, 0, 0)),
                      pl.BlockSpec((1, N_KV_HEADS, s, HEAD_DIM), lambda b, i, l: (b, 0, 0, 0), **once),
                      pl.BlockSpec((1, N_KV_HEADS, HEAD_DIM, s), lambda b, i, l: (b, 0, 0, 0), **once),
                      _lspec(ATTN_W, D_MODEL)],
            out_specs=pl.BlockSpec((1, tq, D_MODEL), lambda b, i, l: (b, i, 0)),
            scratch_shapes=[pltpu.VMEM((s, tq), F32), pltpu.VMEM((_SUBLANES, tq), F32),
                            pltpu.VMEM((N_KV_HEADS, 1, cols), F32),
                            pltpu.VMEM((N_KV_HEADS, HEAD_DIM + _ONES_ROWS, cols), F32),
                            pltpu.VMEM((2, N_KV_HEADS, _TK, cols), F32), pltpu.VMEM((2, N_KV_HEADS, 1, cols), F32),
                            pltpu.VMEM((2, N_KV_HEADS, _TK, cols), BF16), pltpu.VMEM((2, N_KV_HEADS, 1, cols), F32)]),
        compiler_params=_params("parallel", "arbitrary"),
        name="dsa_prompt",
    )(lidx, pair_major(qi_b, _IDX_PAIRS), wit, ki_b.reshape(nb, s, IDX_DIM), pair_major(q_b, N_KV_HEADS), kh, vt,
      w_attn_out)
    return y.reshape(nb * s, D_MODEL)


_QPAD = 8
_PAGES_SC = 32
_PAGES_KV = 16


def _sample_scores_kernel(pt_ref, qi_ref, wi_ref, knt_ref, *rest):
    pages, sc_ref = rest[:_PAGES_SC], rest[_PAGES_SC]
    j = pl.program_id(1)
    last = pl.num_programs(1) - 1

    def scores(kt):
        d = jnp.dot(qi_ref[0], kt, preferred_element_type=F32)
        score = None
        for h in range(N_IDX_HEADS):
            t = wi_ref[0, :, h:h + 1] * jnp.maximum(d[h * _QPAD:(h + 1) * _QPAD, :], 0.0)
            score = t if score is None else score + t
        return jnp.where(score == 0.0, 0.0, score)

    @pl.when(j < last)
    def _past():
        sc_ref[0] = scores(jnp.concatenate([pg[0] for pg in pages], axis=1).astype(BF16))

    @pl.when(j == last)
    def _new():
        sc_ref[0] = jnp.zeros(sc_ref.shape[1:], F32)
        sc_ref[0, :, 0:PAGE_SIZE] = scores(knt_ref[0])


def _sample_select_kernel(sc_in_ref, bias_ref, sc_ref, st_ref, *, past, dec_seq):
    tk = _TK
    n_chunks = sc_ref.shape[0] // tk
    q = lax.broadcasted_iota(I32, (1, LANES), 1) % dec_seq
    krow = lax.broadcasted_iota(I32, (tk, 1), 0)

    def load(c, carry):
        off = pl.multiple_of(c * tk, tk)
        s = sc_in_ref[pl.ds(off, tk), :]
        valid = off + krow <= past + q
        sc_ref[pl.ds(off, tk), :] = jnp.where(valid, s, -jnp.inf)
        return _min_max_update(*carry, s, valid)

    mn, mx = lax.fori_loop(0, n_chunks, load, _min_max_init(LANES))
    n_valid = (past + 1 + q).astype(F32)
    kt = jnp.minimum(n_valid, float(min(TOPK_MAX, (past + dec_seq) // 4)))
    thr = _select_threshold(sc_ref, st_ref, n_chunks, tk, kt, _reduce_rows(mn, jnp.min),
                            _reduce_rows(mx, jnp.max), n_valid)

    def emit(c, carry):
        off = pl.multiple_of(c * tk, tk)
        bias_ref[pl.ds(off, tk), :] = jnp.where(sc_ref[pl.ds(off, tk), :] >= thr, 0.0, -jnp.inf)
        return carry

    lax.fori_loop(0, n_chunks, emit, 0)


def _sample_attend_kernel(pt_ref, l_ref_, q_ref, bias_ref, knt_ref, vnt_ref, wao_ref, *rest):
    kpages, vpages = rest[:_PAGES_KV], rest[_PAGES_KV:2 * _PAGES_KV]
    y_ref, m_ref, l_ref, acc_ref = rest[2 * _PAGES_KV:]
    j = pl.program_id(1)
    last = pl.num_programs(1) - 1

    @pl.when(j == 0)
    def _init():
        m_ref[...] = jnp.full(m_ref.shape, _NEG, F32)
        l_ref[...] = jnp.zeros(l_ref.shape, F32)
        acc_ref[...] = jnp.zeros(acc_ref.shape, F32)

    def attend(kt, vt, bias):
        s = jnp.dot(q_ref[0], kt, preferred_element_type=F32)
        s = jnp.concatenate([s[h * _QPAD:(h + 1) * _QPAD] + bias for h in range(N_HEADS)], axis=0)
        m_old = m_ref[...]
        m_new = jnp.maximum(m_old, jnp.max(s, axis=1, keepdims=True))
        alpha = jnp.exp2(m_old - m_new)
        p = jnp.exp2(s - m_new)
        l_ref[...] = alpha * l_ref[...] + jnp.sum(p, axis=1, keepdims=True)
        acc_ref[...] = alpha * acc_ref[...] + lax.dot_general(p.astype(BF16), vt, _NT,
                                                              preferred_element_type=F32)
        m_ref[...] = m_new

    @pl.when(j < last)
    def _past():
        attend(jnp.concatenate([pg[0] for pg in kpages], axis=1).astype(BF16),
               jnp.concatenate([pg[0] for pg in vpages], axis=1).astype(BF16), bias_ref[0])

    @pl.when(j == last)
    def _new():
        attend(knt_ref[0], vnt_ref[0], bias_ref[0, :, 0:PAGE_SIZE])
        o = acc_ref[...] / l_ref[...]
        y = None
        for h in range(N_HEADS):
            g = h // Q_PER_KV
            o_h = o[h * _QPAD:(h + 1) * _QPAD, g * HEAD_DIM:(g + 1) * HEAD_DIM].astype(BF16)
            t = jnp.dot(o_h, wao_ref[h * HEAD_DIM:(h + 1) * HEAD_DIM, :], preferred_element_type=F32)
            y = t if y is None else y + t
        y_ref[0] = y


def _dsa_sample(q_b, k_b, v_b, qi_b, ki_b, wi, k_pool, v_pool, ki_pool, page_table, nb, s, lidx, w_attn_out):
    n_pages = page_table.shape[1]
    past = n_pages * PAGE_SIZE
    assert _QPAD % s == 0 and (nb * _QPAD) % LANES == 0
    assert n_pages % _PAGES_SC == 0 and n_pages % _PAGES_KV == 0
    dup = jnp.arange(_QPAD) % s
    qi = qi_b.reshape(nb, s, N_IDX_HEADS, IDX_DIM)[:, dup].transpose(0, 2, 1, 3)
    qi = qi.reshape(nb, N_IDX_HEADS * _QPAD, IDX_DIM)
    wi8 = wi.reshape(nb, s, N_IDX_HEADS)[:, dup]
    new_page = lambda a: jnp.pad(a.reshape(nb, s, -1).transpose(0, 2, 1), ((0, 0), (0, 0), (0, PAGE_SIZE - s)))
    kint, knt, vnt = new_page(ki_b), new_page(k_b), new_page(v_b)

    n_sc = n_pages // _PAGES_SC
    sc_w = _PAGES_SC * PAGE_SIZE

    def page_spec(width, per_step, r):
        return pl.BlockSpec((1, width, PAGE_SIZE),
                            lambda b, j, pt, *_: (pt[b, jnp.minimum(j * per_step + r, n_pages - 1)], 0, 0))

    scores = pl.pallas_call(
        _sample_scores_kernel,
        out_shape=jax.ShapeDtypeStruct((nb, _QPAD, (n_sc + 1) * sc_w), F32),
        grid_spec=pltpu.PrefetchScalarGridSpec(
            num_scalar_prefetch=1, grid=(nb, n_sc + 1),
            in_specs=[pl.BlockSpec((1, N_IDX_HEADS * _QPAD, IDX_DIM), lambda b, j, pt: (b, 0, 0)),
                      pl.BlockSpec((1, _QPAD, N_IDX_HEADS), lambda b, j, pt: (b, 0, 0)),
                      pl.BlockSpec((1, IDX_DIM, PAGE_SIZE), lambda b, j, pt: (b, 0, 0))]
                     + [page_spec(IDX_DIM, _PAGES_SC, r) for r in range(_PAGES_SC)],
            out_specs=pl.BlockSpec((1, _QPAD, sc_w), lambda b, j, pt: (b, 0, j))),
        compiler_params=_params("parallel", "arbitrary"),
        name="sample_scores",
    )(page_table, qi, wi8, kint, *([ki_pool] * _PAGES_SC))

    width = scores.shape[2]
    cols = nb * _QPAD
    bias = pl.pallas_call(
        functools.partial(_sample_select_kernel, past=past, dec_seq=s),
        out_shape=jax.ShapeDtypeStruct((width, cols), F32),
        grid=(cols // LANES,),
        in_specs=[pl.BlockSpec((width, LANES), lambda i: (0, i))],
        out_specs=pl.BlockSpec((width, LANES), lambda i: (0, i)),
        scratch_shapes=[pltpu.VMEM((width, LANES), F32), pltpu.VMEM((_SUBLANES, LANES), F32)],
        compiler_params=_params("parallel"),
        name="sample_select",
    )(scores.reshape(cols, width).T).T.reshape(nb, _QPAD, width)

    q4 = q_b.reshape(nb, s, N_KV_HEADS, Q_PER_KV, HEAD_DIM)[:, dup]
    eye = jnp.eye(N_KV_HEADS, dtype=q_b.dtype)
    qx = jnp.einsum("bqgjd,gk->bgjqkd", q4, eye).reshape(nb, N_HEADS * _QPAD, KV_W)
    n_kv = n_pages // _PAGES_KV
    kv_w = _PAGES_KV * PAGE_SIZE
    new_blk = past // kv_w
    hq = N_HEADS * _QPAD
    y = pl.pallas_call(
        _sample_attend_kernel,
        out_shape=jax.ShapeDtypeStruct((nb, _QPAD, D_MODEL), F32),
        grid_spec=pltpu.PrefetchScalarGridSpec(
            num_scalar_prefetch=2, grid=(nb, n_kv + 1),
            in_specs=[pl.BlockSpec((1, hq, KV_W), lambda b, j, *_: (b, 0, 0)),
                      pl.BlockSpec((1, _QPAD, kv_w), lambda b, j, *_: (b, 0, jnp.minimum(j, new_blk))),
                      pl.BlockSpec((1, KV_W, PAGE_SIZE), lambda b, j, *_: (b, 0, 0)),
                      pl.BlockSpec((1, KV_W, PAGE_SIZE), lambda b, j, *_: (b, 0, 0)),
                      _lspec(ATTN_W, D_MODEL)]
                     + [page_spec(KV_W, _PAGES_KV, r) for r in range(_PAGES_KV)] * 2,
            out_specs=pl.BlockSpec((1, _QPAD, D_MODEL), lambda b, j, *_: (b, 0, 0)),
            scratch_shapes=[pltpu.VMEM((hq, 1), F32), pltpu.VMEM((hq, 1), F32), pltpu.VMEM((hq, KV_W), F32)]),
        compiler_params=_params("parallel", "arbitrary"),
        name="sample_attend",
    )(page_table, lidx, qx, bias, knt, vnt, w_attn_out, *([k_pool] * _PAGES_KV), *([v_pool] * _PAGES_KV))
    return y[:, :s].reshape(nb * s, D_MODEL)


_ROUTE_W = 8


def _merge_kernel(l_ref, x_ref, yc_ref, ya_ref, g_ref, wout_ref, lg_ref, lb_ref, rwh_ref, rwl_ref,
                  rb_ref, x1_ref, x1b_ref, route_ref, *, alpha):
    mix_in = g_ref[:, :D_MODEL] * yc_ref[...] + g_ref[:, D_MODEL:] * ya_ref[...]
    mix = jnp.dot(mix_in.astype(BF16), wout_ref[...], preferred_element_type=F32)
    x1 = _layer_norm(alpha * x_ref[...] + mix, lg_ref[...], lb_ref[...])
    x1_ref[...] = x1
    hi = x1.astype(BF16)
    x1b_ref[...] = hi

    lo = (x1 - hi.astype(F32)).astype(BF16)
    logits = (jnp.dot(hi, rwh_ref[...], preferred_element_type=F32)
              + jnp.dot(hi, rwl_ref[...], preferred_element_type=F32)
              + jnp.dot(lo, rwh_ref[...], preferred_element_type=F32)) + rb_ref[...]
    lane = lax.broadcasted_iota(I32, (1, LANES), 1)

    def first_max(v):
        m = jnp.max(v, axis=1, keepdims=True)
        return m, jnp.min(jnp.where(v == m, lane, LANES), axis=1, keepdims=True)

    is_group = (lane >= N_EXPERTS) & (lane < N_EXPERTS + N_GROUPS)
    gl = jnp.where(is_group, logits, -jnp.inf)
    gm, gidx = first_max(gl)
    p_grp = 1.0 / jnp.sum(jnp.exp(gl - gm), axis=1, keepdims=True)
    grp = gidx - N_EXPERTS
    group_of_lane = lane >> (EXP_PER_GROUP.bit_length() - 1)
    el = jnp.where(group_of_lane == grp, logits, -jnp.inf)
    e1, i1 = first_max(el)
    e2, i2 = first_max(jnp.where(lane == i1, -jnp.inf, el))
    t = jnp.exp(e2 - e1)
    g1 = p_grp / (1.0 + t)
    g2 = g1 * t
    rl = lax.broadcasted_iota(I32, (1, _ROUTE_W), 1)
    route_ref[...] = jnp.where(rl == 0, i1.astype(F32), jnp.where(rl == 1, i2.astype(F32),
                               jnp.where(rl == 2, g1, jnp.where(rl == 3, g2, 0.0))))


def _merge(lidx, x2d, yc, ya, g, p, tm, alpha):
    t = x2d.shape[0]
    row = lambda n: pl.BlockSpec((tm, n), lambda i, l: (i, 0))
    return pl.pallas_call(
        functools.partial(_merge_kernel, alpha=alpha),
        out_shape=[jax.ShapeDtypeStruct((t, D_MODEL), F32), jax.ShapeDtypeStruct((t, D_MODEL), BF16),
                   jax.ShapeDtypeStruct((t, _ROUTE_W), F32)],
        grid_spec=pltpu.PrefetchScalarGridSpec(
            num_scalar_prefetch=1, grid=(t // tm,),
            in_specs=[row(D_MODEL), row(D_MODEL), row(D_MODEL), row(2 * D_MODEL),
                      _lspec(D_MODEL, D_MODEL), _lspec(1, D_MODEL),
                      _lspec(1, D_MODEL), _lspec(D_MODEL, LANES), _lspec(D_MODEL, LANES), _lspec(1, LANES)],
            out_specs=[row(D_MODEL), row(D_MODEL), row(_ROUTE_W)]),
        compiler_params=_params("parallel"),
        name="merge_ln1_router",
    )(lidx, x2d, yc, ya, g, p["w_out"], p["ln1_g"], p["ln1_b"], p["router_hi"],
      p["router_lo"], p["router_b"])


_SLOT_ALIGN = SUBLANES_BF16
_ROW_BLK = 128
_EXPERTS_PER_STEP = 4
_MOE_CHUNK = 256
_META_W = LANES
_MOE_TILE_MAX = 256 * _SLOT_ALIGN


def _moe_plan_kernel(route_ref, dcol_ref, drow_ref, meta_ref):
    tm = route_ref.shape[0]
    blk_rows = min(_MOE_CHUNK, tm)
    lane = lax.broadcasted_iota(I32, (1, LANES), 1).astype(F32)
    hit0 = lane == route_ref[:, 0:1]
    hit1 = lane == route_ref[:, 1:2]
    onehot = jnp.where(hit0, 1.0, jnp.where(hit1, 1.0, 0.0))
    r = lax.broadcasted_iota(I32, (blk_rows, blk_rows), 0)
    c = lax.broadcasted_iota(I32, (blk_rows, blk_rows), 1)
    tri = jnp.where(c < r, 1.0, 0.0).astype(BF16)
    carry = jnp.zeros((1, LANES), F32)
    prefix = []
    for b0 in range(0, tm, blk_rows):
        blk = onehot[b0:b0 + blk_rows]
        prefix.append(jnp.dot(tri, blk.astype(BF16), preferred_element_type=F32) + carry)
        carry = carry + jnp.sum(blk, axis=0, keepdims=True)
    prefix = jnp.concatenate(prefix, axis=0)
    units = jnp.ceil(carry * (1.0 / _SLOT_ALIGN))
    rr = lax.broadcasted_iota(I32, (LANES, LANES), 0)
    cc = lax.broadcasted_iota(I32, (LANES, LANES), 1)
    upper = jnp.where(rr < cc, 1.0, 0.0).astype(BF16)
    units8 = jnp.broadcast_to(units, (8, LANES)).astype(BF16)
    offs = jnp.dot(units8, upper, preferred_element_type=F32)[0:1] * _SLOT_ALIGN
    slot = offs + prefix
    d0 = jnp.sum(jnp.where(hit0, slot, 0.0), axis=1, keepdims=True)
    d1 = jnp.sum(jnp.where(hit1, slot, 0.0), axis=1, keepdims=True)
    rl = lax.broadcasted_iota(I32, (1, LANES), 1)
    rec = jnp.where(rl == 0, d0, jnp.where(rl == 1, d1, jnp.where(rl == 2, route_ref[:, 2:3],
                    jnp.where(rl == 3, route_ref[:, 3:4], 0.0))))
    dcol_ref[...] = rec[:, :_ROUTE_W]
    drow_ref[...] = rec.T[:_ROUTE_W, :]
    nblk = jnp.ceil(units * (_SLOT_ALIGN / _ROW_BLK))
    shift = jnp.where(cc == rr + N_EXPERTS, 1.0, 0.0).astype(BF16)
    nblk_sh = jnp.dot(jnp.broadcast_to(nblk, (8, LANES)).astype(BF16), shift,
                      preferred_element_type=F32)[0:1]
    meta_ref[0] = jnp.where(rl < N_EXPERTS, offs, nblk_sh).astype(I32)


def _moe_plan(route, tm):
    t = route.shape[0]
    nt = t // tm
    assert tm <= _MOE_TILE_MAX
    return pl.pallas_call(
        _moe_plan_kernel,
        out_shape=[jax.ShapeDtypeStruct((t, _ROUTE_W), F32), jax.ShapeDtypeStruct((_ROUTE_W, t), F32),
                   jax.ShapeDtypeStruct((nt, 1, _META_W), I32)],
        grid=(nt,),
        in_specs=[pl.BlockSpec((tm, _ROUTE_W), lambda i: (i, 0))],
        out_specs=[pl.BlockSpec((tm, _ROUTE_W), lambda i: (i, 0)),
                   pl.BlockSpec((_ROUTE_W, tm), lambda i: (0, i)),
                   pl.BlockSpec((1, 1, _META_W), lambda i: (i, 0, 0))],
        compiler_params=_params("parallel"),
        name="moe_plan",
    )(route)


def _n_slots(tm):
    n = 2 * tm + N_EXPERTS * (_SLOT_ALIGN - 1) + _ROW_BLK
    return -(-n // _MOE_CHUNK) * _MOE_CHUNK


def _moe_kernel(meta_ref, l_ref, x1b_ref, x1_ref, drow_ref, dcol_ref, wg_ref, wu_ref, wd_ref, lg_ref, lb_ref,
                x2_ref, xb_ref, yb_ref, gs_ref, *, alpha):
    i, e = pl.program_id(0), pl.program_id(1)
    tm = x1_ref.shape[0]
    n_slots = xb_ref.shape[0]

    @pl.when(e == 0)
    def _dispatch():
        d0, d1 = drow_ref[0:1, :], drow_ref[1:2, :]
        g0, g1 = drow_ref[2:3, :], drow_ref[3:4, :]
        for s0 in range(0, n_slots, _MOE_CHUNK):
            sl = (s0 + lax.broadcasted_iota(I32, (_MOE_CHUNK, 1), 0)).astype(F32)
            a, b = sl == d0, sl == d1
            p = jnp.where(a, 1.0, jnp.where(b, 1.0, 0.0)).astype(BF16)
            xb_ref[s0:s0 + _MOE_CHUNK, :] = jnp.dot(p, x1b_ref[...], preferred_element_type=F32).astype(BF16)
            gs_ref[s0:s0 + _MOE_CHUNK, :] = jnp.sum(jnp.where(a, g0, jnp.where(b, g1, 0.0)), axis=1,
                                                    keepdims=True)
        yb_ref[...] = jnp.zeros(yb_ref.shape, BF16)

    for k in range(_EXPERTS_PER_STEP):
        expert = e * _EXPERTS_PER_STEP + k
        off = meta_ref[i * _META_W + expert]
        nblk = meta_ref[i * _META_W + N_EXPERTS + expert]

        def block(r, carry, k=k, off=off):
            start = pl.multiple_of(off + r * _ROW_BLK, _SLOT_ALIGN)
            xs = xb_ref[pl.ds(start, _ROW_BLK), :]
            hg = jnp.dot(xs, wg_ref[k], preferred_element_type=F32)
            hu = jnp.dot(xs, wu_ref[k], preferred_element_type=F32)
            h = (hg * jax.nn.sigmoid(hg)) * hu
            y = jnp.dot(h.astype(BF16), wd_ref[k], preferred_element_type=F32)
            yb_ref[pl.ds(start, _ROW_BLK), :] = (y * gs_ref[pl.ds(start, _ROW_BLK), :]).astype(BF16)
            return carry

        lax.fori_loop(0, nblk, block, 0)

    @pl.when(e == pl.num_programs(1) - 1)
    def _combine():
        lane = lax.broadcasted_iota(I32, (1, n_slots), 1).astype(F32)
        rows = min(_MOE_CHUNK, tm)
        for t0 in range(0, tm, rows):
            d0, d1 = dcol_ref[t0:t0 + rows, 0:1], dcol_ref[t0:t0 + rows, 1:2]
            qm = jnp.where(lane == d0, 1.0, jnp.where(lane == d1, 1.0, 0.0)).astype(BF16)
            y = jnp.dot(qm, yb_ref[...], preferred_element_type=F32)
            z = alpha * x1_ref[t0:t0 + rows, :] + y
            x2_ref[t0:t0 + rows, :] = _layer_norm(z, lg_ref[...], lb_ref[...])


def _moe(lidx, x1, x1b, drow, dcol, meta, p, tm, alpha):
    t = x1.shape[0]
    n_slots = _n_slots(tm)
    tile = lambda shape, imap: pl.BlockSpec(shape, imap)
    return pl.pallas_call(
        functools.partial(_moe_kernel, alpha=alpha),
        out_shape=jax.ShapeDtypeStruct((t, D_MODEL), F32),
        grid_spec=pltpu.PrefetchScalarGridSpec(
            num_scalar_prefetch=2, grid=(t // tm, N_EXPERTS // _EXPERTS_PER_STEP),
            in_specs=[tile((tm, D_MODEL), lambda i, e, m, l: (i, 0)),
                      tile((tm, D_MODEL), lambda i, e, m, l: (i, 0)),
                      tile((_ROUTE_W, tm), lambda i, e, m, l: (0, i)),
                      tile((tm, _ROUTE_W), lambda i, e, m, l: (i, 0)),
                      tile((None, _EXPERTS_PER_STEP, D_MODEL, D_EXPERT), lambda i, e, m, l: (l[0], e, 0, 0)),
                      tile((None, _EXPERTS_PER_STEP, D_MODEL, D_EXPERT), lambda i, e, m, l: (l[0], e, 0, 0)),
                      tile((None, _EXPERTS_PER_STEP, D_EXPERT, D_MODEL), lambda i, e, m, l: (l[0], e, 0, 0)),
                      _lspec(1, D_MODEL), _lspec(1, D_MODEL)],
            out_specs=tile((tm, D_MODEL), lambda i, e, m, l: (i, 0)),
            scratch_shapes=[pltpu.VMEM((n_slots, D_MODEL), BF16), pltpu.VMEM((n_slots, D_MODEL), BF16),
                            pltpu.VMEM((n_slots, 1), F32)]),
        compiler_params=_params("parallel", "arbitrary"),
        name="moe_experts_ln2",
    )(meta.reshape(-1), lidx, x1b, x1, drow, dcol, p["w_gate"], p["w_up"], p["w_down"], p["ln2_g"], p["ln2_b"])


def _token_tile(t, want):
    return want if t % want == 0 else t


def _decoder_layer(lidx, x, prefix, attend, p, alpha, key_transposed):
    nb, s, _ = x.shape
    t = nb * s
    x2d = x.reshape(t, D_MODEL)
    u, q_b, k, v, qi_b, ki, wi, g, k_b, v_b, ki_b = _inproj(lidx, x2d, p["w_in"], p["b_in"], _token_tile(t, 512),
                                                            (nb, s) if key_transposed else None)

    ctm = 512 if s % 512 == 0 else _HALO
    u3 = u.reshape(nb, s, C_CONV)
    history = jnp.pad(prefix, ((0, 0), (_HALO - (CONV_W - 1), 0), (0, 0)))
    u_rows = jnp.pad(u3, ((0, 0), (0, (-s) % ctm), (0, 0)))
    yc = _conv_branch(lidx, history, u_rows, p["conv_w"], p["conv_b"], p["conv_ln_g"], p["conv_ln_b"],
                      p["w_conv_out"], ctm)
    yc = yc[:, :s].reshape(t, D_MODEL)

    ya = attend(q_b, k_b, v_b, qi_b, ki_b, wi, lidx=lidx, w_attn_out=p["w_attn_out"])
    x1, x1b, route = _merge(lidx, x2d, yc, ya, g, p, _token_tile(t, 256), alpha)
    mtm = _token_tile(t, 1024)
    dcol, drow, meta = _moe_plan(route, mtm)
    x2 = _moe(lidx, x1, x1b, drow, dcol, meta, p, mtm, alpha)

    new_conv = jnp.concatenate([prefix, u3], axis=1)[:, -(CONV_W - 1):]
    if key_transposed:
        per_head = lambda a: a.reshape(nb, N_KV_HEADS, HEAD_DIM, s).transpose(0, 3, 1, 2)
    else:
        per_head = lambda a: a.reshape(nb, s, N_KV_HEADS, HEAD_DIM)
    return x2.reshape(nb, s, D_MODEL), per_head(k), per_head(v), ki.reshape(nb, s, IDX_DIM), new_conv


def _prepare_params(w_in, b_in, conv_w, conv_b, conv_ln_g, conv_ln_b, w_conv_out, w_attn_out, w_out,
                    ln1_g, ln1_b, router_group_w, router_group_b, router_expert_w, router_expert_b,
                    w_gate, w_up, w_down, ln2_g, ln2_b):
    depth = w_in.shape[0]
    pad = _COL_G - _N_IN_HEAD
    w_pad = jnp.concatenate([w_in[..., :_N_IN_HEAD], jnp.zeros((depth, D_MODEL, pad), F32),
                             w_in[..., _N_IN_HEAD:]], axis=-1).astype(BF16)
    b_pad = jnp.concatenate([b_in[..., :_N_IN_HEAD], jnp.zeros((depth, pad), F32),
                             b_in[..., _N_IN_HEAD:]], axis=-1)[:, None, :]
    rpad = LANES - N_EXPERTS - N_GROUPS
    rw = jnp.concatenate([router_expert_w, router_group_w, jnp.zeros((depth, D_MODEL, rpad), F32)], axis=-1)
    rb = jnp.concatenate([router_expert_b, router_group_b, jnp.zeros((depth, rpad), F32)], axis=-1)
    rw_hi = rw.astype(BF16)
    rw_lo = (rw - rw_hi.astype(F32)).astype(BF16)
    vec = lambda a: a[:, None, :]
    return dict(w_in=w_pad, b_in=b_pad, conv_w=conv_w, conv_b=vec(conv_b), conv_ln_g=vec(conv_ln_g),
                conv_ln_b=vec(conv_ln_b), w_conv_out=w_conv_out.astype(BF16),
                w_attn_out=w_attn_out.astype(BF16), w_out=w_out.astype(BF16), ln1_g=vec(ln1_g),
                ln1_b=vec(ln1_b), router_hi=rw_hi, router_lo=rw_lo, router_b=vec(rb),
                w_gate=w_gate.astype(BF16), w_up=w_up.astype(BF16), w_down=w_down.astype(BF16),
                ln2_g=vec(ln2_g), ln2_b=vec(ln2_b))


def kernel(x_prompt, x_sample, cache_k, cache_v, cache_kidx, state_conv, page_table, w_in, b_in, conv_w,
           conv_b, conv_ln_g, conv_ln_b, w_conv_out, w_attn_out, w_out, ln1_g, ln1_b, router_group_w,
           router_group_b, router_expert_w, router_expert_b, w_gate, w_up, w_down, ln2_g, ln2_b):
    params = _prepare_params(w_in, b_in, conv_w, conv_b, conv_ln_g, conv_ln_b, w_conv_out, w_attn_out,
                             w_out, ln1_g, ln1_b, router_group_w, router_group_b, router_expert_w,
                             router_expert_b, w_gate, w_up, w_down, ln2_g, ln2_b)
    nb, s, _ = x_prompt.shape
    db, ds, _ = x_sample.shape
    depth, n_phys = cache_k.shape[:2]
    alpha = (2 * depth) ** 0.25
    k_pool = cache_k.transpose(0, 1, 3, 4, 2).reshape(depth * n_phys, KV_W, PAGE_SIZE)
    v_pool = cache_v.transpose(0, 1, 3, 4, 2).reshape(depth * n_phys, KV_W, PAGE_SIZE)
    ki_pool = cache_kidx.transpose(0, 1, 3, 2).reshape(depth * n_phys, IDX_DIM, PAGE_SIZE)
    conv_zero = jnp.zeros((nb, CONV_W - 1, C_CONV), F32)

    def layer(carry, xs):
        xp, xsm = carry
        l, st = xs
        lidx = l.reshape(1)
        prompt_attend = functools.partial(_dsa_prompt, nb=nb, s=s)
        xp, *new_p = _decoder_layer(lidx, xp, conv_zero, prompt_attend, params, alpha, key_transposed=True)
        sample_attend = functools.partial(_dsa_sample, k_pool=k_pool, v_pool=v_pool, ki_pool=ki_pool,
                                          page_table=page_table + l * n_phys, nb=db, s=ds)
        xsm, *new_s = _decoder_layer(lidx, xsm, st, sample_attend, params, alpha, key_transposed=False)
        return (xp, xsm), (tuple(new_p), tuple(new_s))

    (xp, xsm), (new_p, new_s) = lax.scan(layer, (x_prompt, x_sample),
                                         (jnp.arange(depth, dtype=I32), state_conv))
    return (xp, xsm, *new_p, *new_s)
```

```python
import functools

import jax
import jax.numpy as jnp
from jax import lax
from jax.experimental import pallas as pl
from jax.experimental.pallas import tpu as pltpu

F32 = jnp.float32
BF16 = jnp.bfloat16
I32 = jnp.int32

D_MODEL = 1024
PAGE_SIZE = 128
C_CONV = 512
CONV_W = 31
N_HEADS = 8
N_KV_HEADS = 4
HEAD_DIM = 64
Q_PER_KV = N_HEADS // N_KV_HEADS
ATTN_W = N_HEADS * HEAD_DIM
KV_W = N_KV_HEADS * HEAD_DIM
N_IDX_HEADS = 8
IDX_DIM = 64
TOPK_MAX = 256
N_GROUPS = 4
EXP_PER_GROUP = 8
N_EXPERTS = N_GROUPS * EXP_PER_GROUP
D_EXPERT = 256
LN_EPS = 1e-5

LANES = 128
_SUBLANES = 8
SUBLANES_BF16 = 16
VMEM_LIMIT = 56 * 1024 * 1024

_N_SMALL = IDX_DIM + N_IDX_HEADS
_COL_CA = 0
_COL_CB = _COL_CA + C_CONV
_COL_Q = _COL_CB + C_CONV
_COL_K = _COL_Q + ATTN_W
_COL_V = _COL_K + KV_W
_COL_QI = _COL_V + KV_W
_COL_KI = _COL_QI + N_IDX_HEADS * IDX_DIM
_COL_G = _COL_KI + LANES
_N_IN_PAD = _COL_G + 2 * D_MODEL
_N_IN_HEAD = _COL_KI + _N_SMALL

_NT = (((1,), (1,)), ((), ()))
_NEG = -1e30
_LOG2E = 1.4426950408889634


def _params(*sem):
    return pltpu.CompilerParams(dimension_semantics=sem, vmem_limit_bytes=VMEM_LIMIT)


def _lspec(*shape):
    nd = len(shape)
    return pl.BlockSpec((None,) + shape, lambda *a: (a[-1][0],) + (0,) * nd)


def _layer_norm(x, g, b):
    mu = jnp.mean(x, axis=-1, keepdims=True)
    xc = x - mu
    var = jnp.mean(xc * xc, axis=-1, keepdims=True)
    return xc * lax.rsqrt(var + LN_EPS) * g + b


def _inproj_kernel(l_ref, x_ref, w_ref, b_ref, u_ref, q_ref, k_ref, v_ref, qi_ref, ki_ref, wi_ref, g_ref,
                   kb_ref, vb_ref, kib_ref, *, key_transposed):
    xb = x_ref[...].astype(BF16)

    def proj(c0, n):
        return jnp.dot(xb, w_ref[:, c0:c0 + n], preferred_element_type=F32) + b_ref[:, c0:c0 + n]

    u_ref[...] = proj(_COL_CA, C_CONV) * jax.nn.sigmoid(proj(_COL_CB, C_CONV))
    q_ref[...] = (proj(_COL_Q, ATTN_W) * (HEAD_DIM ** -0.5 * _LOG2E)).astype(BF16)
    k = proj(_COL_K, KV_W)
    v = proj(_COL_V, KV_W)
    if key_transposed:
        k_ref[0] = k.T
        v_t = v.T
        v_ref[0] = v_t
        vb_ref[0] = v_t.astype(BF16)
        for g in range(N_KV_HEADS):
            kb_ref[0, g] = k[:, g * HEAD_DIM:(g + 1) * HEAD_DIM].astype(BF16)
    else:
        k_ref[...] = k
        kb_ref[...] = k.astype(BF16)
        v_ref[...] = v
        vb_ref[...] = v.astype(BF16)
    qi_ref[...] = (proj(_COL_QI, N_IDX_HEADS * IDX_DIM) * (IDX_DIM ** -0.5)).astype(BF16)
    small = proj(_COL_KI, LANES)
    ki = small[:, :IDX_DIM]
    ki_ref[...] = ki
    kib_ref[...] = ki.astype(BF16)
    wi_ref[...] = small[:, IDX_DIM:_N_SMALL] * (N_IDX_HEADS ** -0.5)
    g_ref[...] = jax.nn.sigmoid(proj(_COL_G, 2 * D_MODEL)).astype(BF16)


def _inproj(lidx, x2d, w_pad, b_pad, tm, seq=None):
    t = x2d.shape[0]
    rows = lambda n, dt: (jax.ShapeDtypeStruct((t, n), dt), pl.BlockSpec((tm, n), lambda i, l: (i, 0)))
    if seq is None:
        k, v, kb, vb = rows(KV_W, F32), rows(KV_W, F32), rows(KV_W, BF16), rows(KV_W, BF16)
    else:
        nb, s = seq
        tps = s // tm
        kt = lambda dt: (jax.ShapeDtypeStruct((nb, KV_W, s), dt),
                         pl.BlockSpec((1, KV_W, tm), lambda i, l: (i // tps, 0, i % tps)))
        k, v, vb = kt(F32), kt(F32), kt(BF16)
        kb = (jax.ShapeDtypeStruct((nb, N_KV_HEADS, s, HEAD_DIM), BF16),
              pl.BlockSpec((1, N_KV_HEADS, tm, HEAD_DIM), lambda i, l: (i // tps, 0, i % tps, 0)))
    outs = [rows(C_CONV, F32), rows(ATTN_W, BF16), k, v, rows(N_IDX_HEADS * IDX_DIM, BF16), rows(IDX_DIM, F32),
            rows(N_IDX_HEADS, F32), rows(2 * D_MODEL, BF16), kb, vb, rows(IDX_DIM, BF16)]
    return pl.pallas_call(
        functools.partial(_inproj_kernel, key_transposed=seq is not None),
        out_shape=[o[0] for o in outs],
        grid_spec=pltpu.PrefetchScalarGridSpec(
            num_scalar_prefetch=1, grid=(t // tm,),
            in_specs=[pl.BlockSpec((tm, D_MODEL), lambda i, l: (i, 0)),
                      _lspec(D_MODEL, _N_IN_PAD), _lspec(1, _N_IN_PAD)],
            out_specs=[o[1] for o in outs]),
        compiler_params=_params("parallel"),
        name="inproj",
    )(lidx, x2d, w_pad, b_pad)


_HALO = 32
_CONV_ROWS = 32


def _conv_kernel(l_ref, first_ref, prev_ref, cur_ref, cw_ref, cb_ref, lg_ref, lb_ref, wo_ref, y_ref,
                 hist_ref, acc_ref, shift_ref):
    tm = cur_ref.shape[1]
    at_start = pl.program_id(1) == 0
    hist_ref[0:_HALO, :] = jnp.where(at_start, first_ref[0], prev_ref[0])
    hist_ref[_HALO:_HALO + tm, :] = cur_ref[0]
    first = _HALO - (CONV_W - 1)
    for b in range(_SUBLANES):
        rows = tm + _SUBLANES * ((CONV_W - 1 - b) // _SUBLANES)
        shift_ref[b, 0:rows, :] = hist_ref[first + b:first + b + rows, :]
    for r0 in range(0, tm, _CONV_ROWS):
        acc = jnp.zeros((_CONV_ROWS, C_CONV), F32)
        for j in range(CONV_W):
            a, b = divmod(j, _SUBLANES)
            acc = acc + cw_ref[j:j + 1, :] * shift_ref[b, r0 + a * _SUBLANES:r0 + a * _SUBLANES + _CONV_ROWS, :]
        acc_ref[r0:r0 + _CONV_ROWS, :] = acc + cb_ref[...]
    y = _layer_norm(acc_ref[...], lg_ref[...], lb_ref[...])
    y = y * jax.nn.sigmoid(y)
    y_ref[0] = jnp.dot(y.astype(BF16), wo_ref[...], preferred_element_type=F32).astype(BF16)


def _conv_branch(lidx, history, u, conv_w, conv_b, ln_g, ln_b, w_out_bf, tm):
    nb, length, _ = u.shape
    halo_blocks = tm // _HALO
    return pl.pallas_call(
        _conv_kernel,
        out_shape=jax.ShapeDtypeStruct((nb, length, D_MODEL), BF16),
        grid_spec=pltpu.PrefetchScalarGridSpec(
            num_scalar_prefetch=1, grid=(nb, length // tm),
            in_specs=[pl.BlockSpec((1, _HALO, C_CONV), lambda b, i, l: (b, 0, 0)),
                      pl.BlockSpec((1, _HALO, C_CONV), lambda b, i, l: (b, jnp.maximum(i * halo_blocks - 1, 0), 0)),
                      pl.BlockSpec((1, tm, C_CONV), lambda b, i, l: (b, i, 0)),
                      _lspec(CONV_W, C_CONV), _lspec(1, C_CONV), _lspec(1, C_CONV), _lspec(1, C_CONV),
                      _lspec(C_CONV, D_MODEL)],
            out_specs=pl.BlockSpec((1, tm, D_MODEL), lambda b, i, l: (b, i, 0)),
            scratch_shapes=[pltpu.VMEM((_HALO + tm, C_CONV), F32), pltpu.VMEM((tm, C_CONV), F32),
                            pltpu.VMEM((_SUBLANES, tm + _HALO - _SUBLANES, C_CONV), F32)]),
        compiler_params=_params("parallel", "parallel"),
        name="conv_branch",
    )(lidx, history, u, u, conv_w, conv_b, ln_g, ln_b, w_out_bf)


_BISECT_STEPS = 14


_FOLD_ROWS = 64


def _reduce_rows(x, op):
    rows, cols = x.shape
    if rows > _FOLD_ROWS:
        x = op(x.reshape(rows // _FOLD_ROWS, _FOLD_ROWS, cols), axis=0)
    x = op(x.reshape(x.shape[0] // _SUBLANES, _SUBLANES, cols), axis=0)
    return op(x, axis=0, keepdims=True)


def _fold_keys(sc_ref, n_chunks, tk, init, fn, op, merge, first=0):
    def body(c, acc):
        off = pl.multiple_of(c * tk, tk)
        v = fn(sc_ref[pl.ds(off, tk), :], off).reshape(tk // _FOLD_ROWS, _FOLD_ROWS, cols)
        return merge(acc, op(v, axis=0))

    cols = sc_ref.shape[1]
    acc = lax.fori_loop(first, n_chunks, body, jnp.full((_FOLD_ROWS, cols), init, F32))
    return _reduce_rows(acc, op)


def _count_keys(sc_ref, n_chunks, tk, fn, first=0):
    return _fold_keys(sc_ref, n_chunks, tk, 0.0, fn, jnp.sum, jnp.add, first)


def _count_ge(sc_ref, n_chunks, tk, thr):
    return _count_keys(sc_ref, n_chunks, tk, lambda blk, off: jnp.where(blk >= thr, 1.0, 0.0))


def _any(flag):
    return jnp.max(jnp.where(flag, 1, 0))


def _select_threshold(sc_ref, st_ref, n_chunks, tk, kt, s_min, s_max, n_valid):
    lo_ref, hi_ref, cl_ref, done_ref, jlo_ref, jhi_ref = (st_ref.at[n:n + 1] for n in range(6))
    c_max = _count_ge(sc_ref, n_chunks, tk, s_max)
    top = c_max >= kt
    lo_ref[...] = jnp.where(top, s_max, s_min)
    hi_ref[...] = s_max
    cl0 = jnp.where(top, c_max, n_valid)
    cl_ref[...] = cl0
    done0 = jnp.where(top, 1.0, jnp.where(cl0 == kt, 1.0, 0.0))
    done_ref[...] = done0

    def bisect_step():
        lo, hi, cl, done = lo_ref[...], hi_ref[...], cl_ref[...], done_ref[...]
        mid = 0.5 * lo + 0.5 * hi
        c = _count_ge(sc_ref, n_chunks, tk, mid)
        act = done == 0.0
        up = c >= kt
        lo_ref[...] = jnp.where(act, jnp.where(up, mid, lo), lo)
        cl_ref[...] = jnp.where(act, jnp.where(up, c, cl), cl)
        hi_ref[...] = jnp.where(act, jnp.where(up, hi, mid), hi)
        done_ref[...] = jnp.where(act, jnp.where(c == kt, 1.0, 0.0), done)

    @pl.when(_any(done0 == 0.0) > 0)
    def _narrow():
        lax.fori_loop(0, _BISECT_STEPS, lambda _, carry: (bisect_step(), carry)[1], 0)

    def snap(_):
        lo, hi, cl, done = lo_ref[...], hi_ref[...], cl_ref[...], done_ref[...]
        t1 = _fold_keys(sc_ref, n_chunks, tk, -jnp.inf, lambda blk, off: jnp.where(blk < hi, blk, -jnp.inf),
                        jnp.max, jnp.maximum)
        c1 = _count_ge(sc_ref, n_chunks, tk, t1)
        act = done == 0.0
        found = c1 >= kt
        lo_ref[...] = jnp.where(act, jnp.where(found, t1, lo), lo)
        cl_ref[...] = jnp.where(act, jnp.where(found, c1, cl), cl)
        hi_ref[...] = jnp.where(act, jnp.where(found, hi, t1), hi)
        done_new = jnp.where(act, jnp.where(found, 1.0, 0.0), done)
        done_ref[...] = done_new
        return _any(done_new == 0.0)

    lax.while_loop(lambda f: f > 0, snap, _any(done_ref[...] == 0.0))

    thr = lo_ref[...]

    @pl.when(_any(cl_ref[...] != kt) > 0)
    def _cut_ties():
        tied = cl_ref[...] != kt
        need = kt - _count_keys(sc_ref, n_chunks, tk, lambda blk, off: jnp.where(blk > thr, 1.0, 0.0))
        row = lax.broadcasted_iota(I32, (tk, 1), 0)

        def locate(c, carry):
            seen, chunk, seen_before = carry
            off = pl.multiple_of(c * tk, tk)
            cnt = _reduce_rows(jnp.where(sc_ref[pl.ds(off, tk), :] == thr, 1.0, 0.0), jnp.sum)
            here = jnp.where(seen < need, jnp.where(seen + cnt >= need, 1.0, 0.0), 0.0)
            chunk = jnp.where(here > 0.0, jnp.asarray(c, F32), chunk)
            seen_before = jnp.where(here > 0.0, seen, seen_before)
            return seen + cnt, chunk, seen_before

        zero = jnp.zeros_like(thr)
        _, chunk, seen_before = lax.fori_loop(0, n_chunks, locate, (zero, zero, zero))
        first_key = chunk * float(tk)
        need_here = need - seen_before
        c_first = jnp.min(jnp.where(tied, chunk, float(sc_ref.shape[0]))).astype(I32)
        c_last = jnp.max(jnp.where(tied, chunk, -1.0)).astype(I32)

        jlo_ref[...] = first_key - 1.0
        jhi_ref[...] = first_key + float(tk - 1)

        def step(_, carry):
            jlo, jhi = jlo_ref[...], jhi_ref[...]
            mid = jnp.floor(0.5 * (jlo + jhi))

            def in_range(blk, off):
                idx = (off + row).astype(F32)
                return jnp.where(blk == thr, jnp.where(idx <= mid, jnp.where(idx >= first_key, 1.0, 0.0), 0.0), 0.0)

            ok = _count_keys(sc_ref, c_last + 1, tk, in_range, c_first) >= need_here
            jhi_ref[...] = jnp.where(ok, mid, jhi)
            jlo_ref[...] = jnp.where(ok, jlo, mid)
            return carry

        lax.fori_loop(0, tk.bit_length() - 1, step, 0)
        cut = jnp.where(tied, jhi_ref[...], float(sc_ref.shape[0]))

        def demote(c, carry):
            off = pl.multiple_of(c * tk, tk)
            blk = sc_ref[pl.ds(off, tk), :]
            beyond = (off + row).astype(F32) > cut
            sc_ref[pl.ds(off, tk), :] = jnp.where(blk == thr, jnp.where(beyond, -jnp.inf, blk), blk)
            return carry

        lax.fori_loop(c_first, n_chunks, demote, 0)

    return thr


def _min_max_init(cols):
    return (jnp.full((_FOLD_ROWS, cols), jnp.inf, F32), jnp.full((_FOLD_ROWS, cols), -jnp.inf, F32))


def _min_max_update(mn, mx, score, valid):
    shape = (score.shape[0] // _FOLD_ROWS, _FOLD_ROWS, score.shape[1])
    mn = jnp.minimum(mn, jnp.min(jnp.where(valid, score, jnp.inf).reshape(shape), axis=0))
    mx = jnp.maximum(mx, jnp.max(jnp.where(valid, score, -jnp.inf).reshape(shape), axis=0))
    return mn, mx


_TQ = 256
_TK = 512


_IDX_PAIRS = N_IDX_HEADS // 2
_ONES_ROWS = SUBLANES_BF16


def _dsa_prompt_kernel(l_ref, qi_ref, wi_ref, ki_ref, q_ref, k_ref, vt_ref, wao_ref, y_ref,
                       sc_ref, st_ref, m_ref, acc_ref, s_ref, cm_ref, p_ref, al_ref):
    tq, tk = _TQ, _TK
    i = pl.program_id(1)
    n_chunks = ((i + 1) * tq + tk - 1) // tk
    qpos = i * tq + lax.broadcasted_iota(I32, (1, tq), 1)
    krow = lax.broadcasted_iota(I32, (tk, 1), 0)

    def scores(c, carry):
        off = pl.multiple_of(c * tk, tk)
        kc = ki_ref[0, pl.ds(off, tk), :]
        score = None
        for p in range(_IDX_PAIRS):
            d = lax.dot_general(kc, qi_ref[0, 0, p], _NT, preferred_element_type=F32)
            for j in range(2):
                h = 2 * p + j
                t = wi_ref[0, h:h + 1, :] * jnp.maximum(d[:, j * tq:(j + 1) * tq], 0.0)
                score = t if score is None else score + t
        valid = off + krow <= qpos
        sc_ref[pl.ds(off, tk), :] = jnp.where(valid, score, -jnp.inf)
        return _min_max_update(*carry, score, valid)

    mn, mx = lax.fori_loop(0, n_chunks, scores, _min_max_init(tq))
    n_valid = (qpos + 1).astype(F32)
    kt = jnp.minimum(n_valid, float(TOPK_MAX))
    thr = _select_threshold(sc_ref, st_ref, n_chunks, tk, kt, _reduce_rows(mn, jnp.min),
                            _reduce_rows(mx, jnp.max), n_valid)

    m_ref[...] = jnp.full(m_ref.shape, _NEG, F32)
    acc_ref[...] = jnp.zeros(acc_ref.shape, F32)

    def mask_bias(c):
        off = pl.multiple_of(c * tk, tk)
        bias = jnp.where(sc_ref[pl.ds(off, tk), :] >= thr, 0.0, -jnp.inf)
        return jnp.concatenate([bias] * Q_PER_KV, axis=1)

    def qk_stage(c, slot, groups, bias=None):
        off = pl.multiple_of(c * tk, tk)
        bias = mask_bias(c) if bias is None else bias
        for g in groups:
            s = lax.dot_general(k_ref[0, g, pl.ds(off, tk), :], q_ref[0, 0, g], _NT,
                                preferred_element_type=F32) + bias
            s_ref[slot, g] = s
            cm_ref[slot, g] = _reduce_rows(s, jnp.max)

    def exp_stage(slot, groups):
        for g in groups:
            m_old = m_ref[g]
            m_new = jnp.maximum(m_old, cm_ref[slot, g])
            al_ref[slot, g] = jnp.exp2(m_old - m_new)
            p_ref[slot, g] = jnp.exp2(s_ref[slot, g] - m_new).astype(BF16)
            m_ref[g] = m_new

    def pv_stage(c, slot, groups):
        off = pl.multiple_of(c * tk, tk)
        for g in groups:
            v_ones =jnp.concatenate([vt_ref[0, g, :, pl.ds(off, tk)], jnp.ones((_ONES_ROWS, tk), BF16)], axis=0)
            acc_ref[g] = al_ref[slot, g] * acc_ref[g] + jnp.dot(v_ones, p_ref[slot, g],
                                                                preferred_element_type=F32)

    every = range(N_KV_HEADS)

    def attend(c, carry):
        slot = c % 2
        for g in every:
            pv_stage(c, slot, (g,))
            qk_stage(c + 2, slot, (g,))
            exp_stage(1 - slot, (g,))
        return carry

    qk_stage(0, 0, every)
    exp_stage(0, every)

    @pl.when(n_chunks > 1)
    def _fill():
        qk_stage(1, 1, every)

    lax.fori_loop(0, n_chunks - 2, attend, 0)
    last = n_chunks - 1

    @pl.when(n_chunks > 1)
    def _drain():
        exp_stage(last % 2, every)
        pv_stage(last - 1, (last - 1) % 2, every)

    pv_stage(last, last % 2, every)
    heads = []
    for g in range(N_KV_HEADS):
        o_g = acc_ref[g, 0:HEAD_DIM, :] / acc_ref[g, HEAD_DIM:HEAD_DIM + 1, :]
        heads += [o_g[:, j * tq:(j + 1) * tq] for j in range(Q_PER_KV)]
    o_t = jnp.concatenate(heads, axis=0).astype(BF16)
    y_ref[0] = lax.dot_general(o_t, wao_ref[...], (((0,), (0,)), ((), ())),
                               preferred_element_type=F32).astype(BF16)


def _dsa_prompt(q_b, k_hm, vt_b, qi_b, ki_b, wi, nb, s, lidx, w_attn_out):
    tq = _TQ
    nq = s // tq
    cols = Q_PER_KV * tq

    def pair_major(a, n_pairs):
        a = a.reshape(nb, nq, tq, n_pairs, 2, a.shape[-1] // (2 * n_pairs)).transpose(0, 1, 3, 4, 2, 5)
        return a.reshape(nb, nq, n_pairs, 2 * tq, -1)

    kh = k_hm
    vt = vt_b.reshape(nb, N_KV_HEADS, HEAD_DIM, s)
    wit = wi.reshape(nb, s, N_IDX_HEADS).transpose(0, 2, 1)
    once = dict(pipeline_mode=pl.Buffered(1))
    y = pl.pallas_call(
        _dsa_prompt_kernel,
        out_shape=jax.ShapeDtypeStruct((nb, s, D_MODEL), BF16),
        grid_spec=pltpu.PrefetchScalarGridSpec(
            num_scalar_prefetch=1, grid=(nb, nq),
            in_specs=[pl.BlockSpec((1, 1, _IDX_PAIRS, 2 * tq, IDX_DIM), lambda b, i, l: (b, i, 0, 0, 0)),
                      pl.BlockSpec((1, N_IDX_HEADS, tq), lambda b, i, l: (b, 0, i)),
                      pl.BlockSpec((1, s, IDX_DIM), lambda b, i, l: (b, 0, 0), **once),
                      pl.BlockSpec((1, 1, N_KV_HEADS, cols, HEAD_DIM), lambda b, i, l: (b, i, 0, 0, 0)),
                      pl.BlockSpec((1, N_KV_HEADS, s, HEAD_DIM), lambda b, i, l: (b, 0, 0, 0), **once),
                      pl.BlockSpec((1, N_KV_HEADS, HEAD_DIM, s), lambda b, i, l: (b, 0, 0, 0), **once),
                      _lspec(ATTN_W, D_MODEL)],
            out_specs=pl.BlockSpec((1, tq, D_MODEL), lambda b, i, l: (b, i, 0)),
            scratch_shapes=[pltpu.VMEM((s, tq), F32), pltpu.VMEM((_SUBLANES, tq), F32),
                            pltpu.VMEM((N_KV_HEADS, 1, cols), F32),
                            pltpu.VMEM((N_KV_HEADS, HEAD_DIM + _ONES_ROWS, cols), F32),
                            pltpu.VMEM((2, N_KV_HEADS, _TK, cols), F32), pltpu.VMEM((2, N_KV_HEADS, 1, cols), F32),
                            pltpu.VMEM((2, N_KV_HEADS, _TK, cols), BF16), pltpu.VMEM((2, N_KV_HEADS, 1, cols), F32)]),
        compiler_params=_params("parallel", "arbitrary"),
        name="dsa_prompt",
    )(lidx, pair_major(qi_b, _IDX_PAIRS), wit, ki_b.reshape(nb, s, IDX_DIM), pair_major(q_b, N_KV_HEADS), kh, vt,
      w_attn_out)
    return y.reshape(nb * s, D_MODEL)


_QPAD = 8
_PAGES_SC = 32
_PAGES_KV = 16


def _sample_scores_kernel(pt_ref, qi_ref, wi_ref, knt_ref, *rest):
    pages, sc_ref = rest[:_PAGES_SC], rest[_PAGES_SC]
    j = pl.program_id(1)
    last = pl.num_programs(1) - 1

    def scores(kt):
        d = jnp.dot(qi_ref[0], kt, preferred_element_type=F32)
        score = None
        for h in range(N_IDX_HEADS):
            t = wi_ref[0, :, h:h + 1] * jnp.maximum(d[h * _QPAD:(h + 1) * _QPAD, :], 0.0)
            score = t if score is None else score + t
        return score

    @pl.when(j < last)
    def _past():
        sc_ref[0] = scores(jnp.concatenate([pg[0] for pg in pages], axis=1).astype(BF16))

    @pl.when(j == last)
    def _new():
        sc_ref[0] = jnp.zeros(sc_ref.shape[1:], F32)
        sc_ref[0, :, 0:PAGE_SIZE] = scores(knt_ref[0])


def _sample_select_kernel(sc_in_ref, bias_ref, sc_ref, st_ref, *, past, dec_seq):
    tk = _TK
    n_chunks = sc_ref.shape[0] // tk
    q = lax.broadcasted_iota(I32, (1, LANES), 1) % dec_seq
    krow = lax.broadcasted_iota(I32, (tk, 1), 0)

    def load(c, carry):
        off = pl.multiple_of(c * tk, tk)
        s = sc_in_ref[pl.ds(off, tk), :]
        valid = off + krow <= past + q
        sc_ref[pl.ds(off, tk), :] = jnp.where(valid, s, -jnp.inf)
        return _min_max_update(*carry, s, valid)

    mn, mx = lax.fori_loop(0, n_chunks, load, _min_max_init(LANES))
    n_valid = (past + 1 + q).astype(F32)
    kt = jnp.minimum(n_valid, float(min(TOPK_MAX, (past + dec_seq) // 4)))
    thr = _select_threshold(sc_ref, st_ref, n_chunks, tk, kt, _reduce_rows(mn, jnp.min),
                            _reduce_rows(mx, jnp.max), n_valid)

    def emit(c, carry):
        off = pl.multiple_of(c * tk, tk)
        bias_ref[pl.ds(off, tk), :] = jnp.where(sc_ref[pl.ds(off, tk), :] >= thr, 0.0, -jnp.inf)
        return carry

    lax.fori_loop(0, n_chunks, emit, 0)


def _sample_attend_kernel(pt_ref, l_ref_, q_ref, bias_ref, knt_ref, vnt_ref, wao_ref, *rest):
    kpages, vpages = rest[:_PAGES_KV], rest[_PAGES_KV:2 * _PAGES_KV]
    y_ref, m_ref, l_ref, acc_ref = rest[2 * _PAGES_KV:]
    j = pl.program_id(1)
    last = pl.num_programs(1) - 1

    @pl.when(j == 0)
    def _init():
        m_ref[...] = jnp.full(m_ref.shape, _NEG, F32)
        l_ref[...] = jnp.zeros(l_ref.shape, F32)
        acc_ref[...] = jnp.zeros(acc_ref.shape, F32)

    def attend(kt, vt, bias):
        s = jnp.dot(q_ref[0], kt, preferred_element_type=F32)
        s = jnp.concatenate([s[h * _QPAD:(h + 1) * _QPAD] + bias for h in range(N_HEADS)], axis=0)
        m_old = m_ref[...]
        m_new = jnp.maximum(m_old, jnp.max(s, axis=1, keepdims=True))
        alpha = jnp.exp2(m_old - m_new)
        p = jnp.exp2(s - m_new)
        l_ref[...] = alpha * l_ref[...] + jnp.sum(p, axis=1, keepdims=True)
        acc_ref[...] = alpha * acc_ref[...] + lax.dot_general(p.astype(BF16), vt, _NT,
                                                              preferred_element_type=F32)
        m_ref[...] = m_new

    @pl.when(j < last)
    def _past():
        attend(jnp.concatenate([pg[0] for pg in kpages], axis=1).astype(BF16),
               jnp.concatenate([pg[0] for pg in vpages], axis=1).astype(BF16), bias_ref[0])

    @pl.when(j == last)
    def _new():
        attend(knt_ref[0], vnt_ref[0], bias_ref[0, :, 0:PAGE_SIZE])
        o = acc_ref[...] / l_ref[...]
        y = None
        for h in range(N_HEADS):
            g = h // Q_PER_KV
            o_h = o[h * _QPAD:(h + 1) * _QPAD, g * HEAD_DIM:(g + 1) * HEAD_DIM].astype(BF16)
            t = jnp.dot(o_h, wao_ref[h * HEAD_DIM:(h + 1) * HEAD_DIM, :], preferred_element_type=F32)
            y = t if y is None else y + t
        y_ref[0] = y.astype(BF16)


def _dsa_sample(q_b, k_b, v_b, qi_b, ki_b, wi, k_pool, v_pool, ki_pool, page_table, nb, s, lidx, w_attn_out):
    n_pages = page_table.shape[1]
    past = n_pages * PAGE_SIZE
    assert _QPAD % s == 0 and (nb * _QPAD) % LANES == 0
    assert n_pages % _PAGES_SC == 0 and n_pages % _PAGES_KV == 0
    dup = jnp.arange(_QPAD) % s
    qi = qi_b.reshape(nb, s, N_IDX_HEADS, IDX_DIM)[:, dup].transpose(0, 2, 1, 3)
    qi = qi.reshape(nb, N_IDX_HEADS * _QPAD, IDX_DIM)
    wi8 = wi.reshape(nb, s, N_IDX_HEADS)[:, dup]
    new_page = lambda a: jnp.pad(a.reshape(nb, s, -1).transpose(0, 2, 1), ((0, 0), (0, 0), (0, PAGE_SIZE - s)))
    kint, knt, vnt = new_page(ki_b), new_page(k_b), new_page(v_b)

    n_sc = n_pages // _PAGES_SC
    sc_w = _PAGES_SC * PAGE_SIZE

    def page_spec(width, per_step, r):
        return pl.BlockSpec((1, width, PAGE_SIZE),
                            lambda b, j, pt, *_: (pt[b, jnp.minimum(j * per_step + r, n_pages - 1)], 0, 0))

    scores = pl.pallas_call(
        _sample_scores_kernel,
        out_shape=jax.ShapeDtypeStruct((nb, _QPAD, (n_sc + 1) * sc_w), F32),
        grid_spec=pltpu.PrefetchScalarGridSpec(
            num_scalar_prefetch=1, grid=(nb, n_sc + 1),
            in_specs=[pl.BlockSpec((1, N_IDX_HEADS * _QPAD, IDX_DIM), lambda b, j, pt: (b, 0, 0)),
                      pl.BlockSpec((1, _QPAD, N_IDX_HEADS), lambda b, j, pt: (b, 0, 0)),
                      pl.BlockSpec((1, IDX_DIM, PAGE_SIZE), lambda b, j, pt: (b, 0, 0))]
                     + [page_spec(IDX_DIM, _PAGES_SC, r) for r in range(_PAGES_SC)],
            out_specs=pl.BlockSpec((1, _QPAD, sc_w), lambda b, j, pt: (b, 0, j))),
        compiler_params=_params("parallel", "arbitrary"),
        name="sample_scores",
    )(page_table, qi, wi8, kint, *([ki_pool] * _PAGES_SC))

    width = scores.shape[2]
    cols = nb * _QPAD
    bias = pl.pallas_call(
        functools.partial(_sample_select_kernel, past=past, dec_seq=s),
        out_shape=jax.ShapeDtypeStruct((width, cols), F32),
        grid=(cols // LANES,),
        in_specs=[pl.BlockSpec((width, LANES), lambda i: (0, i))],
        out_specs=pl.BlockSpec((width, LANES), lambda i: (0, i)),
        scratch_shapes=[pltpu.VMEM((width, LANES), F32), pltpu.VMEM((_SUBLANES, LANES), F32)],
        compiler_params=_params("parallel"),
        name="sample_select",
    )(scores.reshape(cols, width).T).T.reshape(nb, _QPAD, width)

    q4 = q_b.reshape(nb, s, N_KV_HEADS, Q_PER_KV, HEAD_DIM)[:, dup]
    eye = jnp.eye(N_KV_HEADS, dtype=q_b.dtype)
    qx = jnp.einsum("bqgjd,gk->bgjqkd", q4, eye).reshape(nb, N_HEADS * _QPAD, KV_W)
    n_kv = n_pages // _PAGES_KV
    kv_w = _PAGES_KV * PAGE_SIZE
    new_blk = past // kv_w
    hq = N_HEADS * _QPAD
    y = pl.pallas_call(
        _sample_attend_kernel,
        out_shape=jax.ShapeDtypeStruct((nb, _QPAD, D_MODEL), BF16),
        grid_spec=pltpu.PrefetchScalarGridSpec(
            num_scalar_prefetch=2, grid=(nb, n_kv + 1),
            in_specs=[pl.BlockSpec((1, hq, KV_W), lambda b, j, *_: (b, 0, 0)),
                      pl.BlockSpec((1, _QPAD, kv_w), lambda b, j, *_: (b, 0, jnp.minimum(j, new_blk))),
                      pl.BlockSpec((1, KV_W, PAGE_SIZE), lambda b, j, *_: (b, 0, 0)),
                      pl.BlockSpec((1, KV_W, PAGE_SIZE), lambda b, j, *_: (b, 0, 0)),
                      _lspec(ATTN_W, D_MODEL)]
                     + [page_spec(KV_W, _PAGES_KV, r) for r in range(_PAGES_KV)] * 2,
            out_specs=pl.BlockSpec((1, _QPAD, D_MODEL), lambda b, j, *_: (b, 0, 0)),
            scratch_shapes=[pltpu.VMEM((hq, 1), F32), pltpu.VMEM((hq, 1), F32), pltpu.VMEM((hq, KV_W), F32)]),
        compiler_params=_params("parallel", "arbitrary"),
        name="sample_attend",
    )(page_table, lidx, qx, bias, knt, vnt, w_attn_out, *([k_pool] * _PAGES_KV), *([v_pool] * _PAGES_KV))
    return y[:, :s].reshape(nb * s, D_MODEL)


_ROUTE_W = 8


def _merge_kernel(l_ref, x_ref, yc_ref, ya_ref, g_ref, wout_ref, lg_ref, lb_ref, rwh_ref, rwl_ref,
                  rb_ref, x1_ref, x1b_ref, route_ref, *, alpha):
    mix_in = g_ref[:, :D_MODEL] * yc_ref[...] + g_ref[:, D_MODEL:] * ya_ref[...]
    mix = jnp.dot(mix_in, wout_ref[...], preferred_element_type=F32)
    x1 = _layer_norm(alpha * x_ref[...] + mix, lg_ref[...], lb_ref[...])
    x1_ref[...] = x1
    hi = x1.astype(BF16)
    x1b_ref[...] = hi

    lo = (x1 - hi.astype(F32)).astype(BF16)
    logits = (jnp.dot(hi, rwh_ref[...], preferred_element_type=F32)
              + jnp.dot(hi, rwl_ref[...], preferred_element_type=F32)
              + jnp.dot(lo, rwh_ref[...], preferred_element_type=F32)) + rb_ref[...]
    lane = lax.broadcasted_iota(I32, (1, LANES), 1)

    def first_max(v):
        m = jnp.max(v, axis=1, keepdims=True)
        return m, jnp.min(jnp.where(v == m, lane, LANES), axis=1, keepdims=True)

    is_group = (lane >= N_EXPERTS) & (lane < N_EXPERTS + N_GROUPS)
    gl = jnp.where(is_group, logits, -jnp.inf)
    gm, gidx = first_max(gl)
    p_grp = 1.0 / jnp.sum(jnp.exp(gl - gm), axis=1, keepdims=True)
    grp = gidx - N_EXPERTS
    group_of_lane = lane >> (EXP_PER_GROUP.bit_length() - 1)
    el = jnp.where(group_of_lane == grp, logits, -jnp.inf)
    e1, i1 = first_max(el)
    e2, i2 = first_max(jnp.where(lane == i1, -jnp.inf, el))
    t = jnp.exp(e2 - e1)
    g1 = p_grp / (1.0 + t)
    g2 = g1 * t
    rl = lax.broadcasted_iota(I32, (1, _ROUTE_W), 1)
    route_ref[...] = jnp.where(rl == 0, i1.astype(F32), jnp.where(rl == 1, i2.astype(F32),
                               jnp.where(rl == 2, g1, jnp.where(rl == 3, g2, 0.0))))


def _merge(lidx, x2d, yc, ya, g, p, tm, alpha):
    t = x2d.shape[0]
    row = lambda n: pl.BlockSpec((tm, n), lambda i, l: (i, 0))
    return pl.pallas_call(
        functools.partial(_merge_kernel, alpha=alpha),
        out_shape=[jax.ShapeDtypeStruct((t, D_MODEL), F32), jax.ShapeDtypeStruct((t, D_MODEL), BF16),
                   jax.ShapeDtypeStruct((t, _ROUTE_W), F32)],
        grid_spec=pltpu.PrefetchScalarGridSpec(
            num_scalar_prefetch=1, grid=(t // tm,),
            in_specs=[row(D_MODEL), row(D_MODEL), row(D_MODEL), row(2 * D_MODEL),
                      _lspec(D_MODEL, D_MODEL), _lspec(1, D_MODEL),
                      _lspec(1, D_MODEL), _lspec(D_MODEL, LANES), _lspec(D_MODEL, LANES), _lspec(1, LANES)],
            out_specs=[row(D_MODEL), row(D_MODEL), row(_ROUTE_W)]),
        compiler_params=_params("parallel"),
        name="merge_ln1_router",
    )(lidx, x2d, yc, ya, g, p["w_out"], p["ln1_g"], p["ln1_b"], p["router_hi"],
      p["router_lo"], p["router_b"])


_SLOT_ALIGN = SUBLANES_BF16
_ROW_BLK = 128
_EXPERTS_PER_STEP = 4
_MOE_CHUNK = 256
_META_W = LANES
_MOE_TILE_MAX = 256 * _SLOT_ALIGN


def _moe_plan_kernel(route_ref, dcol_ref, drow_ref, meta_ref):
    tm = route_ref.shape[0]
    blk_rows = min(_MOE_CHUNK, tm)
    lane = lax.broadcasted_iota(I32, (1, LANES), 1).astype(F32)
    hit0 = lane == route_ref[:, 0:1]
    hit1 = lane == route_ref[:, 1:2]
    onehot = jnp.where(hit0, 1.0, jnp.where(hit1, 1.0, 0.0))
    r = lax.broadcasted_iota(I32, (blk_rows, blk_rows), 0)
    c = lax.broadcasted_iota(I32, (blk_rows, blk_rows), 1)
    tri = jnp.where(c < r, 1.0, 0.0).astype(BF16)
    carry = jnp.zeros((1, LANES), F32)
    prefix = []
    for b0 in range(0, tm, blk_rows):
        blk = onehot[b0:b0 + blk_rows]
        prefix.append(jnp.dot(tri, blk.astype(BF16), preferred_element_type=F32) + carry)
        carry = carry + jnp.sum(blk, axis=0, keepdims=True)
    prefix = jnp.concatenate(prefix, axis=0)
    units = jnp.ceil(carry * (1.0 / _SLOT_ALIGN))
    rr = lax.broadcasted_iota(I32, (LANES, LANES), 0)
    cc = lax.broadcasted_iota(I32, (LANES, LANES), 1)
    upper = jnp.where(rr < cc, 1.0, 0.0).astype(BF16)
    units8 = jnp.broadcast_to(units, (8, LANES)).astype(BF16)
    offs = jnp.dot(units8, upper, preferred_element_type=F32)[0:1] * _SLOT_ALIGN
    slot = offs + prefix
    d0 = jnp.sum(jnp.where(hit0, slot, 0.0), axis=1, keepdims=True)
    d1 = jnp.sum(jnp.where(hit1, slot, 0.0), axis=1, keepdims=True)
    rl = lax.broadcasted_iota(I32, (1, LANES), 1)
    rec = jnp.where(rl == 0, d0, jnp.where(rl == 1, d1, jnp.where(rl == 2, route_ref[:, 2:3],
                    jnp.where(rl == 3, route_ref[:, 3:4], 0.0))))
    dcol_ref[...] = rec[:, :_ROUTE_W]
    drow_ref[...] = rec.T[:_ROUTE_W, :]
    nblk = jnp.ceil(units * (_SLOT_ALIGN / _ROW_BLK))
    shift = jnp.where(cc == rr + N_EXPERTS, 1.0, 0.0).astype(BF16)
    nblk_sh = jnp.dot(jnp.broadcast_to(nblk, (8, LANES)).astype(BF16), shift,
                      preferred_element_type=F32)[0:1]
    meta_ref[0] = jnp.where(rl < N_EXPERTS, offs, nblk_sh).astype(I32)


def _moe_plan(route, tm):
    t = route.shape[0]
    nt = t // tm
    assert tm <= _MOE_TILE_MAX
    return pl.pallas_call(
        _moe_plan_kernel,
        out_shape=[jax.ShapeDtypeStruct((t, _ROUTE_W), F32), jax.ShapeDtypeStruct((_ROUTE_W, t), F32),
                   jax.ShapeDtypeStruct((nt, 1, _META_W), I32)],
        grid=(nt,),
        in_specs=[pl.BlockSpec((tm, _ROUTE_W), lambda i: (i, 0))],
        out_specs=[pl.BlockSpec((tm, _ROUTE_W), lambda i: (i, 0)),
                   pl.BlockSpec((_ROUTE_W, tm), lambda i: (0, i)),
                   pl.BlockSpec((1, 1, _META_W), lambda i: (i, 0, 0))],
        compiler_params=_params("parallel"),
        name="moe_plan",
    )(route)


def _n_slots(tm):
    n = 2 * tm + N_EXPERTS * (_SLOT_ALIGN - 1) + _ROW_BLK
    return -(-n // _MOE_CHUNK) * _MOE_CHUNK


def _moe_kernel(meta_ref, l_ref, x1b_ref, x1_ref, drow_ref, dcol_ref, wg_ref, wu_ref, wd_ref, lg_ref, lb_ref,
                x2_ref, xb_ref, yb_ref, gs_ref, *, alpha):
    i, e = pl.program_id(0), pl.program_id(1)
    tm = x1_ref.shape[0]
    n_slots = xb_ref.shape[0]

    @pl.when(e == 0)
    def _dispatch():
        d0, d1 = drow_ref[0:1, :], drow_ref[1:2, :]
        g0, g1 = drow_ref[2:3, :], drow_ref[3:4, :]
        for s0 in range(0, n_slots, _MOE_CHUNK):
            sl = (s0 + lax.broadcasted_iota(I32, (_MOE_CHUNK, 1), 0)).astype(F32)
            a, b = sl == d0, sl == d1
            p = jnp.where(a, 1.0, jnp.where(b, 1.0, 0.0)).astype(BF16)
            xb_ref[s0:s0 + _MOE_CHUNK, :] = jnp.dot(p, x1b_ref[...], preferred_element_type=F32).astype(BF16)
            gs_ref[s0:s0 + _MOE_CHUNK, :] = jnp.sum(jnp.where(a, g0, jnp.where(b, g1, 0.0)), axis=1,
                                                    keepdims=True)
        yb_ref[...] = jnp.zeros(yb_ref.shape, BF16)

    def block(k, start):
        start = pl.multiple_of(start, _SLOT_ALIGN)
        xs = xb_ref[pl.ds(start, _ROW_BLK), :]
        hg = jnp.dot(xs, wg_ref[k], preferred_element_type=F32)
        hu = jnp.dot(xs, wu_ref[k], preferred_element_type=F32)
        h = (hg * jax.nn.sigmoid(hg)) * hu
        y = jnp.dot(h.astype(BF16), wd_ref[k], preferred_element_type=F32)
        yb_ref[pl.ds(start, _ROW_BLK), :] = (y * gs_ref[pl.ds(start, _ROW_BLK), :]).astype(BF16)

    offs = [meta_ref[i * _META_W + e * _EXPERTS_PER_STEP + k] for k in range(_EXPERTS_PER_STEP)]
    nblks = [meta_ref[i * _META_W + N_EXPERTS + e * _EXPERTS_PER_STEP + k] for k in range(_EXPERTS_PER_STEP)]
    one_each = functools.reduce(jnp.logical_and, [n <= 1 for n in nblks])

    @pl.when(one_each)
    def _single_blocks():
        for k in range(_EXPERTS_PER_STEP):
            block(k, offs[k])

    @pl.when(jnp.logical_not(one_each))
    def _ordered_blocks():
        for k in range(_EXPERTS_PER_STEP):
            lax.fori_loop(0, nblks[k], lambda r, carry, k=k: (block(k, offs[k] + r * _ROW_BLK), carry)[1], 0)

    @pl.when(e == pl.num_programs(1) - 1)
    def _combine():
        lane = lax.broadcasted_iota(I32, (1, n_slots), 1).astype(F32)
        rows = min(_MOE_CHUNK, tm)
        for t0 in range(0, tm, rows):
            d0, d1 = dcol_ref[t0:t0 + rows, 0:1], dcol_ref[t0:t0 + rows, 1:2]
            qm = jnp.where(lane == d0, 1.0, jnp.where(lane == d1, 1.0, 0.0)).astype(BF16)
            y = jnp.dot(qm, yb_ref[...], preferred_element_type=F32)
            z = alpha * x1_ref[t0:t0 + rows, :] + y
            x2_ref[t0:t0 + rows, :] = _layer_norm(z, lg_ref[...], lb_ref[...])


def _moe(lidx, x1, x1b, drow, dcol, meta, p, tm, alpha):
    t = x1.shape[0]
    n_slots = _n_slots(tm)
    tile = lambda shape, imap: pl.BlockSpec(shape, imap)
    return pl.pallas_call(
        functools.partial(_moe_kernel, alpha=alpha),
        out_shape=jax.ShapeDtypeStruct((t, D_MODEL), F32),
        grid_spec=pltpu.PrefetchScalarGridSpec(
            num_scalar_prefetch=2, grid=(t // tm, N_EXPERTS // _EXPERTS_PER_STEP),
            in_specs=[tile((tm, D_MODEL), lambda i, e, m, l: (i, 0)),
                      tile((tm, D_MODEL), lambda i, e, m, l: (i, 0)),
                      tile((_ROUTE_W, tm), lambda i, e, m, l: (0, i)),
                      tile((tm, _ROUTE_W), lambda i, e, m, l: (i, 0)),
                      tile((None, _EXPERTS_PER_STEP, D_MODEL, D_EXPERT), lambda i, e, m, l: (l[0], e, 0, 0)),
                      tile((None, _EXPERTS_PER_STEP, D_MODEL, D_EXPERT), lambda i, e, m, l: (l[0], e, 0, 0)),
                      tile((None, _EXPERTS_PER_STEP, D_EXPERT, D_MODEL), lambda i, e, m, l: (l[0], e, 0, 0)),
                      _lspec(1, D_MODEL), _lspec(1, D_MODEL)],
            out_specs=tile((tm, D_MODEL), lambda i, e, m, l: (i, 0)),
            scratch_shapes=[pltpu.VMEM((n_slots, D_MODEL), BF16), pltpu.VMEM((n_slots, D_MODEL), BF16),
                            pltpu.VMEM((n_slots, 1), F32)]),
        compiler_params=_params("parallel", "arbitrary"),
        name="moe_experts_ln2",
    )(meta.reshape(-1), lidx, x1b, x1, drow, dcol, p["w_gate"], p["w_up"], p["w_down"], p["ln2_g"], p["ln2_b"])


def _token_tile(t, want):
    return want if t % want == 0 else t


def _decoder_layer(lidx, x, prefix, attend, p, alpha, key_transposed):
    nb, s, _ = x.shape
    t = nb * s
    x2d = x.reshape(t, D_MODEL)
    u, q_b, k, v, qi_b, ki, wi, g, k_b, v_b, ki_b = _inproj(lidx, x2d, p["w_in"], p["b_in"], _token_tile(t, 512),
                                                            (nb, s) if key_transposed else None)

    ctm = 512 if s % 512 == 0 else _HALO
    u3 = u.reshape(nb, s, C_CONV)
    history = jnp.pad(prefix, ((0, 0), (_HALO - (CONV_W - 1), 0), (0, 0)))
    u_rows = jnp.pad(u3, ((0, 0), (0, (-s) % ctm), (0, 0)))
    yc = _conv_branch(lidx, history, u_rows, p["conv_w"], p["conv_b"], p["conv_ln_g"], p["conv_ln_b"],
                      p["w_conv_out"], ctm)
    yc = yc[:, :s].reshape(t, D_MODEL)

    ya = attend(q_b, k_b, v_b, qi_b, ki_b, wi, lidx=lidx, w_attn_out=p["w_attn_out"])
    x1, x1b, route = _merge(lidx, x2d, yc, ya, g, p, _token_tile(t, 256), alpha)
    mtm = _token_tile(t, 1024)
    dcol, drow, meta = _moe_plan(route, mtm)
    x2 = _moe(lidx, x1, x1b, drow, dcol, meta, p, mtm, alpha)

    new_conv = jnp.concatenate([prefix, u3], axis=1)[:, -(CONV_W - 1):]
    if key_transposed:
        per_head = lambda a: a.reshape(nb, N_KV_HEADS, HEAD_DIM, s).transpose(0, 3, 1, 2)
    else:
        per_head = lambda a: a.reshape(nb, s, N_KV_HEADS, HEAD_DIM)
    return x2.reshape(nb, s, D_MODEL), per_head(k), per_head(v), ki.reshape(nb, s, IDX_DIM), new_conv


def _prepare_params(w_in, b_in, conv_w, conv_b, conv_ln_g, conv_ln_b, w_conv_out, w_attn_out, w_out,
                    ln1_g, ln1_b, router_group_w, router_group_b, router_expert_w, router_expert_b,
                    w_gate, w_up, w_down, ln2_g, ln2_b):
    depth = w_in.shape[0]
    pad = _COL_G - _N_IN_HEAD
    w_pad = jnp.concatenate([w_in[..., :_N_IN_HEAD], jnp.zeros((depth, D_MODEL, pad), F32),
                             w_in[..., _N_IN_HEAD:]], axis=-1).astype(BF16)
    b_pad = jnp.concatenate([b_in[..., :_N_IN_HEAD], jnp.zeros((depth, pad), F32),
                             b_in[..., _N_IN_HEAD:]], axis=-1)[:, None, :]
    rpad = LANES - N_EXPERTS - N_GROUPS
    rw = jnp.concatenate([router_expert_w, router_group_w, jnp.zeros((depth, D_MODEL, rpad), F32)], axis=-1)
    rb = jnp.concatenate([router_expert_b, router_group_b, jnp.zeros((depth, rpad), F32)], axis=-1)
    rw_hi = rw.astype(BF16)
    rw_lo = (rw - rw_hi.astype(F32)).astype(BF16)
    vec = lambda a: a[:, None, :]
    return dict(w_in=w_pad, b_in=b_pad, conv_w=conv_w, conv_b=vec(conv_b), conv_ln_g=vec(conv_ln_g),
                conv_ln_b=vec(conv_ln_b), w_conv_out=w_conv_out.astype(BF16),
                w_attn_out=w_attn_out.astype(BF16), w_out=w_out.astype(BF16), ln1_g=vec(ln1_g),
                ln1_b=vec(ln1_b), router_hi=rw_hi, router_lo=rw_lo, router_b=vec(rb),
                w_gate=w_gate.astype(BF16), w_up=w_up.astype(BF16), w_down=w_down.astype(BF16),
                ln2_g=vec(ln2_g), ln2_b=vec(ln2_b))


def kernel(x_prompt, x_sample, cache_k, cache_v, cache_kidx, state_conv, page_table, w_in, b_in, conv_w,
           conv_b, conv_ln_g, conv_ln_b, w_conv_out, w_attn_out, w_out, ln1_g, ln1_b, router_group_w,
           router_group_b, router_expert_w, router_expert_b, w_gate, w_up, w_down, ln2_g, ln2_b):
    params = _prepare_params(w_in, b_in, conv_w, conv_b, conv_ln_g, conv_ln_b, w_conv_out, w_attn_out,
                             w_out, ln1_g, ln1_b, router_group_w, router_group_b, router_expert_w,
                             router_expert_b, w_gate, w_up, w_down, ln2_g, ln2_b)
    nb, s, _ = x_prompt.shape
    db, ds, _ = x_sample.shape
    depth, n_phys = cache_k.shape[:2]
    alpha = (2 * depth) ** 0.25
    k_pool = cache_k.transpose(0, 1, 3, 4, 2).reshape(depth * n_phys, KV_W, PAGE_SIZE)
    v_pool = cache_v.transpose(0, 1, 3, 4, 2).reshape(depth * n_phys, KV_W, PAGE_SIZE)
    ki_pool = cache_kidx.transpose(0, 1, 3, 2).reshape(depth * n_phys, IDX_DIM, PAGE_SIZE)
    conv_zero = jnp.zeros((nb, CONV_W - 1, C_CONV), F32)

    def layer(carry, xs):
        xp, xsm = carry
        l, st = xs
        lidx = l.reshape(1)
        prompt_attend = functools.partial(_dsa_prompt, nb=nb, s=s)
        xp, *new_p = _decoder_layer(lidx, xp, conv_zero, prompt_attend, params, alpha, key_transposed=True)
        sample_attend = functools.partial(_dsa_sample, k_pool=k_pool, v_pool=v_pool, ki_pool=ki_pool,
                                          page_table=page_table + l * n_phys, nb=db, s=ds)
        xsm, *new_s = _decoder_layer(lidx, xsm, st, sample_attend, params, alpha, key_transposed=False)
        return (xp, xsm), (tuple(new_p), tuple(new_s))

    (xp, xsm), (new_p, new_s) = lax.scan(layer, (x_prompt, x_sample),
                                         (jnp.arange(depth, dtype=I32), state_conv))
    return (xp, xsm, *new_p, *new_s)
```

```python
import functools

import jax
import jax.numpy as jnp
from jax import lax
from jax.experimental import pallas as pl
from jax.experimental.pallas import tpu as pltpu

F32 = jnp.float32
BF16 = jnp.bfloat16
I32 = jnp.int32

D_MODEL = 1024
PAGE_SIZE = 128
C_CONV = 512
CONV_W = 31
N_HEADS = 8
N_KV_HEADS = 4
HEAD_DIM = 64
Q_PER_KV = N_HEADS // N_KV_HEADS
ATTN_W = N_HEADS * HEAD_DIM
KV_W = N_KV_HEADS * HEAD_DIM
N_IDX_HEADS = 8
IDX_DIM = 64
TOPK_MAX = 256
N_GROUPS = 4
EXP_PER_GROUP = 8
N_EXPERTS = N_GROUPS * EXP_PER_GROUP
D_EXPERT = 256
LN_EPS = 1e-5

LANES = 128
_SUBLANES = 8
SUBLANES_BF16 = 16
VMEM_LIMIT = 56 * 1024 * 1024

_N_SMALL = IDX_DIM + N_IDX_HEADS
_COL_CA = 0
_COL_CB = _COL_CA + C_CONV
_COL_Q = _COL_CB + C_CONV
_COL_K = _COL_Q + ATTN_W
_COL_V = _COL_K + KV_W
_COL_QI = _COL_V + KV_W
_COL_KI = _COL_QI + N_IDX_HEADS * IDX_DIM
_COL_G = _COL_KI + LANES
_N_IN_PAD = _COL_G + 2 * D_MODEL
_N_IN_HEAD = _COL_KI + _N_SMALL

_NT = (((1,), (1,)), ((), ()))
_NEG = -1e30
_LOG2E = 1.4426950408889634


def _params(*sem):
    return pltpu.CompilerParams(dimension_semantics=sem, vmem_limit_bytes=VMEM_LIMIT)


def _lspec(*shape):
    nd = len(shape)
    return pl.BlockSpec((None,) + shape, lambda *a: (a[-1][0],) + (0,) * nd)


def _layer_norm(x, g, b):
    mu = jnp.mean(x, axis=-1, keepdims=True)
    xc = x - mu
    var = jnp.mean(xc * xc, axis=-1, keepdims=True)
    return xc * lax.rsqrt(var + LN_EPS) * g + b


def _inproj_kernel(l_ref, x_ref, w_ref, b_ref, u_ref, q_ref, k_ref, v_ref, qi_ref, ki_ref, wi_ref, g_ref,
                   kb_ref, vb_ref, kib_ref, *, key_transposed):
    xb = x_ref[...].astype(BF16)

    def proj(c0, n):
        return jnp.dot(xb, w_ref[:, c0:c0 + n], preferred_element_type=F32) + b_ref[:, c0:c0 + n]

    u_ref[...] = proj(_COL_CA, C_CONV) * jax.nn.sigmoid(proj(_COL_CB, C_CONV))
    q_ref[...] = (proj(_COL_Q, ATTN_W) * (HEAD_DIM ** -0.5 * _LOG2E)).astype(BF16)
    k = proj(_COL_K, KV_W)
    v = proj(_COL_V, KV_W)
    if key_transposed:
        k_ref[0] = k.T
        v_t = v.T
        v_ref[0] = v_t
        vb_ref[0] = v_t.astype(BF16)
        for g in range(N_KV_HEADS):
            kb_ref[0, g] = k[:, g * HEAD_DIM:(g + 1) * HEAD_DIM].astype(BF16)
    else:
        k_ref[...] = k
        kb_ref[...] = k.astype(BF16)
        v_ref[...] = v
        vb_ref[...] = v.astype(BF16)
    qi_ref[...] = (proj(_COL_QI, N_IDX_HEADS * IDX_DIM) * (IDX_DIM ** -0.5)).astype(BF16)
    small = proj(_COL_KI, LANES)
    ki = small[:, :IDX_DIM]
    ki_ref[...] = ki
    kib_ref[...] = ki.astype(BF16)
    wi_ref[...] = small[:, IDX_DIM:_N_SMALL] * (N_IDX_HEADS ** -0.5)
    g_ref[...] = jax.nn.sigmoid(proj(_COL_G, 2 * D_MODEL)).astype(BF16)


def _inproj(lidx, x2d, w_pad, b_pad, tm, seq=None):
    t = x2d.shape[0]
    rows = lambda n, dt: (jax.ShapeDtypeStruct((t, n), dt), pl.BlockSpec((tm, n), lambda i, l: (i, 0)))
    if seq is None:
        k, v, kb, vb = rows(KV_W, F32), rows(KV_W, F32), rows(KV_W, BF16), rows(KV_W, BF16)
    else:
        nb, s = seq
        tps = s // tm
        kt = lambda dt: (jax.ShapeDtypeStruct((nb, KV_W, s), dt),
                         pl.BlockSpec((1, KV_W, tm), lambda i, l: (i // tps, 0, i % tps)))
        k, v, vb = kt(F32), kt(F32), kt(BF16)
        kb = (jax.ShapeDtypeStruct((nb, N_KV_HEADS, s, HEAD_DIM), BF16),
              pl.BlockSpec((1, N_KV_HEADS, tm, HEAD_DIM), lambda i, l: (i // tps, 0, i % tps, 0)))
    outs = [rows(C_CONV, F32), rows(ATTN_W, BF16), k, v, rows(N_IDX_HEADS * IDX_DIM, BF16), rows(IDX_DIM, F32),
            rows(N_IDX_HEADS, F32), rows(2 * D_MODEL, BF16), kb, vb, rows(IDX_DIM, BF16)]
    return pl.pallas_call(
        functools.partial(_inproj_kernel, key_transposed=seq is not None),
        out_shape=[o[0] for o in outs],
        grid_spec=pltpu.PrefetchScalarGridSpec(
            num_scalar_prefetch=1, grid=(t // tm,),
            in_specs=[pl.BlockSpec((tm, D_MODEL), lambda i, l: (i, 0)),
                      _lspec(D_MODEL, _N_IN_PAD), _lspec(1, _N_IN_PAD)],
            out_specs=[o[1] for o in outs]),
        compiler_params=_params("parallel"),
        name="inproj",
    )(lidx, x2d, w_pad, b_pad)


_HALO = 32
_CONV_ROWS = 32


def _conv_kernel(l_ref, first_ref, prev_ref, cur_ref, cw_ref, cb_ref, lg_ref, lb_ref, wo_ref, y_ref,
                 hist_ref, acc_ref, shift_ref):
    tm = cur_ref.shape[1]
    at_start = pl.program_id(1) == 0
    hist_ref[0:_HALO, :] = jnp.where(at_start, first_ref[0], prev_ref[0])
    hist_ref[_HALO:_HALO + tm, :] = cur_ref[0]
    first = _HALO - (CONV_W - 1)
    for b in range(_SUBLANES):
        rows = tm + _SUBLANES * ((CONV_W - 1 - b) // _SUBLANES)
        shift_ref[b, 0:rows, :] = hist_ref[first + b:first + b + rows, :]
    for r0 in range(0, tm, _CONV_ROWS):
        acc = jnp.zeros((_CONV_ROWS, C_CONV), F32)
        for j in range(CONV_W):
            a, b = divmod(j, _SUBLANES)
            acc = acc + cw_ref[j:j + 1, :] * shift_ref[b, r0 + a * _SUBLANES:r0 + a * _SUBLANES + _CONV_ROWS, :]
        acc_ref[r0:r0 + _CONV_ROWS, :] = acc + cb_ref[...]
    y = _layer_norm(acc_ref[...], lg_ref[...], lb_ref[...])
    y = y * jax.nn.sigmoid(y)
    y_ref[0] = jnp.dot(y.astype(BF16), wo_ref[...], preferred_element_type=F32).astype(BF16)


def _conv_branch(lidx, history, u, conv_w, conv_b, ln_g, ln_b, w_out_bf, tm):
    nb, length, _ = u.shape
    halo_blocks = tm // _HALO
    return pl.pallas_call(
        _conv_kernel,
        out_shape=jax.ShapeDtypeStruct((nb, length, D_MODEL), BF16),
        grid_spec=pltpu.PrefetchScalarGridSpec(
            num_scalar_prefetch=1, grid=(nb, length // tm),
            in_specs=[pl.BlockSpec((1, _HALO, C_CONV), lambda b, i, l: (b, 0, 0)),
                      pl.BlockSpec((1, _HALO, C_CONV), lambda b, i, l: (b, jnp.maximum(i * halo_blocks - 1, 0), 0)),
                      pl.BlockSpec((1, tm, C_CONV), lambda b, i, l: (b, i, 0)),
                      _lspec(CONV_W, C_CONV), _lspec(1, C_CONV), _lspec(1, C_CONV), _lspec(1, C_CONV),
                      _lspec(C_CONV, D_MODEL)],
            out_specs=pl.BlockSpec((1, tm, D_MODEL), lambda b, i, l: (b, i, 0)),
            scratch_shapes=[pltpu.VMEM((_HALO + tm, C_CONV), F32), pltpu.VMEM((tm, C_CONV), F32),
                            pltpu.VMEM((_SUBLANES, tm + _HALO - _SUBLANES, C_CONV), F32)]),
        compiler_params=_params("parallel", "parallel"),
        name="conv_branch",
    )(lidx, history, u, u, conv_w, conv_b, ln_g, ln_b, w_out_bf)


_BISECT_STEPS = 14


_FOLD_ROWS = 64


def _reduce_rows(x, op):
    rows, cols = x.shape
    if rows > _FOLD_ROWS:
        x = op(x.reshape(rows // _FOLD_ROWS, _FOLD_ROWS, cols), axis=0)
    x = op(x.reshape(x.shape[0] // _SUBLANES, _SUBLANES, cols), axis=0)
    return op(x, axis=0, keepdims=True)


def _fold_keys(sc_ref, n_chunks, tk, init, fn, op, merge, first=0):
    def body(c, acc):
        off = pl.multiple_of(c * tk, tk)
        v = fn(sc_ref[pl.ds(off, tk), :], off).reshape(tk // _FOLD_ROWS, _FOLD_ROWS, cols)
        return merge(acc, op(v, axis=0))

    cols = sc_ref.shape[1]
    acc = lax.fori_loop(first, n_chunks, body, jnp.full((_FOLD_ROWS, cols), init, F32))
    return _reduce_rows(acc, op)


def _count_keys(sc_ref, n_chunks, tk, fn, first=0):
    return _fold_keys(sc_ref, n_chunks, tk, 0.0, fn, jnp.sum, jnp.add, first)


def _count_ge(sc_ref, n_chunks, tk, thr):
    return _count_keys(sc_ref, n_chunks, tk, lambda blk, off: jnp.where(blk >= thr, 1.0, 0.0))


def _any(flag):
    return jnp.max(jnp.where(flag, 1, 0))


def _select_threshold(sc_ref, st_ref, n_chunks, tk, kt, s_min, s_max, n_valid):
    lo_ref, hi_ref, cl_ref, done_ref, jlo_ref, jhi_ref = (st_ref.at[n:n + 1] for n in range(6))
    c_max = _count_ge(sc_ref, n_chunks, tk, s_max)
    top = c_max >= kt
    lo_ref[...] = jnp.where(top, s_max, s_min)
    hi_ref[...] = s_max
    cl0 = jnp.where(top, c_max, n_valid)
    cl_ref[...] = cl0
    done0 = jnp.where(top, 1.0, jnp.where(cl0 == kt, 1.0, 0.0))
    done_ref[...] = done0

    def bisect_step():
        lo, hi, cl, done = lo_ref[...], hi_ref[...], cl_ref[...], done_ref[...]
        mid = 0.5 * lo + 0.5 * hi
        c = _count_ge(sc_ref, n_chunks, tk, mid)
        act = done == 0.0
        up = c >= kt
        lo_ref[...] = jnp.where(act, jnp.where(up, mid, lo), lo)
        cl_ref[...] = jnp.where(act, jnp.where(up, c, cl), cl)
        hi_ref[...] = jnp.where(act, jnp.where(up, hi, mid), hi)
        done_ref[...] = jnp.where(act, jnp.where(c == kt, 1.0, 0.0), done)

    @pl.when(_any(done0 == 0.0) > 0)
    def _narrow():
        lax.fori_loop(0, _BISECT_STEPS, lambda _, carry: (bisect_step(), carry)[1], 0)

    def snap(_):
        lo, hi, cl, done = lo_ref[...], hi_ref[...], cl_ref[...], done_ref[...]
        t1 = _fold_keys(sc_ref, n_chunks, tk, -jnp.inf, lambda blk, off: jnp.where(blk < hi, blk, -jnp.inf),
                        jnp.max, jnp.maximum)
        c1 = _count_ge(sc_ref, n_chunks, tk, t1)
        act = done == 0.0
        found = c1 >= kt
        lo_ref[...] = jnp.where(act, jnp.where(found, t1, lo), lo)
        cl_ref[...] = jnp.where(act, jnp.where(found, c1, cl), cl)
        hi_ref[...] = jnp.where(act, jnp.where(found, hi, t1), hi)
        done_new = jnp.where(act, jnp.where(found, 1.0, 0.0), done)
        done_ref[...] = done_new
        return _any(done_new == 0.0)

    lax.while_loop(lambda f: f > 0, snap, _any(done_ref[...] == 0.0))

    thr = lo_ref[...]

    @pl.when(_any(cl_ref[...] != kt) > 0)
    def _cut_ties():
        tied = cl_ref[...] != kt
        need = kt - _count_keys(sc_ref, n_chunks, tk, lambda blk, off: jnp.where(blk > thr, 1.0, 0.0))
        row = lax.broadcasted_iota(I32, (tk, 1), 0)

        def locate(c, carry):
            seen, chunk, seen_before = carry
            off = pl.multiple_of(c * tk, tk)
            cnt = _reduce_rows(jnp.where(sc_ref[pl.ds(off, tk), :] == thr, 1.0, 0.0), jnp.sum)
            here = jnp.where(seen < need, jnp.where(seen + cnt >= need, 1.0, 0.0), 0.0)
            chunk = jnp.where(here > 0.0, jnp.asarray(c, F32), chunk)
            seen_before = jnp.where(here > 0.0, seen, seen_before)
            return seen + cnt, chunk, seen_before

        zero = jnp.zeros_like(thr)
        _, chunk, seen_before = lax.fori_loop(0, n_chunks, locate, (zero, zero, zero))
        first_key = chunk * float(tk)
        need_here = need - seen_before
        c_first = jnp.min(jnp.where(tied, chunk, float(sc_ref.shape[0]))).astype(I32)
        c_last = jnp.max(jnp.where(tied, chunk, -1.0)).astype(I32)

        jlo_ref[...] = first_key - 1.0
        jhi_ref[...] = first_key + float(tk - 1)

        def step(_, carry):
            jlo, jhi = jlo_ref[...], jhi_ref[...]
            mid = jnp.floor(0.5 * (jlo + jhi))

            def in_range(blk, off):
                idx = (off + row).astype(F32)
                return jnp.where(blk == thr, jnp.where(idx <= mid, jnp.where(idx >= first_key, 1.0, 0.0), 0.0), 0.0)

            ok = _count_keys(sc_ref, c_last + 1, tk, in_range, c_first) >= need_here
            jhi_ref[...] = jnp.where(ok, mid, jhi)
            jlo_ref[...] = jnp.where(ok, jlo, mid)
            return carry

        lax.fori_loop(0, tk.bit_length() - 1, step, 0)
        cut = jnp.where(tied, jhi_ref[...], float(sc_ref.shape[0]))

        def demote(c, carry):
            off = pl.multiple_of(c * tk, tk)
            blk = sc_ref[pl.ds(off, tk), :]
            beyond = (off + row).astype(F32) > cut
            sc_ref[pl.ds(off, tk), :] = jnp.where(blk == thr, jnp.where(beyond, -jnp.inf, blk), blk)
            return carry

        lax.fori_loop(c_first, n_chunks, demote, 0)

    return thr


def _min_max_init(cols):
    return (jnp.full((_FOLD_ROWS, cols), jnp.inf, F32), jnp.full((_FOLD_ROWS, cols), -jnp.inf, F32))


def _min_max_update(mn, mx, score, valid):
    shape = (score.shape[0] // _FOLD_ROWS, _FOLD_ROWS, score.shape[1])
    mn = jnp.minimum(mn, jnp.min(jnp.where(valid, score, jnp.inf).reshape(shape), axis=0))
    mx = jnp.maximum(mx, jnp.max(jnp.where(valid, score, -jnp.inf).reshape(shape), axis=0))
    return mn, mx


_TQ = 256
_TK = 512


_IDX_PAIRS = N_IDX_HEADS // 2
_ONES_ROWS = SUBLANES_BF16


def _dsa_prompt_kernel(l_ref, qi_ref, wi_ref, ki_ref, q_ref, k_ref, vt_ref, wao_ref, y_ref,
                       sc_ref, st_ref, m_ref, acc_ref, s_ref, cm_ref, p_ref, al_ref):
    tq, tk = _TQ, _TK
    i = pl.program_id(1)
    n_chunks = ((i + 1) * tq + tk - 1) // tk
    qpos = i * tq + lax.broadcasted_iota(I32, (1, tq), 1)
    krow = lax.broadcasted_iota(I32, (tk, 1), 0)

    def scores(c, carry):
        off = pl.multiple_of(c * tk, tk)
        kc = ki_ref[0, pl.ds(off, tk), :]
        score = None
        for p in range(_IDX_PAIRS):
            d = lax.dot_general(kc, qi_ref[0, 0, p], _NT, preferred_element_type=F32)
            for j in range(2):
                h = 2 * p + j
                t = wi_ref[0, h:h + 1, :] * jnp.maximum(d[:, j * tq:(j + 1) * tq], 0.0)
                score = t if score is None else score + t
        valid = off + krow <= qpos
        sc_ref[pl.ds(off, tk), :] = jnp.where(valid, score, -jnp.inf)
        return _min_max_update(*carry, score, valid)

    mn, mx = lax.fori_loop(0, n_chunks, scores, _min_max_init(tq))
    n_valid = (qpos + 1).astype(F32)
    kt = jnp.minimum(n_valid, float(TOPK_MAX))
    thr = _select_threshold(sc_ref, st_ref, n_chunks, tk, kt, _reduce_rows(mn, jnp.min),
                            _reduce_rows(mx, jnp.max), n_valid)

    m_ref[...] = jnp.full(m_ref.shape, _NEG, F32)
    acc_ref[...] = jnp.zeros(acc_ref.shape, F32)

    def mask_bias(c):
        off = pl.multiple_of(c * tk, tk)
        bias = jnp.where(sc_ref[pl.ds(off, tk), :] >= thr, 0.0, -jnp.inf)
        return jnp.concatenate([bias] * Q_PER_KV, axis=1)

    def qk_stage(c, slot, groups, bias=None):
        off = pl.multiple_of(c * tk, tk)
        bias = mask_bias(c) if bias is None else bias
        for g in groups:
            s = lax.dot_general(k_ref[0, g, pl.ds(off, tk), :], q_ref[0, 0, g], _NT,
                                preferred_element_type=F32) + bias
            s_ref[slot, g] = s
            cm_ref[slot, g] = _reduce_rows(s, jnp.max)

    def exp_stage(slot, groups):
        for g in groups:
            m_old = m_ref[g]
            m_new = jnp.maximum(m_old, cm_ref[slot, g])
            al_ref[slot, g] = jnp.exp2(m_old - m_new)
            p_ref[slot, g] = jnp.exp2(s_ref[slot, g] - m_new).astype(BF16)
            m_ref[g] = m_new

    def pv_stage(c, slot, groups):
        off = pl.multiple_of(c * tk, tk)
        for g in groups:
            v_ones =jnp.concatenate([vt_ref[0, g, :, pl.ds(off, tk)], jnp.ones((_ONES_ROWS, tk), BF16)], axis=0)
            acc_ref[g] = al_ref[slot, g] * acc_ref[g] + jnp.dot(v_ones, p_ref[slot, g],
                                                                preferred_element_type=F32)

    every = range(N_KV_HEADS)

    def attend(c, carry):
        slot = c % 2
        for g in every:
            pv_stage(c, slot, (g,))
            qk_stage(c + 2, slot, (g,))
            exp_stage(1 - slot, (g,))
        return carry

    qk_stage(0, 0, every)
    exp_stage(0, every)

    @pl.when(n_chunks > 1)
    def _fill():
        qk_stage(1, 1, every)

    lax.fori_loop(0, n_chunks - 2, attend, 0)
    last = n_chunks - 1

    @pl.when(n_chunks > 1)
    def _drain():
        exp_stage(last % 2, every)
        pv_stage(last - 1, (last - 1) % 2, every)

    pv_stage(last, last % 2, every)
    heads = []
    for g in range(N_KV_HEADS):
        o_g = acc_ref[g, 0:HEAD_DIM, :] / acc_ref[g, HEAD_DIM:HEAD_DIM + 1, :]
        heads += [o_g[:, j * tq:(j + 1) * tq] for j in range(Q_PER_KV)]
    o_t = jnp.concatenate(heads, axis=0).astype(BF16)
    y_ref[0] = lax.dot_general(o_t, wao_ref[...], (((0,), (0,)), ((), ())),
                               preferred_element_type=F32).astype(BF16)


def _dsa_prompt(q_b, k_hm, vt_b, qi_b, ki_b, wi, nb, s, lidx, w_attn_out):
    tq = _TQ
    nq = s // tq
    cols = Q_PER_KV * tq

    def pair_major(a, n_pairs):
        a = a.reshape(nb, nq, tq, n_pairs, 2, a.shape[-1] // (2 * n_pairs)).transpose(0, 1, 3, 4, 2, 5)
        return a.reshape(nb, nq, n_pairs, 2 * tq, -1)

    kh = k_hm
    vt = vt_b.reshape(nb, N_KV_HEADS, HEAD_DIM, s)
    wit = wi.reshape(nb, s, N_IDX_HEADS).transpose(0, 2, 1)
    once = dict(pipeline_mode=pl.Buffered(1))
    y = pl.pallas_call(
        _dsa_prompt_kernel,
        out_shape=jax.ShapeDtypeStruct((nb, s, D_MODEL), BF16),
        grid_spec=pltpu.PrefetchScalarGridSpec(
            num_scalar_prefetch=1, grid=(nb, nq),
            in_specs=[pl.BlockSpec((1, 1, _IDX_PAIRS, 2 * tq, IDX_DIM), lambda b, i, l: (b, i, 0, 0, 0)),
                      pl.BlockSpec((1, N_IDX_HEADS, tq), lambda b, i, l: (b, 0, i)),
                      pl.BlockSpec((1, s, IDX_DIM), lambda b, i, l: (b, 0, 0), **once),
                      pl.BlockSpec((1, 1, N_KV_HEADS, cols, HEAD_DIM), lambda b, i, l: (b, i, 0, 0, 0)),
                      pl.BlockSpec((1, N_KV_HEADS, s, HEAD_DIM), lambda b, i, l: (b, 0, 0, 0), **once),
                      pl.BlockSpec((1, N_KV_HEADS, HEAD_DIM, s), lambda b, i, l: (b, 0, 0, 0), **once),
                      _lspec(ATTN_W, D_MODEL)],
            out_specs=pl.BlockSpec((1, tq, D_MODEL), lambda b, i, l: (b, i, 0)),
            scratch_shapes=[pltpu.VMEM((s, tq), F32), pltpu.VMEM((_SUBLANES, tq), F32),
                            pltpu.VMEM((N_KV_HEADS, 1, cols), F32),
                            pltpu.VMEM((N_KV_HEADS, HEAD_DIM + _ONES_ROWS, cols), F32),
                            pltpu.VMEM((2, N_KV_HEADS, _TK, cols), F32), pltpu.VMEM((2, N_KV_HEADS, 1, cols), F32),
                            pltpu.VMEM((2, N_KV_HEADS, _TK, cols), BF16), pltpu.VMEM((2, N_KV_HEADS, 1, cols), F32)]),
        compiler_params=_params("parallel", "arbitrary"),
        name="dsa_prompt",
    )(lidx, pair_major(qi_b, _IDX_PAIRS), wit, ki_b.reshape(nb, s, IDX_DIM), pair_major(q_b, N_KV_HEADS), kh, vt,
      w_attn_out)
    return y.reshape(nb * s, D_MODEL)


_QPAD = 8
_PAGES_SC = 32
_PAGES_KV = 32


def _sample_scores_kernel(pt_ref, qi_ref, wi_ref, knt_ref, *rest):
    pages, sc_ref = rest[:_PAGES_SC], rest[_PAGES_SC]
    j = pl.program_id(1)
    last = pl.num_programs(1) - 1

    def scores(kt):
        d = jnp.dot(qi_ref[0], kt, preferred_element_type=F32)
        score = None
        for h in range(N_IDX_HEADS):
            t = wi_ref[0, :, h:h + 1] * jnp.maximum(d[h * _QPAD:(h + 1) * _QPAD, :], 0.0)
            score = t if score is None else score + t
        return score

    @pl.when(j < last)
    def _past():
        sc_ref[0] = scores(jnp.concatenate([pg[0] for pg in pages], axis=1).astype(BF16))

    @pl.when(j == last)
    def _new():
        sc_ref[0] = jnp.zeros(sc_ref.shape[1:], F32)
        sc_ref[0, :, 0:PAGE_SIZE] = scores(knt_ref[0])


def _sample_select_kernel(sc_in_ref, bias_ref, sc_ref, st_ref, *, past, dec_seq):
    tk = _TK
    n_chunks = sc_ref.shape[0] // tk
    q = lax.broadcasted_iota(I32, (1, LANES), 1) % dec_seq
    krow = lax.broadcasted_iota(I32, (tk, 1), 0)

    def load(c, carry):
        off = pl.multiple_of(c * tk, tk)
        s = sc_in_ref[pl.ds(off, tk), :]
        valid = off + krow <= past + q
        sc_ref[pl.ds(off, tk), :] = jnp.where(valid, s, -jnp.inf)
        return _min_max_update(*carry, s, valid)

    mn, mx = lax.fori_loop(0, n_chunks, load, _min_max_init(LANES))
    n_valid = (past + 1 + q).astype(F32)
    kt = jnp.minimum(n_valid, float(min(TOPK_MAX, (past + dec_seq) // 4)))
    thr = _select_threshold(sc_ref, st_ref, n_chunks, tk, kt, _reduce_rows(mn, jnp.min),
                            _reduce_rows(mx, jnp.max), n_valid)

    def emit(c, carry):
        off = pl.multiple_of(c * tk, tk)
        bias_ref[pl.ds(off, tk), :] = jnp.where(sc_ref[pl.ds(off, tk), :] >= thr, 0.0, -jnp.inf)
        return carry

    lax.fori_loop(0, n_chunks, emit, 0)


def _sample_attend_kernel(pt_ref, l_ref_, q_ref, bias_ref, knt_ref, vnt_ref, wao_ref, *rest):
    kpages, vpages = rest[:_PAGES_KV], rest[_PAGES_KV:2 * _PAGES_KV]
    y_ref, m_ref, l_ref, acc_ref = rest[2 * _PAGES_KV:]
    j = pl.program_id(1)
    last = pl.num_programs(1) - 1

    @pl.when(j == 0)
    def _init():
        m_ref[...] = jnp.full(m_ref.shape, _NEG, F32)
        l_ref[...] = jnp.zeros(l_ref.shape, F32)
        acc_ref[...] = jnp.zeros(acc_ref.shape, F32)

    def attend(kt, vt, bias):
        s = jnp.dot(q_ref[0], kt, preferred_element_type=F32)
        s = jnp.concatenate([s[h * _QPAD:(h + 1) * _QPAD] + bias for h in range(N_HEADS)], axis=0)
        m_old = m_ref[...]
        m_new = jnp.maximum(m_old, jnp.max(s, axis=1, keepdims=True))
        alpha = jnp.exp2(m_old - m_new)
        p = jnp.exp2(s - m_new)
        l_ref[...] = alpha * l_ref[...] + jnp.sum(p, axis=1, keepdims=True)
        acc_ref[...] = alpha * acc_ref[...] + lax.dot_general(p.astype(BF16), vt, _NT,
                                                              preferred_element_type=F32)
        m_ref[...] = m_new

    @pl.when(j < last)
    def _past():
        attend(jnp.concatenate([pg[0] for pg in kpages], axis=1).astype(BF16),
               jnp.concatenate([pg[0] for pg in vpages], axis=1).astype(BF16), bias_ref[0])

    @pl.when(j == last)
    def _new():
        attend(knt_ref[0], vnt_ref[0], bias_ref[0, :, 0:PAGE_SIZE])
        o = acc_ref[...] / l_ref[...]
        y = None
        for h in range(N_HEADS):
            g = h // Q_PER_KV
            o_h = o[h * _QPAD:(h + 1) * _QPAD, g * HEAD_DIM:(g + 1) * HEAD_DIM].astype(BF16)
            t = jnp.dot(o_h, wao_ref[h * HEAD_DIM:(h + 1) * HEAD_DIM, :], preferred_element_type=F32)
            y = t if y is None else y + t
        y_ref[0] = y.astype(BF16)


def _dsa_sample(q_b, k_b, v_b, qi_b, ki_b, wi, k_pool, v_pool, ki_pool, page_table, nb, s, lidx, w_attn_out):
    n_pages = page_table.shape[1]
    past = n_pages * PAGE_SIZE
    assert _QPAD % s == 0 and (nb * _QPAD) % LANES == 0
    assert n_pages % _PAGES_SC == 0 and n_pages % _PAGES_KV == 0
    dup = jnp.arange(_QPAD) % s
    qi = qi_b.reshape(nb, s, N_IDX_HEADS, IDX_DIM)[:, dup].transpose(0, 2, 1, 3)
    qi = qi.reshape(nb, N_IDX_HEADS * _QPAD, IDX_DIM)
    wi8 = wi.reshape(nb, s, N_IDX_HEADS)[:, dup]
    new_page = lambda a: jnp.pad(a.reshape(nb, s, -1).transpose(0, 2, 1), ((0, 0), (0, 0), (0, PAGE_SIZE - s)))
    kint, knt, vnt = new_page(ki_b), new_page(k_b), new_page(v_b)

    n_sc = n_pages // _PAGES_SC
    sc_w = _PAGES_SC * PAGE_SIZE

    def page_spec(width, per_step, r):
        return pl.BlockSpec((1, width, PAGE_SIZE),
                            lambda b, j, pt, *_: (pt[b, jnp.minimum(j * per_step + r, n_pages - 1)], 0, 0))

    scores = pl.pallas_call(
        _sample_scores_kernel,
        out_shape=jax.ShapeDtypeStruct((nb, _QPAD, (n_sc + 1) * sc_w), F32),
        grid_spec=pltpu.PrefetchScalarGridSpec(
            num_scalar_prefetch=1, grid=(nb, n_sc + 1),
            in_specs=[pl.BlockSpec((1, N_IDX_HEADS * _QPAD, IDX_DIM), lambda b, j, pt: (b, 0, 0)),
                      pl.BlockSpec((1, _QPAD, N_IDX_HEADS), lambda b, j, pt: (b, 0, 0)),
                      pl.BlockSpec((1, IDX_DIM, PAGE_SIZE), lambda b, j, pt: (b, 0, 0))]
                     + [page_spec(IDX_DIM, _PAGES_SC, r) for r in range(_PAGES_SC)],
            out_specs=pl.BlockSpec((1, _QPAD, sc_w), lambda b, j, pt: (b, 0, j))),
        compiler_params=_params("parallel", "arbitrary"),
        name="sample_scores",
    )(page_table, qi, wi8, kint, *([ki_pool] * _PAGES_SC))

    width = scores.shape[2]
    cols = nb * _QPAD
    bias = pl.pallas_call(
        functools.partial(_sample_select_kernel, past=past, dec_seq=s),
        out_shape=jax.ShapeDtypeStruct((width, cols), F32),
        grid=(cols // LANES,),
        in_specs=[pl.BlockSpec((width, LANES), lambda i: (0, i))],
        out_specs=pl.BlockSpec((width, LANES), lambda i: (0, i)),
        scratch_shapes=[pltpu.VMEM((width, LANES), F32), pltpu.VMEM((_SUBLANES, LANES), F32)],
        compiler_params=_params("parallel"),
        name="sample_select",
    )(scores.reshape(cols, width).T).T.reshape(nb, _QPAD, width)

    q4 = q_b.reshape(nb, s, N_KV_HEADS, Q_PER_KV, HEAD_DIM)[:, dup]
    eye = jnp.eye(N_KV_HEADS, dtype=q_b.dtype)
    qx = jnp.einsum("bqgjd,gk->bgjqkd", q4, eye).reshape(nb, N_HEADS * _QPAD, KV_W)
    n_kv = n_pages // _PAGES_KV
    kv_w = _PAGES_KV * PAGE_SIZE
    new_blk = past // kv_w
    hq = N_HEADS * _QPAD
    y = pl.pallas_call(
        _sample_attend_kernel,
        out_shape=jax.ShapeDtypeStruct((nb, _QPAD, D_MODEL), BF16),
        grid_spec=pltpu.PrefetchScalarGridSpec(
            num_scalar_prefetch=2, grid=(nb, n_kv + 1),
            in_specs=[pl.BlockSpec((1, hq, KV_W), lambda b, j, *_: (b, 0, 0)),
                      pl.BlockSpec((1, _QPAD, kv_w), lambda b, j, *_: (b, 0, jnp.minimum(j, new_blk))),
                      pl.BlockSpec((1, KV_W, PAGE_SIZE), lambda b, j, *_: (b, 0, 0)),
                      pl.BlockSpec((1, KV_W, PAGE_SIZE), lambda b, j, *_: (b, 0, 0)),
                      _lspec(ATTN_W, D_MODEL)]
                     + [page_spec(KV_W, _PAGES_KV, r) for r in range(_PAGES_KV)] * 2,
            out_specs=pl.BlockSpec((1, _QPAD, D_MODEL), lambda b, j, *_: (b, 0, 0)),
            scratch_shapes=[pltpu.VMEM((hq, 1), F32), pltpu.VMEM((hq, 1), F32), pltpu.VMEM((hq, KV_W), F32)]),
        compiler_params=_params("parallel", "arbitrary"),
        name="sample_attend",
    )(page_table, lidx, qx, bias, knt, vnt, w_attn_out, *([k_pool] * _PAGES_KV), *([v_pool] * _PAGES_KV))
    return y[:, :s].reshape(nb * s, D_MODEL)


_ROUTE_W = 8


def _merge_kernel(l_ref, x_ref, yc_ref, ya_ref, g_ref, wout_ref, lg_ref, lb_ref, rwh_ref, rwl_ref,
                  rb_ref, x1_ref, x1b_ref, route_ref, *, alpha):
    mix_in = g_ref[:, :D_MODEL] * yc_ref[...] + g_ref[:, D_MODEL:] * ya_ref[...]
    mix = jnp.dot(mix_in, wout_ref[...], preferred_element_type=F32)
    x1 = _layer_norm(alpha * x_ref[...] + mix, lg_ref[...], lb_ref[...])
    x1_ref[...] = x1
    hi = x1.astype(BF16)
    x1b_ref[...] = hi

    lo = (x1 - hi.astype(F32)).astype(BF16)
    logits = (jnp.dot(hi, rwh_ref[...], preferred_element_type=F32)
              + jnp.dot(hi, rwl_ref[...], preferred_element_type=F32)
              + jnp.dot(lo, rwh_ref[...], preferred_element_type=F32)) + rb_ref[...]
    lane = lax.broadcasted_iota(I32, (1, LANES), 1)

    def first_max(v):
        m = jnp.max(v, axis=1, keepdims=True)
        return m, jnp.min(jnp.where(v == m, lane, LANES), axis=1, keepdims=True)

    is_group = (lane >= N_EXPERTS) & (lane < N_EXPERTS + N_GROUPS)
    gl = jnp.where(is_group, logits, -jnp.inf)
    gm, gidx = first_max(gl)
    p_grp = 1.0 / jnp.sum(jnp.exp(gl - gm), axis=1, keepdims=True)
    grp = gidx - N_EXPERTS
    group_of_lane = lane >> (EXP_PER_GROUP.bit_length() - 1)
    el = jnp.where(group_of_lane == grp, logits, -jnp.inf)
    e1, i1 = first_max(el)
    e2, i2 = first_max(jnp.where(lane == i1, -jnp.inf, el))
    t = jnp.exp(e2 - e1)
    g1 = p_grp / (1.0 + t)
    g2 = g1 * t
    rl = lax.broadcasted_iota(I32, (1, _ROUTE_W), 1)
    route_ref[...] = jnp.where(rl == 0, i1.astype(F32), jnp.where(rl == 1, i2.astype(F32),
                               jnp.where(rl == 2, g1, jnp.where(rl == 3, g2, 0.0))))


def _merge(lidx, x2d, yc, ya, g, p, tm, alpha):
    t = x2d.shape[0]
    row = lambda n: pl.BlockSpec((tm, n), lambda i, l: (i, 0))
    return pl.pallas_call(
        functools.partial(_merge_kernel, alpha=alpha),
        out_shape=[jax.ShapeDtypeStruct((t, D_MODEL), F32), jax.ShapeDtypeStruct((t, D_MODEL), BF16),
                   jax.ShapeDtypeStruct((t, _ROUTE_W), F32)],
        grid_spec=pltpu.PrefetchScalarGridSpec(
            num_scalar_prefetch=1, grid=(t // tm,),
            in_specs=[row(D_MODEL), row(D_MODEL), row(D_MODEL), row(2 * D_MODEL),
                      _lspec(D_MODEL, D_MODEL), _lspec(1, D_MODEL),
                      _lspec(1, D_MODEL), _lspec(D_MODEL, LANES), _lspec(D_MODEL, LANES), _lspec(1, LANES)],
            out_specs=[row(D_MODEL), row(D_MODEL), row(_ROUTE_W)]),
        compiler_params=_params("parallel"),
        name="merge_ln1_router",
    )(lidx, x2d, yc, ya, g, p["w_out"], p["ln1_g"], p["ln1_b"], p["router_hi"],
      p["router_lo"], p["router_b"])


_SLOT_ALIGN = SUBLANES_BF16
_ROW_BLK = 128
_EXPERTS_PER_STEP = 4
_MOE_CHUNK = 256
_META_W = LANES
_MOE_TILE_MAX = 256 * _SLOT_ALIGN


def _moe_plan_kernel(route_ref, dcol_ref, drow_ref, meta_ref):
    tm = route_ref.shape[0]
    blk_rows = min(_MOE_CHUNK, tm)
    lane = lax.broadcasted_iota(I32, (1, LANES), 1).astype(F32)
    hit0 = lane == route_ref[:, 0:1]
    hit1 = lane == route_ref[:, 1:2]
    onehot = jnp.where(hit0, 1.0, jnp.where(hit1, 1.0, 0.0))
    r = lax.broadcasted_iota(I32, (blk_rows, blk_rows), 0)
    c = lax.broadcasted_iota(I32, (blk_rows, blk_rows), 1)
    tri = jnp.where(c < r, 1.0, 0.0).astype(BF16)
    carry = jnp.zeros((1, LANES), F32)
    prefix = []
    for b0 in range(0, tm, blk_rows):
        blk = onehot[b0:b0 + blk_rows]
        prefix.append(jnp.dot(tri, blk.astype(BF16), preferred_element_type=F32) + carry)
        carry = carry + jnp.sum(blk, axis=0, keepdims=True)
    prefix = jnp.concatenate(prefix, axis=0)
    units = jnp.ceil(carry * (1.0 / _SLOT_ALIGN))
    rr = lax.broadcasted_iota(I32, (LANES, LANES), 0)
    cc = lax.broadcasted_iota(I32, (LANES, LANES), 1)
    upper = jnp.where(rr < cc, 1.0, 0.0).astype(BF16)
    units8 = jnp.broadcast_to(units, (8, LANES)).astype(BF16)
    offs = jnp.dot(units8, upper, preferred_element_type=F32)[0:1] * _SLOT_ALIGN
    slot = offs + prefix
    d0 = jnp.sum(jnp.where(hit0, slot, 0.0), axis=1, keepdims=True)
    d1 = jnp.sum(jnp.where(hit1, slot, 0.0), axis=1, keepdims=True)
    rl = lax.broadcasted_iota(I32, (1, LANES), 1)
    rec = jnp.where(rl == 0, d0, jnp.where(rl == 1, d1, jnp.where(rl == 2, route_ref[:, 2:3],
                    jnp.where(rl == 3, route_ref[:, 3:4], 0.0))))
    dcol_ref[...] = rec[:, :_ROUTE_W]
    drow_ref[...] = rec.T[:_ROUTE_W, :]
    nblk = jnp.ceil(units * (_SLOT_ALIGN / _ROW_BLK))
    shift = jnp.where(cc == rr + N_EXPERTS, 1.0, 0.0).astype(BF16)
    nblk_sh = jnp.dot(jnp.broadcast_to(nblk, (8, LANES)).astype(BF16), shift,
                      preferred_element_type=F32)[0:1]
    meta_ref[0] = jnp.where(rl < N_EXPERTS, offs, nblk_sh).astype(I32)


def _moe_plan(route, tm):
    t = route.shape[0]
    nt = t // tm
    assert tm <= _MOE_TILE_MAX
    return pl.pallas_call(
        _moe_plan_kernel,
        out_shape=[jax.ShapeDtypeStruct((t, _ROUTE_W), F32), jax.ShapeDtypeStruct((_ROUTE_W, t), F32),
                   jax.ShapeDtypeStruct((nt, 1, _META_W), I32)],
        grid=(nt,),
        in_specs=[pl.BlockSpec((tm, _ROUTE_W), lambda i: (i, 0))],
        out_specs=[pl.BlockSpec((tm, _ROUTE_W), lambda i: (i, 0)),
                   pl.BlockSpec((_ROUTE_W, tm), lambda i: (0, i)),
                   pl.BlockSpec((1, 1, _META_W), lambda i: (i, 0, 0))],
        compiler_params=_params("parallel"),
        name="moe_plan",
    )(route)


def _n_slots(tm):
    n = 2 * tm + N_EXPERTS * (_SLOT_ALIGN - 1) + _ROW_BLK
    return -(-n // _MOE_CHUNK) * _MOE_CHUNK


def _moe_kernel(meta_ref, l_ref, x1b_ref, x1_ref, drow_ref, dcol_ref, wg_ref, wu_ref, wd_ref, lg_ref, lb_ref,
                x2_ref, xb_ref, yb_ref, gs_ref, *, alpha):
    i, e = pl.program_id(0), pl.program_id(1)
    tm = x1_ref.shape[0]
    n_slots = xb_ref.shape[0]

    @pl.when(e == 0)
    def _dispatch():
        d0, d1 = drow_ref[0:1, :], drow_ref[1:2, :]
        g0, g1 = drow_ref[2:3, :], drow_ref[3:4, :]
        for s0 in range(0, n_slots, _MOE_CHUNK):
            sl = (s0 + lax.broadcasted_iota(I32, (_MOE_CHUNK, 1), 0)).astype(F32)
            a, b = sl == d0, sl == d1
            p = jnp.where(a, 1.0, jnp.where(b, 1.0, 0.0)).astype(BF16)
            xb_ref[s0:s0 + _MOE_CHUNK, :] = jnp.dot(p, x1b_ref[...], preferred_element_type=F32).astype(BF16)
            gs_ref[s0:s0 + _MOE_CHUNK, :] = jnp.sum(jnp.where(a, g0, jnp.where(b, g1, 0.0)), axis=1,
                                                    keepdims=True)
        yb_ref[...] = jnp.zeros(yb_ref.shape, BF16)

    def block(k, start):
        start = pl.multiple_of(start, _SLOT_ALIGN)
        xs = xb_ref[pl.ds(start, _ROW_BLK), :]
        hg = jnp.dot(xs, wg_ref[k], preferred_element_type=F32)
        hu = jnp.dot(xs, wu_ref[k], preferred_element_type=F32)
        h = (hg * jax.nn.sigmoid(hg)) * hu
        y = jnp.dot(h.astype(BF16), wd_ref[k], preferred_element_type=F32)
        yb_ref[pl.ds(start, _ROW_BLK), :] = (y * gs_ref[pl.ds(start, _ROW_BLK), :]).astype(BF16)

    offs = [meta_ref[i * _META_W + e * _EXPERTS_PER_STEP + k] for k in range(_EXPERTS_PER_STEP)]
    nblks = [meta_ref[i * _META_W + N_EXPERTS + e * _EXPERTS_PER_STEP + k] for k in range(_EXPERTS_PER_STEP)]
    one_each = functools.reduce(jnp.logical_and, [n <= 1 for n in nblks])

    @pl.when(one_each)
    def _single_blocks():
        for k in range(_EXPERTS_PER_STEP):
            block(k, offs[k])

    @pl.when(jnp.logical_not(one_each))
    def _ordered_blocks():
        for k in range(_EXPERTS_PER_STEP):
            lax.fori_loop(0, nblks[k], lambda r, carry, k=k: (block(k, offs[k] + r * _ROW_BLK), carry)[1], 0)

    @pl.when(e == pl.num_programs(1) - 1)
    def _combine():
        lane = lax.broadcasted_iota(I32, (1, n_slots), 1).astype(F32)
        rows = min(_MOE_CHUNK, tm)
        for t0 in range(0, tm, rows):
            d0, d1 = dcol_ref[t0:t0 + rows, 0:1], dcol_ref[t0:t0 + rows, 1:2]
            qm = jnp.where(lane == d0, 1.0, jnp.where(lane == d1, 1.0, 0.0)).astype(BF16)
            y = jnp.dot(qm, yb_ref[...], preferred_element_type=F32)
            z = alpha * x1_ref[t0:t0 + rows, :] + y
            x2_ref[t0:t0 + rows, :] = _layer_norm(z, lg_ref[...], lb_ref[...])


def _moe(lidx, x1, x1b, drow, dcol, meta, p, tm, alpha):
    t = x1.shape[0]
    n_slots = _n_slots(tm)
    tile = lambda shape, imap: pl.BlockSpec(shape, imap)
    return pl.pallas_call(
        functools.partial(_moe_kernel, alpha=alpha),
        out_shape=jax.ShapeDtypeStruct((t, D_MODEL), F32),
        grid_spec=pltpu.PrefetchScalarGridSpec(
            num_scalar_prefetch=2, grid=(t // tm, N_EXPERTS // _EXPERTS_PER_STEP),
            in_specs=[tile((tm, D_MODEL), lambda i, e, m, l: (i, 0)),
                      tile((tm, D_MODEL), lambda i, e, m, l: (i, 0)),
                      tile((_ROUTE_W, tm), lambda i, e, m, l: (0, i)),
                      tile((tm, _ROUTE_W), lambda i, e, m, l: (i, 0)),
                      tile((None, _EXPERTS_PER_STEP, D_MODEL, D_EXPERT), lambda i, e, m, l: (l[0], e, 0, 0)),
                      tile((None, _EXPERTS_PER_STEP, D_MODEL, D_EXPERT), lambda i, e, m, l: (l[0], e, 0, 0)),
                      tile((None, _EXPERTS_PER_STEP, D_EXPERT, D_MODEL), lambda i, e, m, l: (l[0], e, 0, 0)),
                      _lspec(1, D_MODEL), _lspec(1, D_MODEL)],
            out_specs=tile((tm, D_MODEL), lambda i, e, m, l: (i, 0)),
            scratch_shapes=[pltpu.VMEM((n_slots, D_MODEL), BF16), pltpu.VMEM((n_slots, D_MODEL), BF16),
                            pltpu.VMEM((n_slots, 1), F32)]),
        compiler_params=_params("parallel", "arbitrary"),
        name="moe_experts_ln2",
    )(meta.reshape(-1), lidx, x1b, x1, drow, dcol, p["w_gate"], p["w_up"], p["w_down"], p["ln2_g"], p["ln2_b"])


def _token_tile(t, want):
    return want if t % want == 0 else t


def _decoder_layer(lidx, x, prefix, attend, p, alpha, key_transposed):
    nb, s, _ = x.shape
    t = nb * s
    x2d = x.reshape(t, D_MODEL)
    u, q_b, k, v, qi_b, ki, wi, g, k_b, v_b, ki_b = _inproj(lidx, x2d, p["w_in"], p["b_in"], _token_tile(t, 512),
                                                            (nb, s) if key_transposed else None)

    ctm = 512 if s % 512 == 0 else _HALO
    u3 = u.reshape(nb, s, C_CONV)
    history = jnp.pad(prefix, ((0, 0), (_HALO - (CONV_W - 1), 0), (0, 0)))
    u_rows = jnp.pad(u3, ((0, 0), (0, (-s) % ctm), (0, 0)))
    yc = _conv_branch(lidx, history, u_rows, p["conv_w"], p["conv_b"], p["conv_ln_g"], p["conv_ln_b"],
                      p["w_conv_out"], ctm)
    yc = yc[:, :s].reshape(t, D_MODEL)

    ya = attend(q_b, k_b, v_b, qi_b, ki_b, wi, lidx=lidx, w_attn_out=p["w_attn_out"])
    x1, x1b, route = _merge(lidx, x2d, yc, ya, g, p, _token_tile(t, 256), alpha)
    mtm = _token_tile(t, 1024)
    dcol, drow, meta = _moe_plan(route, mtm)
    x2 = _moe(lidx, x1, x1b, drow, dcol, meta, p, mtm, alpha)

    new_conv = jnp.concatenate([prefix, u3], axis=1)[:, -(CONV_W - 1):]
    if key_transposed:
        per_head = lambda a: a.reshape(nb, N_KV_HEADS, HEAD_DIM, s).transpose(0, 3, 1, 2)
    else:
        per_head = lambda a: a.reshape(nb, s, N_KV_HEADS, HEAD_DIM)
    return x2.reshape(nb, s, D_MODEL), per_head(k), per_head(v), ki.reshape(nb, s, IDX_DIM), new_conv


def _prepare_params(w_in, b_in, conv_w, conv_b, conv_ln_g, conv_ln_b, w_conv_out, w_attn_out, w_out,
                    ln1_g, ln1_b, router_group_w, router_group_b, router_expert_w, router_expert_b,
                    w_gate, w_up, w_down, ln2_g, ln2_b):
    depth = w_in.shape[0]
    pad = _COL_G - _N_IN_HEAD
    w_pad = jnp.concatenate([w_in[..., :_N_IN_HEAD], jnp.zeros((depth, D_MODEL, pad), F32),
                             w_in[..., _N_IN_HEAD:]], axis=-1).astype(BF16)
    b_pad = jnp.concatenate([b_in[..., :_N_IN_HEAD], jnp.zeros((depth, pad), F32),
                             b_in[..., _N_IN_HEAD:]], axis=-1)[:, None, :]
    rpad = LANES - N_EXPERTS - N_GROUPS
    rw = jnp.concatenate([router_expert_w, router_group_w, jnp.zeros((depth, D_MODEL, rpad), F32)], axis=-1)
    rb = jnp.concatenate([router_expert_b, router_group_b, jnp.zeros((depth, rpad), F32)], axis=-1)
    rw_hi = rw.astype(BF16)
    rw_lo = (rw - rw_hi.astype(F32)).astype(BF16)
    vec = lambda a: a[:, None, :]
    return dict(w_in=w_pad, b_in=b_pad, conv_w=conv_w, conv_b=vec(conv_b), conv_ln_g=vec(conv_ln_g),
                conv_ln_b=vec(conv_ln_b), w_conv_out=w_conv_out.astype(BF16),
                w_attn_out=w_attn_out.astype(BF16), w_out=w_out.astype(BF16), ln1_g=vec(ln1_g),
                ln1_b=vec(ln1_b), router_hi=rw_hi, router_lo=rw_lo, router_b=vec(rb),
                w_gate=w_gate.astype(BF16), w_up=w_up.astype(BF16), w_down=w_down.astype(BF16),
                ln2_g=vec(ln2_g), ln2_b=vec(ln2_b))


def kernel(x_prompt, x_sample, cache_k, cache_v, cache_kidx, state_conv, page_table, w_in, b_in, conv_w,
           conv_b, conv_ln_g, conv_ln_b, w_conv_out, w_attn_out, w_out, ln1_g, ln1_b, router_group_w,
           router_group_b, router_expert_w, router_expert_b, w_gate, w_up, w_down, ln2_g, ln2_b):
    params = _prepare_params(w_in, b_in, conv_w, conv_b, conv_ln_g, conv_ln_b, w_conv_out, w_attn_out,
                             w_out, ln1_g, ln1_b, router_group_w, router_group_b, router_expert_w,
                             router_expert_b, w_gate, w_up, w_down, ln2_g, ln2_b)
    nb, s, _ = x_prompt.shape
    db, ds, _ = x_sample.shape
    depth, n_phys = cache_k.shape[:2]
    alpha = (2 * depth) ** 0.25
    k_pool = cache_k.transpose(0, 1, 3, 4, 2).reshape(depth * n_phys, KV_W, PAGE_SIZE)
    v_pool = cache_v.transpose(0, 1, 3, 4, 2).reshape(depth * n_phys, KV_W, PAGE_SIZE)
    ki_pool = cache_kidx.transpose(0, 1, 3, 2).reshape(depth * n_phys, IDX_DIM, PAGE_SIZE)
    conv_zero = jnp.zeros((nb, CONV_W - 1, C_CONV), F32)

    def layer(carry, xs):
        xp, xsm = carry
        l, st = xs
        lidx = l.reshape(1)
        prompt_attend = functools.partial(_dsa_prompt, nb=nb, s=s)
        xp, *new_p = _decoder_layer(lidx, xp, conv_zero, prompt_attend, params, alpha, key_transposed=True)
        sample_attend = functools.partial(_dsa_sample, k_pool=k_pool, v_pool=v_pool, ki_pool=ki_pool,
                                          page_table=page_table + l * n_phys, nb=db, s=ds)
        xsm, *new_s = _decoder_layer(lidx, xsm, st, sample_attend, params, alpha, key_transposed=False)
        return (xp, xsm), (tuple(new_p), tuple(new_s))

    (xp, xsm), (new_p, new_s) = lax.scan(layer, (x_prompt, x_sample),
                                         (jnp.arange(depth, dtype=I32), state_conv))
    return (xp, xsm, *new_p, *new_s)
```

```python
import functools

import jax
import jax.numpy as jnp
from jax import lax
from jax.experimental import pallas as pl
from jax.experimental.pallas import tpu as pltpu

F32 = jnp.float32
BF16 = jnp.bfloat16
I32 = jnp.int32

D_MODEL = 1024
PAGE_SIZE = 128
C_CONV = 512
CONV_W = 31
N_HEADS = 8
N_KV_HEADS = 4
HEAD_DIM = 64
Q_PER_KV = N_HEADS // N_KV_HEADS
ATTN_W = N_HEADS * HEAD_DIM
KV_W = N_KV_HEADS * HEAD_DIM
N_IDX_HEADS = 8
IDX_DIM = 64
TOPK_MAX = 256
N_GROUPS = 4
EXP_PER_GROUP = 8
N_EXPERTS = N_GROUPS * EXP_PER_GROUP
D_EXPERT = 256
LN_EPS = 1e-5

LANES = 128
_SUBLANES = 8
SUBLANES_BF16 = 16
VMEM_LIMIT = 56 * 1024 * 1024

_N_SMALL = IDX_DIM + N_IDX_HEADS
_COL_CA = 0
_COL_CB = _COL_CA + C_CONV
_COL_Q = _COL_CB + C_CONV
_COL_K = _COL_Q + ATTN_W
_COL_V = _COL_K + KV_W
_COL_QI = _COL_V + KV_W
_COL_KI = _COL_QI + N_IDX_HEADS * IDX_DIM
_COL_G = _COL_KI + LANES
_N_IN_PAD = _COL_G + 2 * D_MODEL
_N_IN_HEAD = _COL_KI + _N_SMALL

_NT = (((1,), (1,)), ((), ()))
_NEG = -1e30
_LOG2E = 1.4426950408889634


def _params(*sem):
    return pltpu.CompilerParams(dimension_semantics=sem, vmem_limit_bytes=VMEM_LIMIT)


def _lspec(*shape):
    nd = len(shape)
    return pl.BlockSpec((None,) + shape, lambda *a: (a[-1][0],) + (0,) * nd)


def _layer_norm(x, g, b):
    mu = jnp.mean(x, axis=-1, keepdims=True)
    xc = x - mu
    var = jnp.mean(xc * xc, axis=-1, keepdims=True)
    return xc * lax.rsqrt(var + LN_EPS) * g + b


def _inproj_kernel(l_ref, x_ref, w_ref, b_ref, u_ref, q_ref, k_ref, v_ref, qi_ref, ki_ref, wi_ref, g_ref,
                   kb_ref, vb_ref, kib_ref, *, key_transposed):
    xb = x_ref[...].astype(BF16)

    def proj(c0, n):
        return jnp.dot(xb, w_ref[:, c0:c0 + n], preferred_element_type=F32) + b_ref[:, c0:c0 + n]

    u_ref[...] = proj(_COL_CA, C_CONV) * jax.nn.sigmoid(proj(_COL_CB, C_CONV))
    q_ref[...] = (proj(_COL_Q, ATTN_W) * (HEAD_DIM ** -0.5 * _LOG2E)).astype(BF16)
    k = proj(_COL_K, KV_W)
    v = proj(_COL_V, KV_W)
    if key_transposed:
        k_ref[0] = k.T
        v_t = v.T
        v_ref[0] = v_t
        vb_ref[0] = v_t.astype(BF16)
        for g in range(N_KV_HEADS):
            kb_ref[0, g] = k[:, g * HEAD_DIM:(g + 1) * HEAD_DIM].astype(BF16)
    else:
        k_ref[...] = k
        kb_ref[...] = k.astype(BF16)
        v_ref[...] = v
        vb_ref[...] = v.astype(BF16)
    qi_ref[...] = (proj(_COL_QI, N_IDX_HEADS * IDX_DIM) * (IDX_DIM ** -0.5)).astype(BF16)
    small = proj(_COL_KI, LANES)
    ki = small[:, :IDX_DIM]
    ki_ref[...] = ki
    kib_ref[...] = ki.astype(BF16)
    wi_ref[...] = small[:, IDX_DIM:_N_SMALL] * (N_IDX_HEADS ** -0.5)
    g_ref[...] = jax.nn.sigmoid(proj(_COL_G, 2 * D_MODEL)).astype(BF16)


def _inproj(lidx, x2d, w_pad, b_pad, tm, seq=None):
    t = x2d.shape[0]
    rows = lambda n, dt: (jax.ShapeDtypeStruct((t, n), dt), pl.BlockSpec((tm, n), lambda i, l: (i, 0)))
    if seq is None:
        k, v, kb, vb = rows(KV_W, F32), rows(KV_W, F32), rows(KV_W, BF16), rows(KV_W, BF16)
    else:
        nb, s = seq
        tps = s // tm
        kt = lambda dt: (jax.ShapeDtypeStruct((nb, KV_W, s), dt),
                         pl.BlockSpec((1, KV_W, tm), lambda i, l: (i // tps, 0, i % tps)))
        k, v, vb = kt(F32), kt(F32), kt(BF16)
        kb = (jax.ShapeDtypeStruct((nb, N_KV_HEADS, s, HEAD_DIM), BF16),
              pl.BlockSpec((1, N_KV_HEADS, tm, HEAD_DIM), lambda i, l: (i // tps, 0, i % tps, 0)))
    outs = [rows(C_CONV, F32), rows(ATTN_W, BF16), k, v, rows(N_IDX_HEADS * IDX_DIM, BF16), rows(IDX_DIM, F32),
            rows(N_IDX_HEADS, F32), rows(2 * D_MODEL, BF16), kb, vb, rows(IDX_DIM, BF16)]
    return pl.pallas_call(
        functools.partial(_inproj_kernel, key_transposed=seq is not None),
        out_shape=[o[0] for o in outs],
        grid_spec=pltpu.PrefetchScalarGridSpec(
            num_scalar_prefetch=1, grid=(t // tm,),
            in_specs=[pl.BlockSpec((tm, D_MODEL), lambda i, l: (i, 0)),
                      _lspec(D_MODEL, _N_IN_PAD), _lspec(1, _N_IN_PAD)],
            out_specs=[o[1] for o in outs]),
        compiler_params=_params("parallel"),
        name="inproj",
    )(lidx, x2d, w_pad, b_pad)


_HALO = 32
_CONV_ROWS = 32


def _conv_kernel(l_ref, first_ref, prev_ref, cur_ref, cw_ref, cb_ref, lg_ref, lb_ref, wo_ref, y_ref,
                 hist_ref, acc_ref, shift_ref):
    tm = cur_ref.shape[1]
    at_start = pl.program_id(1) == 0
    hist_ref[0:_HALO, :] = jnp.where(at_start, first_ref[0], prev_ref[0])
    hist_ref[_HALO:_HALO + tm, :] = cur_ref[0]
    first = _HALO - (CONV_W - 1)
    for b in range(_SUBLANES):
        rows = tm + _SUBLANES * ((CONV_W - 1 - b) // _SUBLANES)
        shift_ref[b, 0:rows, :] = hist_ref[first + b:first + b + rows, :]
    for r0 in range(0, tm, _CONV_ROWS):
        acc = jnp.zeros((_CONV_ROWS, C_CONV), F32)
        for j in range(CONV_W):
            a, b = divmod(j, _SUBLANES)
            acc = acc + cw_ref[j:j + 1, :] * shift_ref[b, r0 + a * _SUBLANES:r0 + a * _SUBLANES + _CONV_ROWS, :]
        acc_ref[r0:r0 + _CONV_ROWS, :] = acc + cb_ref[...]
    y = _layer_norm(acc_ref[...], lg_ref[...], lb_ref[...])
    y = y * jax.nn.sigmoid(y)
    y_ref[0] = jnp.dot(y.astype(BF16), wo_ref[...], preferred_element_type=F32).astype(BF16)


def _conv_branch(lidx, history, u, conv_w, conv_b, ln_g, ln_b, w_out_bf, tm):
    nb, length, _ = u.shape
    halo_blocks = tm // _HALO
    return pl.pallas_call(
        _conv_kernel,
        out_shape=jax.ShapeDtypeStruct((nb, length, D_MODEL), BF16),
        grid_spec=pltpu.PrefetchScalarGridSpec(
            num_scalar_prefetch=1, grid=(nb, length // tm),
            in_specs=[pl.BlockSpec((1, _HALO, C_CONV), lambda b, i, l: (b, 0, 0)),
                      pl.BlockSpec((1, _HALO, C_CONV), lambda b, i, l: (b, jnp.maximum(i * halo_blocks - 1, 0), 0)),
                      pl.BlockSpec((1, tm, C_CONV), lambda b, i, l: (b, i, 0)),
                      _lspec(CONV_W, C_CONV), _lspec(1, C_CONV), _lspec(1, C_CONV), _lspec(1, C_CONV),
                      _lspec(C_CONV, D_MODEL)],
            out_specs=pl.BlockSpec((1, tm, D_MODEL), lambda b, i, l: (b, i, 0)),
            scratch_shapes=[pltpu.VMEM((_HALO + tm, C_CONV), F32), pltpu.VMEM((tm, C_CONV), F32),
                            pltpu.VMEM((_SUBLANES, tm + _HALO - _SUBLANES, C_CONV), F32)]),
        compiler_params=_params("parallel", "parallel"),
        name="conv_branch",
    )(lidx, history, u, u, conv_w, conv_b, ln_g, ln_b, w_out_bf)


_BISECT_STEPS = 14


_FOLD_ROWS = 64


def _reduce_rows(x, op):
    rows, cols = x.shape
    if rows > _FOLD_ROWS:
        x = op(x.reshape(rows // _FOLD_ROWS, _FOLD_ROWS, cols), axis=0)
    x = op(x.reshape(x.shape[0] // _SUBLANES, _SUBLANES, cols), axis=0)
    return op(x, axis=0, keepdims=True)


def _fold_keys(sc_ref, n_chunks, tk, init, fn, op, merge, first=0):
    def body(c, acc):
        off = pl.multiple_of(c * tk, tk)
        v = fn(sc_ref[pl.ds(off, tk), :], off).reshape(tk // _FOLD_ROWS, _FOLD_ROWS, cols)
        return merge(acc, op(v, axis=0))

    cols = sc_ref.shape[1]
    acc = lax.fori_loop(first, n_chunks, body, jnp.full((_FOLD_ROWS, cols), init, F32))
    return _reduce_rows(acc, op)


def _count_keys(sc_ref, n_chunks, tk, fn, first=0):
    return _fold_keys(sc_ref, n_chunks, tk, 0.0, fn, jnp.sum, jnp.add, first)


def _count_ge(sc_ref, n_chunks, tk, thr):
    return _count_keys(sc_ref, n_chunks, tk, lambda blk, off: jnp.where(blk >= thr, 1.0, 0.0))


def _any(flag):
    return jnp.max(jnp.where(flag, 1, 0))


def _select_threshold(sc_ref, st_ref, n_chunks, tk, kt, s_min, s_max, n_valid):
    lo_ref, hi_ref, cl_ref, done_ref, jlo_ref, jhi_ref = (st_ref.at[n:n + 1] for n in range(6))
    c_max = _count_ge(sc_ref, n_chunks, tk, s_max)
    top = c_max >= kt
    lo_ref[...] = jnp.where(top, s_max, s_min)
    hi_ref[...] = s_max
    cl0 = jnp.where(top, c_max, n_valid)
    cl_ref[...] = cl0
    done0 = jnp.where(top, 1.0, jnp.where(cl0 == kt, 1.0, 0.0))
    done_ref[...] = done0

    def bisect_step():
        lo, hi, cl, done = lo_ref[...], hi_ref[...], cl_ref[...], done_ref[...]
        mid = 0.5 * lo + 0.5 * hi
        c = _count_ge(sc_ref, n_chunks, tk, mid)
        act = done == 0.0
        up = c >= kt
        lo_ref[...] = jnp.where(act, jnp.where(up, mid, lo), lo)
        cl_ref[...] = jnp.where(act, jnp.where(up, c, cl), cl)
        hi_ref[...] = jnp.where(act, jnp.where(up, hi, mid), hi)
        done_ref[...] = jnp.where(act, jnp.where(c == kt, 1.0, 0.0), done)

    @pl.when(_any(done0 == 0.0) > 0)
    def _narrow():
        lax.fori_loop(0, _BISECT_STEPS, lambda _, carry: (bisect_step(), carry)[1], 0)

    def snap(_):
        lo, hi, cl, done = lo_ref[...], hi_ref[...], cl_ref[...], done_ref[...]
        t1 = _fold_keys(sc_ref, n_chunks, tk, -jnp.inf, lambda blk, off: jnp.where(blk < hi, blk, -jnp.inf),
                        jnp.max, jnp.maximum)
        c1 = _count_ge(sc_ref, n_chunks, tk, t1)
        act = done == 0.0
        found = c1 >= kt
        lo_ref[...] = jnp.where(act, jnp.where(found, t1, lo), lo)
        cl_ref[...] = jnp.where(act, jnp.where(found, c1, cl), cl)
        hi_ref[...] = jnp.where(act, jnp.where(found, hi, t1), hi)
        done_new = jnp.where(act, jnp.where(found, 1.0, 0.0), done)
        done_ref[...] = done_new
        return _any(done_new == 0.0)

    lax.while_loop(lambda f: f > 0, snap, _any(done_ref[...] == 0.0))

    thr = lo_ref[...]

    @pl.when(_any(cl_ref[...] != kt) > 0)
    def _cut_ties():
        tied = cl_ref[...] != kt
        need = kt - _count_keys(sc_ref, n_chunks, tk, lambda blk, off: jnp.where(blk > thr, 1.0, 0.0))
        row = lax.broadcasted_iota(I32, (tk, 1), 0)

        def locate(c, carry):
            seen, chunk, seen_before = carry
            off = pl.multiple_of(c * tk, tk)
            cnt = _reduce_rows(jnp.where(sc_ref[pl.ds(off, tk), :] == thr, 1.0, 0.0), jnp.sum)
            here = jnp.where(seen < need, jnp.where(seen + cnt >= need, 1.0, 0.0), 0.0)
            chunk = jnp.where(here > 0.0, jnp.asarray(c, F32), chunk)
            seen_before = jnp.where(here > 0.0, seen, seen_before)
            return seen + cnt, chunk, seen_before

        zero = jnp.zeros_like(thr)
        _, chunk, seen_before = lax.fori_loop(0, n_chunks, locate, (zero, zero, zero))
        first_key = chunk * float(tk)
        need_here = need - seen_before
        c_first = jnp.min(jnp.where(tied, chunk, float(sc_ref.shape[0]))).astype(I32)
        c_last = jnp.max(jnp.where(tied, chunk, -1.0)).astype(I32)

        jlo_ref[...] = first_key - 1.0
        jhi_ref[...] = first_key + float(tk - 1)

        def step(_, carry):
            jlo, jhi = jlo_ref[...], jhi_ref[...]
            mid = jnp.floor(0.5 * (jlo + jhi))

            def in_range(blk, off):
                idx = (off + row).astype(F32)
                return jnp.where(blk == thr, jnp.where(idx <= mid, jnp.where(idx >= first_key, 1.0, 0.0), 0.0), 0.0)

            ok = _count_keys(sc_ref, c_last + 1, tk, in_range, c_first) >= need_here
            jhi_ref[...] = jnp.where(ok, mid, jhi)
            jlo_ref[...] = jnp.where(ok, jlo, mid)
            return carry

        lax.fori_loop(0, tk.bit_length() - 1, step, 0)
        cut = jnp.where(tied, jhi_ref[...], float(sc_ref.shape[0]))

        def demote(c, carry):
            off = pl.multiple_of(c * tk, tk)
            blk = sc_ref[pl.ds(off, tk), :]
            beyond = (off + row).astype(F32) > cut
            sc_ref[pl.ds(off, tk), :] = jnp.where(blk == thr, jnp.where(beyond, -jnp.inf, blk), blk)
            return carry

        lax.fori_loop(c_first, n_chunks, demote, 0)

    return thr


def _min_max_init(cols):
    return (jnp.full((_FOLD_ROWS, cols), jnp.inf, F32), jnp.full((_FOLD_ROWS, cols), -jnp.inf, F32))


def _min_max_update(mn, mx, score, valid):
    shape = (score.shape[0] // _FOLD_ROWS, _FOLD_ROWS, score.shape[1])
    mn = jnp.minimum(mn, jnp.min(jnp.where(valid, score, jnp.inf).reshape(shape), axis=0))
    mx = jnp.maximum(mx, jnp.max(jnp.where(valid, score, -jnp.inf).reshape(shape), axis=0))
    return mn, mx


_TQ = 256
_TK = 512


_IDX_PAIRS = N_IDX_HEADS // 2
_ONES_ROWS = SUBLANES_BF16


def _dsa_prompt_kernel(l_ref, qi_ref, wi_ref, ki_ref, q_ref, k_ref, vt_ref, wao_ref, y_ref,
                       sc_ref, st_ref, m_ref, acc_ref, s_ref, cm_ref, p_ref, al_ref):
    tq, tk = _TQ, _TK
    i = pl.program_id(1)
    n_chunks = ((i + 1) * tq + tk - 1) // tk
    qpos = i * tq + lax.broadcasted_iota(I32, (1, tq), 1)
    krow = lax.broadcasted_iota(I32, (tk, 1), 0)

    def scores(c, carry):
        off = pl.multiple_of(c * tk, tk)
        kc = ki_ref[0, pl.ds(off, tk), :]
        score = None
        for p in range(_IDX_PAIRS):
            d = lax.dot_general(kc, qi_ref[0, 0, p], _NT, preferred_element_type=F32)
            for j in range(2):
                h = 2 * p + j
                t = wi_ref[0, h:h + 1, :] * jnp.maximum(d[:, j * tq:(j + 1) * tq], 0.0)
                score = t if score is None else score + t
        valid = off + krow <= qpos
        sc_ref[pl.ds(off, tk), :] = jnp.where(valid, score, -jnp.inf)
        return _min_max_update(*carry, score, valid)

    mn, mx = lax.fori_loop(0, n_chunks, scores, _min_max_init(tq))
    n_valid = (qpos + 1).astype(F32)
    kt = jnp.minimum(n_valid, float(TOPK_MAX))
    thr = _select_threshold(sc_ref, st_ref, n_chunks, tk, kt, _reduce_rows(mn, jnp.min),
                            _reduce_rows(mx, jnp.max), n_valid)

    m_ref[...] = jnp.full(m_ref.shape, _NEG, F32)
    acc_ref[...] = jnp.zeros(acc_ref.shape, F32)

    def mask_bias(c):
        off = pl.multiple_of(c * tk, tk)
        bias = jnp.where(sc_ref[pl.ds(off, tk), :] >= thr, 0.0, -jnp.inf)
        return jnp.concatenate([bias] * Q_PER_KV, axis=1)

    def qk_stage(c, slot, groups, bias=None):
        off = pl.multiple_of(c * tk, tk)
        bias = mask_bias(c) if bias is None else bias
        for g in groups:
            s = lax.dot_general(k_ref[0, g, pl.ds(off, tk), :], q_ref[0, 0, g], _NT,
                                preferred_element_type=F32) + bias
            s_ref[slot, g] = s
            cm_ref[slot, g] = _reduce_rows(s, jnp.max)

    def exp_stage(slot, groups):
        for g in groups:
            m_old = m_ref[g]
            m_new = jnp.maximum(m_old, cm_ref[slot, g])
            al_ref[slot, g] = jnp.exp2(m_old - m_new)
            p_ref[slot, g] = jnp.exp2(s_ref[slot, g] - m_new).astype(BF16)
            m_ref[g] = m_new

    def pv_stage(c, slot, groups):
        off = pl.multiple_of(c * tk, tk)
        for g in groups:
            v_ones =jnp.concatenate([vt_ref[0, g, :, pl.ds(off, tk)], jnp.ones((_ONES_ROWS, tk), BF16)], axis=0)
            acc_ref[g] = al_ref[slot, g] * acc_ref[g] + jnp.dot(v_ones, p_ref[slot, g],
                                                                preferred_element_type=F32)

    every = range(N_KV_HEADS)

    def attend(c, carry):
        slot = c % 2
        for g in every:
            pv_stage(c, slot, (g,))
            qk_stage(c + 2, slot, (g,))
            exp_stage(1 - slot, (g,))
        return carry

    for g in every:
        qk_stage(0, 0, (g,))
        exp_stage(0, (g,))

    @pl.when(n_chunks > 1)
    def _fill():
        qk_stage(1, 1, every)

    lax.fori_loop(0, n_chunks - 2, attend, 0)
    last = n_chunks - 1

    @pl.when(n_chunks > 1)
    def _drain():
        for g in every:
            pv_stage(last - 1, (last - 1) % 2, (g,))
            exp_stage(last % 2, (g,))
            pv_stage(last, last % 2, (g,))

    @pl.when(n_chunks == 1)
    def _only_chunk():
        pv_stage(0, 0, every)
    heads = []
    for g in range(N_KV_HEADS):
        o_g = acc_ref[g, 0:HEAD_DIM, :] / acc_ref[g, HEAD_DIM:HEAD_DIM + 1, :]
        heads += [o_g[:, j * tq:(j + 1) * tq] for j in range(Q_PER_KV)]
    o_t = jnp.concatenate(heads, axis=0).astype(BF16)
    y_ref[0] = lax.dot_general(o_t, wao_ref[...], (((0,), (0,)), ((), ())),
                               preferred_element_type=F32).astype(BF16)


def _dsa_prompt(q_b, k_hm, vt_b, qi_b, ki_b, wi, nb, s, lidx, w_attn_out):
    tq = _TQ
    nq = s // tq
    cols = Q_PER_KV * tq

    def pair_major(a, n_pairs):
        a = a.reshape(nb, nq, tq, n_pairs, 2, a.shape[-1] // (2 * n_pairs)).transpose(0, 1, 3, 4, 2, 5)
        return a.reshape(nb, nq, n_pairs, 2 * tq, -1)

    kh = k_hm
    vt = vt_b.reshape(nb, N_KV_HEADS, HEAD_DIM, s)
    wit = wi.reshape(nb, s, N_IDX_HEADS).transpose(0, 2, 1)
    once = dict(pipeline_mode=pl.Buffered(1))
    y = pl.pallas_call(
        _dsa_prompt_kernel,
        out_shape=jax.ShapeDtypeStruct((nb, s, D_MODEL), BF16),
        grid_spec=pltpu.PrefetchScalarGridSpec(
            num_scalar_prefetch=1, grid=(nb, nq),
            in_specs=[pl.BlockSpec((1, 1, _IDX_PAIRS, 2 * tq, IDX_DIM), lambda b, i, l: (b, i, 0, 0, 0)),
                      pl.BlockSpec((1, N_IDX_HEADS, tq), lambda b, i, l: (b, 0, i)),
                      pl.BlockSpec((1, s, IDX_DIM), lambda b, i, l: (b, 0, 0), **once),
                      pl.BlockSpec((1, 1, N_KV_HEADS, cols, HEAD_DIM), lambda b, i, l: (b, i, 0, 0, 0)),
                      pl.BlockSpec((1, N_KV_HEADS, s, HEAD_DIM), lambda b, i, l: (b, 0, 0, 0), **once),
                      pl.BlockSpec((1, N_KV_HEADS, HEAD_DIM, s), lambda b, i, l: (b, 0, 0, 0), **once),
                      _lspec(ATTN_W, D_MODEL)],
            out_specs=pl.BlockSpec((1, tq, D_MODEL), lambda b, i, l: (b, i, 0)),
            scratch_shapes=[pltpu.VMEM((s, tq), F32), pltpu.VMEM((_SUBLANES, tq), F32),
                            pltpu.VMEM((N_KV_HEADS, 1, cols), F32),
                            pltpu.VMEM((N_KV_HEADS, HEAD_DIM + _ONES_ROWS, cols), F32),
                            pltpu.VMEM((2, N_KV_HEADS, _TK, cols), F32), pltpu.VMEM((2, N_KV_HEADS, 1, cols), F32),
                            pltpu.VMEM((2, N_KV_HEADS, _TK, cols), BF16), pltpu.VMEM((2, N_KV_HEADS, 1, cols), F32)]),
        compiler_params=_params("parallel", "arbitrary"),
        name="dsa_prompt",
    )(lidx, pair_major(qi_b, _IDX_PAIRS), wit, ki_b.reshape(nb, s, IDX_DIM), pair_major(q_b, N_KV_HEADS), kh, vt,
      w_attn_out)
    return y.reshape(nb * s, D_MODEL)


_QPAD = 8
_PAGES_SC = 32
_PAGES_KV = 16


def _sample_scores_kernel(pt_ref, qi_ref, wi_ref, knt_ref, *rest):
    pages, sc_ref = rest[:_PAGES_SC], rest[_PAGES_SC]
    j = pl.program_id(1)
    last = pl.num_programs(1) - 1

    def scores(kt):
        d = jnp.dot(qi_ref[0], kt, preferred_element_type=F32)
        score = None
        for h in range(N_IDX_HEADS):
            t = wi_ref[0, :, h:h + 1] * jnp.maximum(d[h * _QPAD:(h + 1) * _QPAD, :], 0.0)
            score = t if score is None else score + t
        return score

    @pl.when(j < last)
    def _past():
        sc_ref[0] = scores(jnp.concatenate([pg[0] for pg in pages], axis=1).astype(BF16))

    @pl.when(j == last)
    def _new():
        sc_ref[0] = jnp.zeros(sc_ref.shape[1:], F32)
        sc_ref[0, :, 0:PAGE_SIZE] = scores(knt_ref[0])


def _sample_select_kernel(sc_in_ref, bias_ref, sc_ref, st_ref, *, past, dec_seq):
    tk = _TK
    n_chunks = sc_ref.shape[0] // tk
    q = lax.broadcasted_iota(I32, (1, LANES), 1) % dec_seq
    krow = lax.broadcasted_iota(I32, (tk, 1), 0)

    def load(c, carry):
        off = pl.multiple_of(c * tk, tk)
        s = sc_in_ref[pl.ds(off, tk), :]
        valid = off + krow <= past + q
        sc_ref[pl.ds(off, tk), :] = jnp.where(valid, s, -jnp.inf)
        return _min_max_update(*carry, s, valid)

    mn, mx = lax.fori_loop(0, n_chunks, load, _min_max_init(LANES))
    n_valid = (past + 1 + q).astype(F32)
    kt = jnp.minimum(n_valid, float(min(TOPK_MAX, (past + dec_seq) // 4)))
    thr = _select_threshold(sc_ref, st_ref, n_chunks, tk, kt, _reduce_rows(mn, jnp.min),
                            _reduce_rows(mx, jnp.max), n_valid)

    def emit(c, carry):
        off = pl.multiple_of(c * tk, tk)
        bias_ref[pl.ds(off, tk), :] = jnp.where(sc_ref[pl.ds(off, tk), :] >= thr, 0.0, -jnp.inf)
        return carry

    lax.fori_loop(0, n_chunks, emit, 0)


def _sample_attend_kernel(pt_ref, l_ref_, q_ref, bias_ref, knt_ref, vnt_ref, wao_ref, *rest):
    kpages, vpages = rest[:_PAGES_KV], rest[_PAGES_KV:2 * _PAGES_KV]
    y_ref, m_ref, l_ref, acc_ref = rest[2 * _PAGES_KV:]
    j = pl.program_id(1)
    last = pl.num_programs(1) - 1

    @pl.when(j == 0)
    def _init():
        m_ref[...] = jnp.full(m_ref.shape, _NEG, F32)
        l_ref[...] = jnp.zeros(l_ref.shape, F32)
        acc_ref[...] = jnp.zeros(acc_ref.shape, F32)

    def attend(kt, vt, bias):
        s = jnp.dot(q_ref[0], kt, preferred_element_type=F32)
        s = jnp.concatenate([s[h * _QPAD:(h + 1) * _QPAD] + bias for h in range(N_HEADS)], axis=0)
        m_old = m_ref[...]
        m_new = jnp.maximum(m_old, jnp.max(s, axis=1, keepdims=True))
        alpha = jnp.exp2(m_old - m_new)
        p = jnp.exp2(s - m_new)
        l_ref[...] = alpha * l_ref[...] + jnp.sum(p, axis=1, keepdims=True)
        acc_ref[...] = alpha * acc_ref[...] + lax.dot_general(p.astype(BF16), vt, _NT,
                                                              preferred_element_type=F32)
        m_ref[...] = m_new

    @pl.when(j < last)
    def _past():
        attend(jnp.concatenate([pg[0] for pg in kpages], axis=1).astype(BF16),
               jnp.concatenate([pg[0] for pg in vpages], axis=1).astype(BF16), bias_ref[0])

    @pl.when(j == last)
    def _new():
        attend(knt_ref[0], vnt_ref[0], bias_ref[0, :, 0:PAGE_SIZE])
        o = acc_ref[...] / l_ref[...]
        y = None
        for h in range(N_HEADS):
            g = h // Q_PER_KV
            o_h = o[h * _QPAD:(h + 1) * _QPAD, g * HEAD_DIM:(g + 1) * HEAD_DIM].astype(BF16)
            t = jnp.dot(o_h, wao_ref[h * HEAD_DIM:(h + 1) * HEAD_DIM, :], preferred_element_type=F32)
            y = t if y is None else y + t
        y_ref[0] = y.astype(BF16)


def _dsa_sample(q_b, k_b, v_b, qi_b, ki_b, wi, k_pool, v_pool, ki_pool, page_table, nb, s, lidx, w_attn_out):
    n_pages = page_table.shape[1]
    past = n_pages * PAGE_SIZE
    assert _QPAD % s == 0 and (nb * _QPAD) % LANES == 0
    assert n_pages % _PAGES_SC == 0 and n_pages % _PAGES_KV == 0
    dup = jnp.arange(_QPAD) % s
    qi = qi_b.reshape(nb, s, N_IDX_HEADS, IDX_DIM)[:, dup].transpose(0, 2, 1, 3)
    qi = qi.reshape(nb, N_IDX_HEADS * _QPAD, IDX_DIM)
    wi8 = wi.reshape(nb, s, N_IDX_HEADS)[:, dup]
    new_page = lambda a: jnp.pad(a.reshape(nb, s, -1).transpose(0, 2, 1), ((0, 0), (0, 0), (0, PAGE_SIZE - s)))
    kint, knt, vnt = new_page(ki_b), new_page(k_b), new_page(v_b)

    n_sc = n_pages // _PAGES_SC
    sc_w = _PAGES_SC * PAGE_SIZE

    def page_spec(width, per_step, r):
        return pl.BlockSpec((1, width, PAGE_SIZE),
                            lambda b, j, pt, *_: (pt[b, jnp.minimum(j * per_step + r, n_pages - 1)], 0, 0))

    scores = pl.pallas_call(
        _sample_scores_kernel,
        out_shape=jax.ShapeDtypeStruct((nb, _QPAD, (n_sc + 1) * sc_w), F32),
        grid_spec=pltpu.PrefetchScalarGridSpec(
            num_scalar_prefetch=1, grid=(nb, n_sc + 1),
            in_specs=[pl.BlockSpec((1, N_IDX_HEADS * _QPAD, IDX_DIM), lambda b, j, pt: (b, 0, 0)),
                      pl.BlockSpec((1, _QPAD, N_IDX_HEADS), lambda b, j, pt: (b, 0, 0)),
                      pl.BlockSpec((1, IDX_DIM, PAGE_SIZE), lambda b, j, pt: (b, 0, 0))]
                     + [page_spec(IDX_DIM, _PAGES_SC, r) for r in range(_PAGES_SC)],
            out_specs=pl.BlockSpec((1, _QPAD, sc_w), lambda b, j, pt: (b, 0, j))),
        compiler_params=_params("parallel", "arbitrary"),
        name="sample_scores",
    )(page_table, qi, wi8, kint, *([ki_pool] * _PAGES_SC))

    width = scores.shape[2]
    cols = nb * _QPAD
    bias = pl.pallas_call(
        functools.partial(_sample_select_kernel, past=past, dec_seq=s),
        out_shape=jax.ShapeDtypeStruct((width, cols), F32),
        grid=(cols // LANES,),
        in_specs=[pl.BlockSpec((width, LANES), lambda i: (0, i))],
        out_specs=pl.BlockSpec((width, LANES), lambda i: (0, i)),
        scratch_shapes=[pltpu.VMEM((width, LANES), F32), pltpu.VMEM((_SUBLANES, LANES), F32)],
        compiler_params=_params("parallel"),
        name="sample_select",
    )(scores.reshape(cols, width).T).T.reshape(nb, _QPAD, width)

    q4 = q_b.reshape(nb, s, N_KV_HEADS, Q_PER_KV, HEAD_DIM)[:, dup]
    eye = jnp.eye(N_KV_HEADS, dtype=q_b.dtype)
    qx = jnp.einsum("bqgjd,gk->bgjqkd", q4, eye).reshape(nb, N_HEADS * _QPAD, KV_W)
    n_kv = n_pages // _PAGES_KV
    kv_w = _PAGES_KV * PAGE_SIZE
    new_blk = past // kv_w
    hq = N_HEADS * _QPAD
    y = pl.pallas_call(
        _sample_attend_kernel,
        out_shape=jax.ShapeDtypeStruct((nb, _QPAD, D_MODEL), BF16),
        grid_spec=pltpu.PrefetchScalarGridSpec(
            num_scalar_prefetch=2, grid=(nb, n_kv + 1),
            in_specs=[pl.BlockSpec((1, hq, KV_W), lambda b, j, *_: (b, 0, 0)),
                      pl.BlockSpec((1, _QPAD, kv_w), lambda b, j, *_: (b, 0, jnp.minimum(j, new_blk))),
                      pl.BlockSpec((1, KV_W, PAGE_SIZE), lambda b, j, *_: (b, 0, 0)),
                      pl.BlockSpec((1, KV_W, PAGE_SIZE), lambda b, j, *_: (b, 0, 0)),
                      _lspec(ATTN_W, D_MODEL)]
                     + [page_spec(KV_W, _PAGES_KV, r) for r in range(_PAGES_KV)] * 2,
            out_specs=pl.BlockSpec((1, _QPAD, D_MODEL), lambda b, j, *_: (b, 0, 0)),
            scratch_shapes=[pltpu.VMEM((hq, 1), F32), pltpu.VMEM((hq, 1), F32), pltpu.VMEM((hq, KV_W), F32)]),
        compiler_params=_params("parallel", "arbitrary"),
        name="sample_attend",
    )(page_table, lidx, qx, bias, knt, vnt, w_attn_out, *([k_pool] * _PAGES_KV), *([v_pool] * _PAGES_KV))
    return y[:, :s].reshape(nb * s, D_MODEL)


_ROUTE_W = 8


def _merge_kernel(l_ref, x_ref, yc_ref, ya_ref, g_ref, wout_ref, lg_ref, lb_ref, rwh_ref, rwl_ref,
                  rb_ref, x1_ref, x1b_ref, route_ref, *, alpha):
    mix_in = g_ref[:, :D_MODEL] * yc_ref[...] + g_ref[:, D_MODEL:] * ya_ref[...]
    mix = jnp.dot(mix_in, wout_ref[...], preferred_element_type=F32)
    x1 = _layer_norm(alpha * x_ref[...] + mix, lg_ref[...], lb_ref[...])
    x1_ref[...] = x1
    hi = x1.astype(BF16)
    x1b_ref[...] = hi

    lo = (x1 - hi.astype(F32)).astype(BF16)
    logits = (jnp.dot(hi, rwh_ref[...], preferred_element_type=F32)
              + jnp.dot(hi, rwl_ref[...], preferred_element_type=F32)
              + jnp.dot(lo, rwh_ref[...], preferred_element_type=F32)) + rb_ref[...]
    lane = lax.broadcasted_iota(I32, (1, LANES), 1)

    def first_max(v):
        m = jnp.max(v, axis=1, keepdims=True)
        return m, jnp.min(jnp.where(v == m, lane, LANES), axis=1, keepdims=True)

    is_group = (lane >= N_EXPERTS) & (lane < N_EXPERTS + N_GROUPS)
    gl = jnp.where(is_group, logits, -jnp.inf)
    gm, gidx = first_max(gl)
    p_grp = 1.0 / jnp.sum(jnp.exp(gl - gm), axis=1, keepdims=True)
    grp = gidx - N_EXPERTS
    group_of_lane = lane >> (EXP_PER_GROUP.bit_length() - 1)
    el = jnp.where(group_of_lane == grp, logits, -jnp.inf)
    e1, i1 = first_max(el)
    e2, i2 = first_max(jnp.where(lane == i1, -jnp.inf, el))
    t = jnp.exp(e2 - e1)
    g1 = p_grp / (1.0 + t)
    g2 = g1 * t
    rl = lax.broadcasted_iota(I32, (1, _ROUTE_W), 1)
    route_ref[...] = jnp.where(rl == 0, i1.astype(F32), jnp.where(rl == 1, i2.astype(F32),
                               jnp.where(rl == 2, g1, jnp.where(rl == 3, g2, 0.0))))


def _merge(lidx, x2d, yc, ya, g, p, tm, alpha):
    t = x2d.shape[0]
    row = lambda n: pl.BlockSpec((tm, n), lambda i, l: (i, 0))
    return pl.pallas_call(
        functools.partial(_merge_kernel, alpha=alpha),
        out_shape=[jax.ShapeDtypeStruct((t, D_MODEL), F32), jax.ShapeDtypeStruct((t, D_MODEL), BF16),
                   jax.ShapeDtypeStruct((t, _ROUTE_W), F32)],
        grid_spec=pltpu.PrefetchScalarGridSpec(
            num_scalar_prefetch=1, grid=(t // tm,),
            in_specs=[row(D_MODEL), row(D_MODEL), row(D_MODEL), row(2 * D_MODEL),
                      _lspec(D_MODEL, D_MODEL), _lspec(1, D_MODEL),
                      _lspec(1, D_MODEL), _lspec(D_MODEL, LANES), _lspec(D_MODEL, LANES), _lspec(1, LANES)],
            out_specs=[row(D_MODEL), row(D_MODEL), row(_ROUTE_W)]),
        compiler_params=_params("parallel"),
        name="merge_ln1_router",
    )(lidx, x2d, yc, ya, g, p["w_out"], p["ln1_g"], p["ln1_b"], p["router_hi"],
      p["router_lo"], p["router_b"])


_SLOT_ALIGN = SUBLANES_BF16
_ROW_BLK = 128
_EXPERTS_PER_STEP = 4
_MOE_CHUNK = 256
_META_W = LANES
_MOE_TILE_MAX = 256 * _SLOT_ALIGN


def _moe_plan_kernel(route_ref, dcol_ref, drow_ref, meta_ref):
    tm = route_ref.shape[0]
    blk_rows = min(_MOE_CHUNK, tm)
    lane = lax.broadcasted_iota(I32, (1, LANES), 1).astype(F32)
    hit0 = lane == route_ref[:, 0:1]
    hit1 = lane == route_ref[:, 1:2]
    onehot = jnp.where(hit0, 1.0, jnp.where(hit1, 1.0, 0.0))
    r = lax.broadcasted_iota(I32, (blk_rows, blk_rows), 0)
    c = lax.broadcasted_iota(I32, (blk_rows, blk_rows), 1)
    tri = jnp.where(c < r, 1.0, 0.0).astype(BF16)
    carry = jnp.zeros((1, LANES), F32)
    prefix = []
    for b0 in range(0, tm, blk_rows):
        blk = onehot[b0:b0 + blk_rows]
        prefix.append(jnp.dot(tri, blk.astype(BF16), preferred_element_type=F32) + carry)
        carry = carry + jnp.sum(blk, axis=0, keepdims=True)
    prefix = jnp.concatenate(prefix, axis=0)
    units = jnp.ceil(carry * (1.0 / _SLOT_ALIGN))
    rr = lax.broadcasted_iota(I32, (LANES, LANES), 0)
    cc = lax.broadcasted_iota(I32, (LANES, LANES), 1)
    upper = jnp.where(rr < cc, 1.0, 0.0).astype(BF16)
    units8 = jnp.broadcast_to(units, (8, LANES)).astype(BF16)
    offs = jnp.dot(units8, upper, preferred_element_type=F32)[0:1] * _SLOT_ALIGN
    slot = offs + prefix
    d0 = jnp.sum(jnp.where(hit0, slot, 0.0), axis=1, keepdims=True)
    d1 = jnp.sum(jnp.where(hit1, slot, 0.0), axis=1, keepdims=True)
    rl = lax.broadcasted_iota(I32, (1, LANES), 1)
    rec = jnp.where(rl == 0, d0, jnp.where(rl == 1, d1, jnp.where(rl == 2, route_ref[:, 2:3],
                    jnp.where(rl == 3, route_ref[:, 3:4], 0.0))))
    dcol_ref[...] = rec[:, :_ROUTE_W]
    drow_ref[...] = rec.T[:_ROUTE_W, :]
    nblk = jnp.ceil(units * (_SLOT_ALIGN / _ROW_BLK))
    shift = jnp.where(cc == rr + N_EXPERTS, 1.0, 0.0).astype(BF16)
    nblk_sh = jnp.dot(jnp.broadcast_to(nblk, (8, LANES)).astype(BF16), shift,
                      preferred_element_type=F32)[0:1]
    meta_ref[0] = jnp.where(rl < N_EXPERTS, offs, nblk_sh).astype(I32)


def _moe_plan(route, tm):
    t = route.shape[0]
    nt = t // tm
    assert tm <= _MOE_TILE_MAX
    return pl.pallas_call(
        _moe_plan_kernel,
        out_shape=[jax.ShapeDtypeStruct((t, _ROUTE_W), F32), jax.ShapeDtypeStruct((_ROUTE_W, t), F32),
                   jax.ShapeDtypeStruct((nt, 1, _META_W), I32)],
        grid=(nt,),
        in_specs=[pl.BlockSpec((tm, _ROUTE_W), lambda i: (i, 0))],
        out_specs=[pl.BlockSpec((tm, _ROUTE_W), lambda i: (i, 0)),
                   pl.BlockSpec((_ROUTE_W, tm), lambda i: (0, i)),
                   pl.BlockSpec((1, 1, _META_W), lambda i: (i, 0, 0))],
        compiler_params=_params("parallel"),
        name="moe_plan",
    )(route)


def _n_slots(tm):
    n = 2 * tm + N_EXPERTS * (_SLOT_ALIGN - 1) + _ROW_BLK
    return -(-n // _MOE_CHUNK) * _MOE_CHUNK


def _moe_kernel(meta_ref, l_ref, x1b_ref, x1_ref, drow_ref, dcol_ref, wg_ref, wu_ref, wd_ref, lg_ref, lb_ref,
                x2_ref, xb_ref, yb_ref, gs_ref, *, alpha):
    i, e = pl.program_id(0), pl.program_id(1)
    tm = x1_ref.shape[0]
    n_slots = xb_ref.shape[0]

    @pl.when(e == 0)
    def _dispatch():
        d0, d1 = drow_ref[0:1, :], drow_ref[1:2, :]
        g0, g1 = drow_ref[2:3, :], drow_ref[3:4, :]
        for s0 in range(0, n_slots, _MOE_CHUNK):
            sl = (s0 + lax.broadcasted_iota(I32, (_MOE_CHUNK, 1), 0)).astype(F32)
            a, b = sl == d0, sl == d1
            p = jnp.where(a, 1.0, jnp.where(b, 1.0, 0.0)).astype(BF16)
            xb_ref[s0:s0 + _MOE_CHUNK, :] = jnp.dot(p, x1b_ref[...], preferred_element_type=F32).astype(BF16)
            gs_ref[s0:s0 + _MOE_CHUNK, :] = jnp.sum(jnp.where(a, g0, jnp.where(b, g1, 0.0)), axis=1,
                                                    keepdims=True)
        yb_ref[...] = jnp.zeros(yb_ref.shape, BF16)

    def block(k, start):
        start = pl.multiple_of(start, _SLOT_ALIGN)
        xs = xb_ref[pl.ds(start, _ROW_BLK), :]
        hg = jnp.dot(xs, wg_ref[k], preferred_element_type=F32)
        hu = jnp.dot(xs, wu_ref[k], preferred_element_type=F32)
        h = (hg * jax.nn.sigmoid(hg)) * hu
        y = jnp.dot(h.astype(BF16), wd_ref[k], preferred_element_type=F32)
        yb_ref[pl.ds(start, _ROW_BLK), :] = (y * gs_ref[pl.ds(start, _ROW_BLK), :]).astype(BF16)

    offs = [meta_ref[i * _META_W + e * _EXPERTS_PER_STEP + k] for k in range(_EXPERTS_PER_STEP)]
    nblks = [meta_ref[i * _META_W + N_EXPERTS + e * _EXPERTS_PER_STEP + k] for k in range(_EXPERTS_PER_STEP)]
    one_each = functools.reduce(jnp.logical_and, [n <= 1 for n in nblks])

    @pl.when(one_each)
    def _single_blocks():
        for k in range(_EXPERTS_PER_STEP):
            block(k, offs[k])

    @pl.when(jnp.logical_not(one_each))
    def _ordered_blocks():
        for k in range(_EXPERTS_PER_STEP):
            lax.fori_loop(0, nblks[k], lambda r, carry, k=k: (block(k, offs[k] + r * _ROW_BLK), carry)[1], 0)

    @pl.when(e == pl.num_programs(1) - 1)
    def _combine():
        lane = lax.broadcasted_iota(I32, (1, n_slots), 1).astype(F32)
        rows = min(_MOE_CHUNK, tm)
        for t0 in range(0, tm, rows):
            d0, d1 = dcol_ref[t0:t0 + rows, 0:1], dcol_ref[t0:t0 + rows, 1:2]
            qm = jnp.where(lane == d0, 1.0, jnp.where(lane == d1, 1.0, 0.0)).astype(BF16)
            y = jnp.dot(qm, yb_ref[...], preferred_element_type=F32)
            z = alpha * x1_ref[t0:t0 + rows, :] + y
            x2_ref[t0:t0 + rows, :] = _layer_norm(z, lg_ref[...], lb_ref[...])


def _moe(lidx, x1, x1b, drow, dcol, meta, p, tm, alpha):
    t = x1.shape[0]
    n_slots = _n_slots(tm)
    tile = lambda shape, imap: pl.BlockSpec(shape, imap)
    return pl.pallas_call(
        functools.partial(_moe_kernel, alpha=alpha),
        out_shape=jax.ShapeDtypeStruct((t, D_MODEL), F32),
        grid_spec=pltpu.PrefetchScalarGridSpec(
            num_scalar_prefetch=2, grid=(t // tm, N_EXPERTS // _EXPERTS_PER_STEP),
            in_specs=[tile((tm, D_MODEL), lambda i, e, m, l: (i, 0)),
                      tile((tm, D_MODEL), lambda i, e, m, l: (i, 0)),
                      tile((_ROUTE_W, tm), lambda i, e, m, l: (0, i)),
                      tile((tm, _ROUTE_W), lambda i, e, m, l: (i, 0)),
                      tile((None, _EXPERTS_PER_STEP, D_MODEL, D_EXPERT), lambda i, e, m, l: (l[0], e, 0, 0)),
                      tile((None, _EXPERTS_PER_STEP, D_MODEL, D_EXPERT), lambda i, e, m, l: (l[0], e, 0, 0)),
                      tile((None, _EXPERTS_PER_STEP, D_EXPERT, D_MODEL), lambda i, e, m, l: (l[0], e, 0, 0)),
                      _lspec(1, D_MODEL), _lspec(1, D_MODEL)],
            out_specs=tile((tm, D_MODEL), lambda i, e, m, l: (i, 0)),
            scratch_shapes=[pltpu.VMEM((n_slots, D_MODEL), BF16), pltpu.VMEM((n_slots, D_MODEL), BF16),
                            pltpu.VMEM((n_slots, 1), F32)]),
        compiler_params=_params("parallel", "arbitrary"),
        name="moe_experts_ln2",
    )(meta.reshape(-1), lidx, x1b, x1, drow, dcol, p["w_gate"], p["w_up"], p["w_down"], p["ln2_g"], p["ln2_b"])


def _token_tile(t, want):
    return want if t % want == 0 else t


def _decoder_layer(lidx, x, prefix, attend, p, alpha, key_transposed):
    nb, s, _ = x.shape
    t = nb * s
    x2d = x.reshape(t, D_MODEL)
    u, q_b, k, v, qi_b, ki, wi, g, k_b, v_b, ki_b = _inproj(lidx, x2d, p["w_in"], p["b_in"], _token_tile(t, 512),
                                                            (nb, s) if key_transposed else None)

    ctm = 512 if s % 512 == 0 else _HALO
    u3 = u.reshape(nb, s, C_CONV)
    history = jnp.pad(prefix, ((0, 0), (_HALO - (CONV_W - 1), 0), (0, 0)))
    u_rows = jnp.pad(u3, ((0, 0), (0, (-s) % ctm), (0, 0)))
    yc = _conv_branch(lidx, history, u_rows, p["conv_w"], p["conv_b"], p["conv_ln_g"], p["conv_ln_b"],
                      p["w_conv_out"], ctm)
    yc = yc[:, :s].reshape(t, D_MODEL)

    ya = attend(q_b, k_b, v_b, qi_b, ki_b, wi, lidx=lidx, w_attn_out=p["w_attn_out"])
    x1, x1b, route = _merge(lidx, x2d, yc, ya, g, p, _token_tile(t, 256), alpha)
    mtm = _token_tile(t, 1024)
    dcol, drow, meta = _moe_plan(route, mtm)
    x2 = _moe(lidx, x1, x1b, drow, dcol, meta, p, mtm, alpha)

    new_conv = jnp.concatenate([prefix, u3], axis=1)[:, -(CONV_W - 1):]
    if key_transposed:
        per_head = lambda a: a.reshape(nb, N_KV_HEADS, HEAD_DIM, s).transpose(0, 3, 1, 2)
    else:
        per_head = lambda a: a.reshape(nb, s, N_KV_HEADS, HEAD_DIM)
    return x2.reshape(nb, s, D_MODEL), per_head(k), per_head(v), ki.reshape(nb, s, IDX_DIM), new_conv


def _prepare_params(w_in, b_in, conv_w, conv_b, conv_ln_g, conv_ln_b, w_conv_out, w_attn_out, w_out,
                    ln1_g, ln1_b, router_group_w, router_group_b, router_expert_w, router_expert_b,
                    w_gate, w_up, w_down, ln2_g, ln2_b):
    depth = w_in.shape[0]
    pad = _COL_G - _N_IN_HEAD
    w_pad = jnp.concatenate([w_in[..., :_N_IN_HEAD], jnp.zeros((depth, D_MODEL, pad), F32),
                             w_in[..., _N_IN_HEAD:]], axis=-1).astype(BF16)
    b_pad = jnp.concatenate([b_in[..., :_N_IN_HEAD], jnp.zeros((depth, pad), F32),
                             b_in[..., _N_IN_HEAD:]], axis=-1)[:, None, :]
    rpad = LANES - N_EXPERTS - N_GROUPS
    rw = jnp.concatenate([router_expert_w, router_group_w, jnp.zeros((depth, D_MODEL, rpad), F32)], axis=-1)
    rb = jnp.concatenate([router_expert_b, router_group_b, jnp.zeros((depth, rpad), F32)], axis=-1)
    rw_hi = rw.astype(BF16)
    rw_lo = (rw - rw_hi.astype(F32)).astype(BF16)
    vec = lambda a: a[:, None, :]
    return dict(w_in=w_pad, b_in=b_pad, conv_w=conv_w, conv_b=vec(conv_b), conv_ln_g=vec(conv_ln_g),
                conv_ln_b=vec(conv_ln_b), w_conv_out=w_conv_out.astype(BF16),
                w_attn_out=w_attn_out.astype(BF16), w_out=w_out.astype(BF16), ln1_g=vec(ln1_g),
                ln1_b=vec(ln1_b), router_hi=rw_hi, router_lo=rw_lo, router_b=vec(rb),
                w_gate=w_gate.astype(BF16), w_up=w_up.astype(BF16), w_down=w_down.astype(BF16),
                ln2_g=vec(ln2_g), ln2_b=vec(ln2_b))


def kernel(x_prompt, x_sample, cache_k, cache_v, cache_kidx, state_conv, page_table, w_in, b_in, conv_w,
           conv_b, conv_ln_g, conv_ln_b, w_conv_out, w_attn_out, w_out, ln1_g, ln1_b, router_group_w,
           router_group_b, router_expert_w, router_expert_b, w_gate, w_up, w_down, ln2_g, ln2_b):
    params = _prepare_params(w_in, b_in, conv_w, conv_b, conv_ln_g, conv_ln_b, w_conv_out, w_attn_out,
                             w_out, ln1_g, ln1_b, router_group_w, router_group_b, router_expert_w,
                             router_expert_b, w_gate, w_up, w_down, ln2_g, ln2_b)
    nb, s, _ = x_prompt.shape
    db, ds, _ = x_sample.shape
    depth, n_phys = cache_k.shape[:2]
    alpha = (2 * depth) ** 0.25
    k_pool = cache_k.transpose(0, 1, 3, 4, 2).reshape(depth * n_phys, KV_W, PAGE_SIZE)
    v_pool = cache_v.transpose(0, 1, 3, 4, 2).reshape(depth * n_phys, KV_W, PAGE_SIZE)
    ki_pool = cache_kidx.transpose(0, 1, 3, 2).reshape(depth * n_phys, IDX_DIM, PAGE_SIZE)
    conv_zero = jnp.zeros((nb, CONV_W - 1, C_CONV), F32)

    def layer(carry, xs):
        xp, xsm = carry
        l, st = xs
        lidx = l.reshape(1)
        prompt_attend = functools.partial(_dsa_prompt, nb=nb, s=s)
        xp, *new_p = _decoder_layer(lidx, xp, conv_zero, prompt_attend, params, alpha, key_transposed=True)
        sample_attend = functools.partial(_dsa_sample, k_pool=k_pool, v_pool=v_pool, ki_pool=ki_pool,
                                          page_table=page_table + l * n_phys, nb=db, s=ds)
        xsm, *new_s = _decoder_layer(lidx, xsm, st, sample_attend, params, alpha, key_transposed=False)
        return (xp, xsm), (tuple(new_p), tuple(new_s))

    (xp, xsm), (new_p, new_s) = lax.scan(layer, (x_prompt, x_sample),
                                         (jnp.arange(depth, dtype=I32), state_conv))
    return (xp, xsm, *new_p, *new_s)
```
